```python
import math
import jax, jax.numpy as jnp
from jax import lax
import numpy as np

D_MODEL = 2048
BATCH = 4
SEQ = 8192
DEPTH = 4
DEC_BATCH = 1
DEC_SEQ = 8192
PAST_LEN = 128

GRID_W = 64
N_MIXERS = 3
N_HEADS = 16
N_KV_HEADS = 4
GQA_GROUP = N_HEADS // N_KV_HEADS
HEAD_DIM = 128
Q_BLOCK = 128
ROPE_THETA = 10000.0
AXIS_ROT_DIM = HEAD_DIM // 2
QKV_DIM = (N_HEADS + 2 * N_KV_HEADS) * HEAD_DIM
POOL_WINDOWS = (2, 4, 8, 16)
POOL_GROUP = D_MODEL // len(POOL_WINDOWS)
SSD_EXPAND = 2
D_INNER = SSD_EXPAND * D_MODEL
SSD_HEAD_DIM = 64
SSD_HEADS = D_INNER // SSD_HEAD_DIM
SSD_GROUPS = 8
D_STATE = 128
D_CONV = 4
CONV_LEFT = D_CONV // 2
SSD_CHUNK = 128
CONV_DIM = D_INNER + 2 * SSD_GROUPS * D_STATE
SSD_IN_DIM = D_INNER + CONV_DIM + 2 * SSD_HEADS
N_EXPERTS = 16
EC_CAPACITY_FACTOR = 2
D_FF_EXPERT = D_MODEL // 2
LN_EPS = 1e-5
RMS_EPS = 1e-6
DEEPNORM_ALPHA = (2 * DEPTH) ** 0.25
DEEPNORM_BETA = (8 * DEPTH) ** -0.25
N_ATTN = (DEPTH + 2) // 3
N_POOL = (DEPTH + 1) // 3
N_SSD = DEPTH // 3

kernel_name = "hybrid_attn_pool_ssd_ec_moe_encoder"

F32 = jnp.float32


def layer_norm(x, g, b):
    xf = x.astype(F32)
    mu = jnp.mean(xf, axis=-1, keepdims=True)
    xc = xf - mu
    var = jnp.mean(xc * xc, axis=-1, keepdims=True)
    return (xc * lax.rsqrt(var + LN_EPS) * g.astype(F32) + b.astype(F32)).astype(x.dtype)


def rms_norm_f32(x, g):
    xf = x.astype(F32)
    return xf * lax.rsqrt(jnp.mean(xf * xf, axis=-1, keepdims=True) + RMS_EPS) * g.astype(F32)


def axial_rope_tables(seq_len):
    rows = seq_len // GRID_W
    row = jnp.repeat(jnp.arange(rows, dtype=F32), GRID_W)
    col = jnp.tile(jnp.arange(GRID_W, dtype=F32), rows)
    inv_freq = ROPE_THETA ** (-jnp.arange(0, AXIS_ROT_DIM, 2, dtype=F32) / AXIS_ROT_DIM)
    ang_row = row[:, None] * inv_freq
    ang_col = col[:, None] * inv_freq
    return (jnp.cos(ang_row), jnp.sin(ang_row), jnp.cos(ang_col), jnp.sin(ang_col))


def rotate_axis(x, cos, sin):
    half = x.shape[-1] // 2
    x1, x2 = x[..., :half], x[..., half:]
    c = cos[None, :, None, :]
    s = sin[None, :, None, :]
    return jnp.concatenate([x1 * c - x2 * s, x2 * c + x1 * s], axis=-1)


def apply_axial_rope(x, rope):
    cos_r, sin_r, cos_c, sin_c = rope
    return jnp.concatenate([rotate_axis(x[..., :AXIS_ROT_DIM], cos_r, sin_r),
                            rotate_axis(x[..., AXIS_ROT_DIM:], cos_c, sin_c)], axis=-1)


def attention_mixer(x, w_qkv, q_norm, k_norm, w_o, rope):
    b, s, _ = x.shape
    qkv = x @ w_qkv
    q, k, v = jnp.split(qkv, [N_HEADS * HEAD_DIM, (N_HEADS + N_KV_HEADS) * HEAD_DIM], axis=-1)
    q = apply_axial_rope(rms_norm_f32(q.reshape(b, s, N_HEADS, HEAD_DIM), q_norm), rope)
    k = apply_axial_rope(rms_norm_f32(k.reshape(b, s, N_KV_HEADS, HEAD_DIM), k_norm), rope)
    q = (q * (HEAD_DIM ** -0.5)).astype(x.dtype)
    k = k.astype(x.dtype)
    v = v.reshape(b, s, N_KV_HEADS, HEAD_DIM)
    nb = s // Q_BLOCK
    qb = q.reshape(b, nb, Q_BLOCK, N_KV_HEADS, GQA_GROUP, HEAD_DIM).transpose(1, 0, 2, 3, 4, 5)

    def block(qblk):
        sc = jnp.einsum('bqkgd,bskd->bkgqs', qblk, k, preferred_element_type=F32)
        p = jax.nn.softmax(sc, axis=-1).astype(v.dtype)
        return jnp.einsum('bkgqs,bskd->bqkgd', p, v)

    o = lax.map(block, qb)
    o = o.transpose(1, 0, 2, 3, 4, 5).reshape(b, s, N_HEADS * HEAD_DIM)
    return o @ w_o


def pool_mixer(x, w_pool, scale):
    b, s, d = x.shape
    xf = x.astype(F32)
    cs = jnp.concatenate([jnp.zeros((b, 1, d), F32), jnp.cumsum(xf, axis=1)], axis=1)
    t = jnp.arange(s)
    outs = []
    for gi, w in enumerate(POOL_WINDOWS):
        sl = slice(gi * POOL_GROUP, (gi + 1) * POOL_GROUP)
        lo = jnp.clip(t - w // 2, 0, s)
        hi = jnp.clip(t + w // 2, 0, s)
        cnt = (hi - lo).astype(F32)
        csg = cs[:, :, sl]
        mean = (csg[:, hi] - csg[:, lo]) / cnt[None, :, None]
        mixed = (mean - xf[:, :, sl]).astype(x.dtype)
        outs.append(mixed @ w_pool[gi])
    return jnp.concatenate(outs, axis=-1) * scale


def depthwise_conv_centred(u, w, bias):
    s = u.shape[1]
    up = jnp.pad(u, ((0, 0), (CONV_LEFT, D_CONV - 1 - CONV_LEFT), (0, 0)))
    out = up[:, 0:s] * w[0]
    for kk in range(1, D_CONV):
        out = out + up[:, kk:kk + s] * w[kk]
    return out + bias


def ssd_scan(x, dt, A, Bm, Cm):
    b, s, h, p = x.shape
    g, n = Bm.shape[2], Bm.shape[3]
    r = h // g
    q = SSD_CHUNK
    c = s // q
    X = (x * dt[..., None]).reshape(b, c, q, g, r, p)
    Acs = jnp.cumsum((dt * A).reshape(b, c, q, g, r), axis=2)
    Bc = Bm.reshape(b, c, q, g, n)
    Cc = Cm.reshape(b, c, q, g, n)
    Acs_t = Acs.transpose(0, 1, 3, 4, 2)
    mask = jnp.tril(jnp.ones((q, q), bool))
    L = jnp.exp(jnp.where(mask, Acs_t[..., :, None] - Acs_t[..., None, :], -jnp.inf))
    CB = jnp.einsum('bclgn,bcsgn->bcgls', Cc, Bc)
    y_diag = jnp.einsum('bcgrls,bcsgrp->bclgrp', L * CB[:, :, :, None], X)
    decay_states = jnp.exp(Acs[:, :, -1:] - Acs)
    states = jnp.einsum('bcsgn,bcsgrp->bcgrpn', Bc, X * decay_states[..., None])
    chunk_decay = jnp.exp(Acs[:, :, -1])

    def step(carry, inp):
        st, dec = inp
        return carry * dec[..., None, None] + st, carry

    init = jnp.zeros((b, g, r, p, n), F32)
    _, prev = lax.scan(step, init, (jnp.swapaxes(states, 0, 1), jnp.swapaxes(chunk_decay, 0, 1)))
    y_off = jnp.einsum('bclgn,cbgrpn->bclgrp', Cc, prev) * jnp.exp(Acs)[..., None]
    return (y_diag + y_off).reshape(b, s, h, p)


def ssd_mixer(x, w_in, conv_w, conv_b, dt_bias, A_log, D_skip, norm_w, w_out):
    b, s, _ = x.shape
    zxbcdt = x @ w_in
    z, xBC, dt_raw = jnp.split(zxbcdt, [D_INNER, D_INNER + CONV_DIM], axis=-1)
    xBC = jax.nn.silu(depthwise_conv_centred(xBC, conv_w, conv_b))
    xs, Bm, Cm = jnp.split(xBC, [D_INNER, D_INNER + SSD_GROUPS * D_STATE], axis=-1)
    xs = xs.astype(F32).reshape(b, s, SSD_HEADS, SSD_HEAD_DIM)
    Bm = Bm.astype(F32).reshape(b, s, SSD_GROUPS, D_STATE)
    Cm = Cm.astype(F32).reshape(b, s, SSD_GROUPS, D_STATE)
    dt = jax.nn.softplus(dt_raw.astype(F32).reshape(b, s, 2, SSD_HEADS) + dt_bias.astype(F32))
    A = -jnp.exp(A_log.astype(F32))
    y_f = ssd_scan(xs, dt[:, :, 0], A[0], Bm, Cm)
    flip = lambda t: jnp.flip(t, axis=1)
    y_b = flip(ssd_scan(flip(xs), flip(dt[:, :, 1]), A[1], flip(Bm), flip(Cm)))
    y = y_f + y_b + D_skip.astype(F32)[:, None] * xs
    y = y.reshape(b, s, D_INNER) * jax.nn.silu(z.astype(F32))
    y = rms_norm_f32(y, norm_w)
    return y.astype(x.dtype) @ w_out


def ec_moe(x, w_router, w_gate, w_up, w_down):
    b, s, d = x.shape
    n = b * s
    cap = EC_CAPACITY_FACTOR * n // N_EXPERTS
    xf = x.reshape(n, d)
    aff = jax.nn.softmax((xf @ w_router).astype(F32), axis=-1)
    gate, idx = lax.top_k(aff.T, cap)
    xs = xf[idx]
    hdn = jax.nn.silu(jnp.einsum('ecd,edf->ecf', xs, w_gate)) * jnp.einsum('ecd,edf->ecf', xs, w_up)
    o = jnp.einsum('ecf,efd->ecd', hdn, w_down) * gate[..., None].astype(x.dtype)
    y = jnp.zeros_like(xf).at[idx.reshape(-1)].add(o.reshape(-1, d))
    return y.reshape(b, s, d)


def trunk(x, attn_p, pool_p, ssd_p, moe_p, ln_g, ln_b):
    attn_w_qkv, attn_q_norm, attn_k_norm, attn_w_o = attn_p
    pool_w, pool_scale = pool_p
    ssd_w_in, ssd_conv_w, ssd_conv_b, ssd_dt_bias, ssd_A_log, ssd_D, ssd_norm, ssd_w_out = ssd_p
    moe_w_router, moe_w_gate, moe_w_up, moe_w_down = moe_p
    rope = axial_rope_tables(x.shape[1])
    ia, ip, isd = 0, 0, 0
    for i in range(DEPTH):
        kind = i % N_MIXERS
        if kind == 0:
            h = attention_mixer(x, attn_w_qkv[ia], attn_q_norm[ia], attn_k_norm[ia], attn_w_o[ia], rope)
            ia += 1
        elif kind == 1:
            h = pool_mixer(x, pool_w[ip], pool_scale[ip])
            ip += 1
        else:
            h = ssd_mixer(x, ssd_w_in[isd], ssd_conv_w[isd], ssd_conv_b[isd], ssd_dt_bias[isd],
                          ssd_A_log[isd], ssd_D[isd], ssd_norm[isd], ssd_w_out[isd])
            isd += 1
        x = layer_norm(DEEPNORM_ALPHA * x + h, ln_g[i, 0], ln_b[i, 0])
        m = ec_moe(x, moe_w_router[i], moe_w_gate[i], moe_w_up[i], moe_w_down[i])
        x = layer_norm(DEEPNORM_ALPHA * x + m, ln_g[i, 1], ln_b[i, 1])
    return x


def setup_inputs(seed: int = 0) -> dict:
    key = jax.random.key(seed)
    ks = jax.random.split(key, 24)
    nrm = jax.random.normal
    beta = DEEPNORM_BETA
    u_dt = jax.random.uniform(ks[13], (N_SSD, 2, SSD_HEADS))
    dt0 = jnp.exp(u_dt * (math.log(0.1) - math.log(0.001)) + math.log(0.001))
    return {
        'x_prompt': nrm(ks[0], (BATCH, SEQ, D_MODEL), F32),
        'x_sample': nrm(ks[1], (DEC_BATCH, DEC_SEQ, D_MODEL), F32),
        'attn_w_qkv': nrm(ks[2], (N_ATTN, D_MODEL, QKV_DIM), F32) * D_MODEL ** -0.5,
        'attn_q_norm': 1.0 + 0.02 * nrm(ks[3], (N_ATTN, HEAD_DIM), F32),
        'attn_k_norm': 1.0 + 0.02 * nrm(ks[4], (N_ATTN, HEAD_DIM), F32),
        'attn_w_o': nrm(ks[5], (N_ATTN, N_HEADS * HEAD_DIM, D_MODEL), F32) * (N_HEADS * HEAD_DIM) ** -0.5 * beta,
        'pool_w': nrm(ks[6], (N_POOL, len(POOL_WINDOWS), POOL_GROUP, POOL_GROUP), F32) * POOL_GROUP ** -0.5 * beta,
        'pool_scale': 1.0 + 0.02 * nrm(ks[7], (N_POOL, D_MODEL), F32),
        'ssd_w_in': nrm(ks[8], (N_SSD, D_MODEL, SSD_IN_DIM), F32) * D_MODEL ** -0.5,
        'ssd_conv_w': nrm(ks[9], (N_SSD, D_CONV, CONV_DIM), F32) * D_CONV ** -0.5,
        'ssd_conv_b': 0.01 * nrm(ks[10], (N_SSD, CONV_DIM), F32),
        'ssd_dt_bias': dt0 + jnp.log(-jnp.expm1(-dt0)),
        'ssd_A_log': jnp.log(jax.random.uniform(ks[11], (N_SSD, 2, SSD_HEADS), F32, 1.0, 16.0)),
        'ssd_D': 1.0 + 0.02 * nrm(ks[12], (N_SSD, SSD_HEADS), F32),
        'ssd_norm': 1.0 + 0.02 * nrm(ks[14], (N_SSD, D_INNER), F32),
        'ssd_w_out': nrm(ks[15], (N_SSD, D_INNER, D_MODEL), F32) * D_INNER ** -0.5 * beta,
        'moe_w_router': nrm(ks[16], (DEPTH, D_MODEL, N_EXPERTS), F32) * D_MODEL ** -0.5,
        'moe_w_gate': nrm(ks[17], (DEPTH, N_EXPERTS, D_MODEL, D_FF_EXPERT), F32) * D_MODEL ** -0.5,
        'moe_w_up': nrm(ks[18], (DEPTH, N_EXPERTS, D_MODEL, D_FF_EXPERT), F32) * D_MODEL ** -0.5,
        'moe_w_down': nrm(ks[19], (DEPTH, N_EXPERTS, D_FF_EXPERT, D_MODEL), F32) * D_FF_EXPERT ** -0.5 * beta,
        'ln_g': 1.0 + 0.02 * nrm(ks[20], (DEPTH, 2, D_MODEL), F32),
        'ln_b': 0.02 * nrm(ks[21], (DEPTH, 2, D_MODEL), F32),
    }


def reference(x_prompt, x_sample, attn_w_qkv, attn_q_norm, attn_k_norm, attn_w_o,
              pool_w, pool_scale, ssd_w_in, ssd_conv_w, ssd_conv_b, ssd_dt_bias, ssd_A_log,
              ssd_D, ssd_norm, ssd_w_out, moe_w_router, moe_w_gate, moe_w_up, moe_w_down,
              ln_g, ln_b):
    attn_p = (attn_w_qkv, attn_q_norm, attn_k_norm, attn_w_o)
    pool_p = (pool_w, pool_scale)
    ssd_p = (ssd_w_in, ssd_conv_w, ssd_conv_b, ssd_dt_bias, ssd_A_log, ssd_D, ssd_norm, ssd_w_out)
    moe_p = (moe_w_router, moe_w_gate, moe_w_up, moe_w_down)
    y_prompt = trunk(x_prompt, attn_p, pool_p, ssd_p, moe_p, ln_g, ln_b)
    y_sample = trunk(x_sample, attn_p, pool_p, ssd_p, moe_p, ln_g, ln_b)
    return (y_prompt, y_sample)
```

```python
import functools
import math

import jax
import jax.numpy as jnp
from jax import lax
from jax.experimental import pallas as pl
from jax.experimental.pallas import tpu as pltpu

F32 = jnp.float32
BF16 = jnp.bfloat16

HEAD_DIM = 128
GRID_W = 64
ROPE_THETA = 10000.0
POOL_WINDOWS = (2, 4, 8, 16)
POOL_HALO = 8
D_STATE = 128
SSD_CHUNK = 128
SSD_HEAD_DIM = 64
SSD_HEADS_PER_GROUP = 8
D_CONV = 4
CONV_LEFT = D_CONV // 2
EC_CAPACITY_FACTOR = 2
LN_EPS = 1e-5
RMS_EPS = 1e-6
LOG2E = 1.4426950408889634

V7X_VMEM_LIMIT_BYTES = 52 * 1024 * 1024
LANES = 128


def _params(*sem):
    return pltpu.CompilerParams(dimension_semantics=sem, vmem_limit_bytes=V7X_VMEM_LIMIT_BYTES)


def _pick(n, pref):
    t = min(n, pref)
    while n % t:
        t //= 2
    return t


def _res_ln(x, h, g, b, alpha):
    y = alpha * x + h
    mu = jnp.mean(y, axis=-1, keepdims=True)
    yc = y - mu
    var = jnp.mean(yc * yc, axis=-1, keepdims=True)
    return yc * lax.rsqrt(var + LN_EPS) * g + b


def _router_affinity(xn, wr_ref, n_exp):
    xh = xn.astype(BF16)
    xl = (xn - xh.astype(F32)).astype(BF16)
    wr = wr_ref[...]
    r1 = jnp.dot(xh, wr, preferred_element_type=F32)
    r2 = jnp.dot(xl, wr[:, :n_exp], preferred_element_type=F32)
    logits = r1[:, :n_exp] + (r1[:, n_exp:] + r2)
    m = jnp.max(logits, axis=-1, keepdims=True)
    e = jnp.exp(logits - m)
    return e / jnp.sum(e, axis=-1, keepdims=True)


def _qkv_kernel(x_ref, w_ref, cos_ref, sa_ref, sb_ref, qn_ref, kn_ref, o_ref, xb_ref, *,
                n_q_tiles, n_k_tiles, heads_per_tile, q_scale):
    j = pl.program_id(1)

    @pl.when(j == 0)
    def _():
        xb_ref[...] = x_ref[...].astype(BF16)

    acc = jnp.dot(xb_ref[...], w_ref[...], preferred_element_type=F32)

    def norm_rope(gain_ref, scale):
        cos = cos_ref[...]
        sa = sa_ref[...]
        sb = sb_ref[...]
        g = gain_ref[...]
        for h in range(heads_per_tile):
            a = acc[:, h * HEAD_DIM:(h + 1) * HEAD_DIM]
            a = a * lax.rsqrt(jnp.mean(a * a, axis=-1, keepdims=True) + RMS_EPS) * g
            r = a * cos + pltpu.roll(a, HEAD_DIM - 32, 1) * sa + pltpu.roll(a, 32, 1) * sb
            o_ref[:, h * HEAD_DIM:(h + 1) * HEAD_DIM] = (r * scale).astype(BF16)

    @pl.when(j < n_q_tiles)
    def _():
        norm_rope(qn_ref, q_scale)

    @pl.when((j >= n_q_tiles) & (j < n_q_tiles + n_k_tiles))
    def _():
        norm_rope(kn_ref, 1.0)

    @pl.when(j >= n_q_tiles + n_k_tiles)
    def _():
        o_ref[...] = acc.astype(BF16)


def _rope_tables(seq_len):
    rows = seq_len // GRID_W
    row = jnp.repeat(jnp.arange(rows, dtype=F32), GRID_W)
    col = jnp.tile(jnp.arange(GRID_W, dtype=F32), rows)
    half = HEAD_DIM // 4
    inv_freq = ROPE_THETA ** (-jnp.arange(0, HEAD_DIM // 2, 2, dtype=F32) / (HEAD_DIM // 2))
    ang_r = row[:, None] * inv_freq
    ang_c = col[:, None] * inv_freq
    zeros = jnp.zeros((seq_len, half), F32)
    cos = jnp.concatenate([jnp.cos(ang_r)] * 2 + [jnp.cos(ang_c)] * 2, axis=-1)
    sin_a = jnp.concatenate([-jnp.sin(ang_r), zeros, -jnp.sin(ang_c), zeros], axis=-1)
    sin_b = jnp.concatenate([zeros, jnp.sin(ang_r), zeros, jnp.sin(ang_c)], axis=-1)
    return cos, sin_a, sin_b


def _qkv_proj(x2d, w_bf16, q_norm, k_norm, rope, seq_len, n_heads, n_kv):
    n, d = x2d.shape
    qkv_dim = w_bf16.shape[1]
    tn = n_kv * HEAD_DIM
    tm = _pick(seq_len, 512)
    cos, sa, sb = rope
    nsb = seq_len // tm
    kern = functools.partial(
        _qkv_kernel, n_q_tiles=n_heads // n_kv, n_k_tiles=1, heads_per_tile=n_kv,
        q_scale=HEAD_DIM ** -0.5 * LOG2E)
    tab = pl.BlockSpec((tm, HEAD_DIM), lambda i, j: (i % nsb, 0))
    vec = pl.BlockSpec((1, HEAD_DIM), lambda i, j: (0, 0))
    return pl.pallas_call(
        kern,
        grid=(n // tm, qkv_dim // tn),
        in_specs=[pl.BlockSpec((tm, d), lambda i, j: (i, 0)),
                  pl.BlockSpec((d, tn), lambda i, j: (0, j)),
                  tab, tab, tab, vec, vec],
        out_specs=pl.BlockSpec((tm, tn), lambda i, j: (i, j)),
        out_shape=jax.ShapeDtypeStruct((n, qkv_dim), BF16),
        scratch_shapes=[pltpu.VMEM((tm, d), BF16)],
        compiler_params=_params("parallel", "arbitrary"),
        name="qkv_proj",
    )(x2d, w_bf16, cos, sa, sb, q_norm.reshape(1, HEAD_DIM), k_norm.reshape(1, HEAD_DIM))


def _flash_kernel(q_ref, k_ref, v_ref, o_ref, *, tk, group):
    tq = q_ref.shape[1]
    seq = k_ref.shape[1]
    q = jnp.concatenate([q_ref[0, :, g * HEAD_DIM:(g + 1) * HEAD_DIM] for g in range(group)], axis=0)
    rows = group * tq

    def body(c, carry):
        m, l, acc = carry
        off = pl.multiple_of(c * tk, tk)
        k = k_ref[0, pl.ds(off, tk), :]
        v = v_ref[0, pl.ds(off, tk), :]
        s = lax.dot_general(q, k, (((1,), (1,)), ((), ())), preferred_element_type=F32)
        m_new = jnp.maximum(m, jnp.max(s, axis=1, keepdims=True))
        alpha = jnp.exp2(m - m_new)
        p = jnp.exp2(s - m_new)
        l = alpha * l + jnp.sum(p, axis=1, keepdims=True)
        acc = alpha * acc + jnp.dot(p.astype(BF16), v, preferred_element_type=F32)
        return m_new, l, acc

    init = (jnp.full((rows, 1), -jnp.inf, F32), jnp.zeros((rows, 1), F32), jnp.zeros((rows, HEAD_DIM), F32))
    _, l, acc = lax.fori_loop(0, seq // tk, body, init)
    o = acc / l
    for g in range(group):
        o_ref[0, :, g * HEAD_DIM:(g + 1) * HEAD_DIM] = o[g * tq:(g + 1) * tq].astype(BF16)


def _flash_attention(qkv, n_heads, n_kv):
    b, s, _ = qkv.shape
    group = n_heads // n_kv
    tq = _pick(s, 128)
    tk = _pick(s, 512)
    gw = group * HEAD_DIM
    kern = functools.partial(_flash_kernel, tk=tk, group=group)
    return pl.pallas_call(
        kern,
        grid=(b, n_kv, s // tq),
        in_specs=[pl.BlockSpec((1, tq, gw), lambda bi, h, i: (bi, i, h)),
                  pl.BlockSpec((1, s, HEAD_DIM), lambda bi, h, i: (bi, 0, n_heads + h)),
                  pl.BlockSpec((1, s, HEAD_DIM), lambda bi, h, i: (bi, 0, n_heads + n_kv + h))],
        out_specs=pl.BlockSpec((1, tq, gw), lambda bi, h, i: (bi, i, h)),
        out_shape=jax.ShapeDtypeStruct((b, s, n_heads * HEAD_DIM), BF16),
        compiler_params=_params("parallel", "parallel", "arbitrary"),
        name="flash_attention",
    )(qkv, qkv, qkv)


def _mm_ln_kernel(a_ref, w_ref, x_ref, g_ref, b_ref, wr_ref, o_ref, aff_ref, acc_ref, *, alpha, nk, n_exp):
    k = pl.program_id(1)

    @pl.when(k == 0)
    def _():
        acc_ref[...] = jnp.zeros_like(acc_ref)

    acc_ref[...] += jnp.dot(a_ref[...], w_ref[...], preferred_element_type=F32)

    @pl.when(k == nk - 1)
    def _():
        xn = _res_ln(x_ref[...], acc_ref[...], g_ref[...], b_ref[...], alpha)
        o_ref[...] = xn
        aff_ref[...] = _router_affinity(xn, wr_ref, n_exp)


def _mm_res_ln_router(a_bf16, w_bf16, x2d, g, b, wr2, alpha):
    n, kdim = a_bf16.shape
    d = w_bf16.shape[1]
    n_exp = wr2.shape[1] // 2
    tm = _pick(n, 512)
    tk = _pick(kdim, 512)
    nk = kdim // tk
    kern = functools.partial(_mm_ln_kernel, alpha=alpha, nk=nk, n_exp=n_exp)
    row = pl.BlockSpec((1, d), lambda i, k: (0, 0))
    return pl.pallas_call(
        kern,
        grid=(n // tm, nk),
        in_specs=[pl.BlockSpec((tm, tk), lambda i, k: (i, k)),
                  pl.BlockSpec((tk, d), lambda i, k: (k, 0)),
                  pl.BlockSpec((tm, d), lambda i, k: (i, 0)),
                  row, row,
                  pl.BlockSpec((d, 2 * n_exp), lambda i, k: (0, 0))],
        out_specs=[pl.BlockSpec((tm, d), lambda i, k: (i, 0)),
                   pl.BlockSpec((tm, n_exp), lambda i, k: (i, 0))],
        out_shape=[jax.ShapeDtypeStruct((n, d), F32), jax.ShapeDtypeStruct((n, n_exp), F32)],
        scratch_shapes=[pltpu.VMEM((tm, d), F32)],
        compiler_params=_params("parallel", "arbitrary"),
        name="mm_res_ln_router",
    )(a_bf16, w_bf16, x2d, g.reshape(1, d), b.reshape(1, d), wr2)


def _res_ln_kernel(x_ref, y_ref, g_ref, b_ref, o_ref, *, alpha):
    o_ref[...] = _res_ln(x_ref[...], y_ref[...], g_ref[...], b_ref[...], alpha)


def _res_ln_call(x2d, y2d, g, b, alpha):
    n, d = x2d.shape
    tm = _pick(n, 512)
    blk = pl.BlockSpec((tm, d), lambda i: (i, 0))
    row = pl.BlockSpec((1, d), lambda i: (0, 0))
    return pl.pallas_call(
        functools.partial(_res_ln_kernel, alpha=alpha),
        grid=(n // tm,),
        in_specs=[blk, blk, row, row],
        out_specs=blk,
        out_shape=jax.ShapeDtypeStruct((n, d), F32),
        compiler_params=_params("parallel"),
        name="res_ln",
    )(x2d, y2d, g.reshape(1, d), b.reshape(1, d))


def _pool_kernel(prev_ref, cur_ref, next_ref, w_ref, sc_ref, g_ref, b_ref, wr_ref, o_ref, aff_ref, ext_ref, *,
                 alpha, nt, seq_len, n_exp):
    i = pl.program_id(1)
    t = cur_ref.shape[1]
    d = cur_ref.shape[2]
    pg = d // len(POOL_WINDOWS)
    x = cur_ref[0]
    ext_ref[0:POOL_HALO, :] = jnp.where(i == 0, 0.0, prev_ref[0])
    ext_ref[POOL_HALO:POOL_HALO + t, :] = x
    ext_ref[POOL_HALO + t:2 * POOL_HALO + t, :] = jnp.where(i == nt - 1, 0.0, next_ref[0])
    pos = i * t + lax.broadcasted_iota(jnp.int32, (t, 1), 0)
    hs = []
    for gi, w in enumerate(POOL_WINDOWS):
        half = w // 2
        cols = slice(gi * pg, (gi + 1) * pg)
        acc = ext_ref[pl.ds(POOL_HALO - half, t), cols]
        for jj in range(1, w):
            acc = acc + ext_ref[pl.ds(POOL_HALO - half + jj, t), cols]
        cnt = (jnp.minimum(pos + half, seq_len) - jnp.maximum(pos - half, 0)).astype(F32)
        mixed = (acc / cnt - x[:, cols]).astype(BF16)
        hs.append(jnp.dot(mixed, w_ref[gi], preferred_element_type=F32))
    h = jnp.concatenate(hs, axis=-1) * sc_ref[...]
    xn = _res_ln(x, h, g_ref[...], b_ref[...], alpha)
    o_ref[0] = xn
    aff_ref[0] = _router_affinity(xn, wr_ref, n_exp)


def _pool_layer(x3d, w_bf16, scale, g, b, wr2, alpha):
    bsz, s, d = x3d.shape
    n_exp = wr2.shape[1] // 2
    t = _pick(s, 256)
    nt = s // t
    hb = t // POOL_HALO
    last_hb = s // POOL_HALO - 1
    pg = d // len(POOL_WINDOWS)
    kern = functools.partial(_pool_kernel, alpha=alpha, nt=nt, seq_len=s, n_exp=n_exp)
    row = pl.BlockSpec((1, d), lambda bi, i: (0, 0))
    return pl.pallas_call(
        kern,
        grid=(bsz, nt),
        in_specs=[pl.BlockSpec((1, POOL_HALO, d), lambda bi, i: (bi, jnp.maximum(i * hb - 1, 0), 0)),
                  pl.BlockSpec((1, t, d), lambda bi, i: (bi, i, 0)),
                  pl.BlockSpec((1, POOL_HALO, d), lambda bi, i: (bi, jnp.minimum((i + 1) * hb, last_hb), 0)),
                  pl.BlockSpec((len(POOL_WINDOWS), pg, pg), lambda bi, i: (0, 0, 0)),
                  row, row, row,
                  pl.BlockSpec((d, 2 * n_exp), lambda bi, i: (0, 0))],
        out_specs=[pl.BlockSpec((1, t, d), lambda bi, i: (bi, i, 0)),
                   pl.BlockSpec((1, t, n_exp), lambda bi, i: (bi, i, 0))],
        out_shape=[jax.ShapeDtypeStruct((bsz, s, d), F32), jax.ShapeDtypeStruct((bsz, s, n_exp), F32)],
        scratch_shapes=[pltpu.VMEM((t + 2 * POOL_HALO, d), F32)],
        compiler_params=_params("parallel", "parallel"),
        name="pool_mixer",
    )(x3d, x3d, x3d, w_bf16, scale.reshape(1, d), g.reshape(1, d), b.reshape(1, d), wr2)


def _mm_kernel(x_ref, w_ref, o_ref, xb_ref):
    @pl.when(pl.program_id(1) == 0)
    def _():
        xb_ref[...] = x_ref[...].astype(BF16)

    o_ref[...] = jnp.dot(xb_ref[...], w_ref[...], preferred_element_type=F32)


def _matmul_f32(x2d, w_bf16, tn_pref):
    n, d = x2d.shape
    nout = w_bf16.shape[1]
    tm = _pick(n, 512)
    tn = tn_pref
    assert nout % tn == 0
    return pl.pallas_call(
        _mm_kernel,
        grid=(n // tm, nout // tn),
        in_specs=[pl.BlockSpec((tm, d), lambda i, j: (i, 0)),
                  pl.BlockSpec((d, tn), lambda i, j: (0, j))],
        out_specs=pl.BlockSpec((tm, tn), lambda i, j: (i, j)),
        out_shape=jax.ShapeDtypeStruct((n, nout), F32),
        scratch_shapes=[pltpu.VMEM((tm, d), BF16)],
        compiler_params=_params("parallel", "arbitrary"),
        name="ssd_in_proj",
    )(x2d, w_bf16)


def _conv_kernel(prev_ref, cur_ref, next_ref, w_ref, b_ref, o_ref, ext_ref, *, nt):
    i = pl.program_id(1)
    t = cur_ref.shape[1]
    ext_ref[0:POOL_HALO, :] = jnp.where(i == 0, 0.0, prev_ref[0])
    ext_ref[POOL_HALO:POOL_HALO + t, :] = cur_ref[0]
    ext_ref[POOL_HALO + t:2 * POOL_HALO + t, :] = jnp.where(i == nt - 1, 0.0, next_ref[0])
    acc = ext_ref[pl.ds(POOL_HALO - CONV_LEFT, t), :] * w_ref[0:1, :]
    for kk in range(1, D_CONV):
        acc = acc + ext_ref[pl.ds(POOL_HALO - CONV_LEFT + kk, t), :] * w_ref[kk:kk + 1, :]
    acc = acc + b_ref[...]
    o_ref[0] = acc / (1.0 + jnp.exp(-acc))


def _ssd_conv(zx3d, conv_w, conv_b, d_inner, conv_dim):
    bsz, s, _ = zx3d.shape
    tc = 512
    t = _pick(s, 512)
    nt = s // t
    hb = t // POOL_HALO
    last_hb = s // POOL_HALO - 1
    c0 = d_inner // tc
    return pl.pallas_call(
        functools.partial(_conv_kernel, nt=nt),
        grid=(bsz, nt, conv_dim // tc),
        in_specs=[pl.BlockSpec((1, POOL_HALO, tc), lambda bi, i, j: (bi, jnp.maximum(i * hb - 1, 0), c0 + j)),
                  pl.BlockSpec((1, t, tc), lambda bi, i, j: (bi, i, c0 + j)),
                  pl.BlockSpec((1, POOL_HALO, tc), lambda bi, i, j: (bi, jnp.minimum((i + 1) * hb, last_hb), c0 + j)),
                  pl.BlockSpec((D_CONV, tc), lambda bi, i, j: (0, j)),
                  pl.BlockSpec((1, tc), lambda bi, i, j: (0, j))],
        out_specs=pl.BlockSpec((1, t, tc), lambda bi, i, j: (bi, i, j)),
        out_shape=jax.ShapeDtypeStruct((bsz, s, conv_dim), F32),
        scratch_shapes=[pltpu.VMEM((t + 2 * POOL_HALO, tc), F32)],
        compiler_params=_params("parallel", "parallel", "parallel"),
        name="ssd_conv",
    )(zx3d, zx3d, zx3d, conv_w, conv_b.reshape(1, conv_dim))


def _split3(x):
    hi = x.astype(BF16)
    r1 = x - hi.astype(F32)
    mid = r1.astype(BF16)
    lo = (r1 - mid.astype(F32)).astype(BF16)
    return hi, mid, lo


def _dt_kernel(raw_ref, bias_ref, a_ref, dt_ref, e_ref, tot_ref):
    v = raw_ref[0] + bias_ref[...]
    dt = jnp.maximum(v, 0.0) + jnp.log1p(jnp.exp(-jnp.abs(v)))
    dt_ref[0] = dt
    a = dt * a_ref[...]
    q, w = a.shape
    li = lax.broadcasted_iota(jnp.int32, (q, q), 0)
    si = lax.broadcasted_iota(jnp.int32, (q, q), 1)
    tri = jnp.where(li >= si, 1.0, 0.0).astype(BF16)
    hi, mid, lo = _split3(a)
    cs = (jnp.dot(tri, lo, preferred_element_type=F32) + jnp.dot(tri, mid, preferred_element_type=F32)
          + jnp.dot(tri, hi, preferred_element_type=F32))
    lane = lax.broadcasted_iota(jnp.int32, (q, w), 1)
    e_ref[0] = jnp.where(lane < w // 2, cs, cs - a)
    tot_ref[0, 0] = cs[q - 1:q, :]


def _ssd_dt(zx3d, dt_bias, a_neg, col0):
    bsz, s, _ = zx3d.shape
    w = dt_bias.shape[-1]
    assert w == LANES and col0 % LANES == 0
    nc = s // SSD_CHUNK
    blk = pl.BlockSpec((1, SSD_CHUNK, w), lambda bi, c: (bi, c, 0))
    row = pl.BlockSpec((1, w), lambda bi, c: (0, 0))
    return pl.pallas_call(
        _dt_kernel,
        grid=(bsz, nc),
        in_specs=[pl.BlockSpec((1, SSD_CHUNK, w), lambda bi, c: (bi, c, col0 // LANES)), row, row],
        out_specs=[blk, blk, pl.BlockSpec((1, 1, 1, w), lambda bi, c: (bi, c, 0, 0))],
        out_shape=[jax.ShapeDtypeStruct((bsz, s, w), F32), jax.ShapeDtypeStruct((bsz, s, w), F32),
                   jax.ShapeDtypeStruct((bsz, nc, 1, w), F32)],
        compiler_params=_params("parallel", "parallel"),
        name="ssd_dt",
    )(zx3d, dt_bias.reshape(1, w), a_neg.reshape(1, w))


def _expand_heads(v, width):
    m = v.shape[0]
    lane = lax.broadcasted_iota(jnp.int32, (m, LANES), 1)
    parts = []
    for pr in range(width // LANES):
        parts.append(jnp.where(lane < SSD_HEAD_DIM, v[:, 2 * pr:2 * pr + 1], v[:, 2 * pr + 1:2 * pr + 2]))
    return jnp.concatenate(parts, axis=1)


def _ssd_direction(x, bmat, cmat, dt_col, dt_row, e_col, e_row, tot, st_ref, forward):
    q, width = x.shape
    li = lax.broadcasted_iota(jnp.int32, (q, q), 0)
    si = lax.broadcasted_iota(jnp.int32, (q, q), 1)
    lane = lax.broadcasted_iota(jnp.int32, (q, LANES), 1)
    if forward:
        mask = li >= si
        out_dec = jnp.exp(e_col)
        st_w = dt_col * jnp.exp(tot - e_col)
    else:
        mask = si >= li
        out_dec = jnp.exp(tot - e_col)
        st_w = dt_col * jnp.exp(e_col)
    cb = lax.dot_general(cmat.astype(BF16), bmat.astype(BF16), (((1,), (1,)), ((), ())),
                         preferred_element_type=F32)
    xb = x.astype(BF16)
    y_parts = []
    for pr in range(width // LANES):
        ms = []
        for r in (2 * pr, 2 * pr + 1):
            if forward:
                diff = e_col[:, r:r + 1] - e_row[r:r + 1, :]
            else:
                diff = e_row[r:r + 1, :] - e_col[:, r:r + 1]
            decay = jnp.exp(jnp.where(mask, diff, -jnp.inf))
            ms.append((decay * cb * dt_row[r:r + 1, :]).astype(BF16))
        xp = xb[:, pr * LANES:(pr + 1) * LANES]
        zero = jnp.zeros_like(xp)
        rhs = jnp.concatenate([jnp.where(lane < SSD_HEAD_DIM, xp, zero),
                               jnp.where(lane >= SSD_HEAD_DIM, xp, zero)], axis=0)
        y_parts.append(jnp.dot(jnp.concatenate(ms, axis=1), rhs, preferred_element_type=F32))
    y = jnp.concatenate(y_parts, axis=1)
    st = st_ref[...]
    y = y + jnp.dot(cmat.astype(BF16), st.astype(BF16), preferred_element_type=F32) * _expand_heads(out_dec, width)
    xd = (x * _expand_heads(st_w, width)).astype(BF16)
    st_new = lax.dot_general(bmat.astype(BF16), xd, (((0,), (0,)), ((), ())), preferred_element_type=F32)
    chunk_dec = _expand_heads(jnp.broadcast_to(jnp.exp(tot), (8, tot.shape[1])), width)[0:1, :]
    st_ref[...] = st * chunk_dec + st_new
    return y


def _ssd_scan_kernel(xf_ref, bf_ref, cf_ref, dcf_ref, drf_ref, ecf_ref, erf_ref, tf_ref,
                     xr_ref, br_ref, cr_ref, dcr_ref, drr_ref, ecr_ref, err_ref, tr_ref,
                     dskip_ref, yf_ref, yb_ref, stf_ref, stb_ref):
    @pl.when(pl.program_id(2) == 0)
    def _():
        stf_ref[...] = jnp.zeros_like(stf_ref)
        stb_ref[...] = jnp.zeros_like(stb_ref)

    xf = xf_ref[0]
    yf = _ssd_direction(xf, bf_ref[0], cf_ref[0], dcf_ref[0, 0, 0], drf_ref[0, 0, 0], ecf_ref[0, 0, 0],
                        erf_ref[0, 0, 0], tf_ref[0, 0, 0, 0], stf_ref, True)
    yf_ref[0] = yf + dskip_ref[...] * xf
    yb_ref[0] = _ssd_direction(xr_ref[0], br_ref[0], cr_ref[0], dcr_ref[0, 0, 0], drr_ref[0, 0, 0], ecr_ref[0, 0, 0],
                               err_ref[0, 0, 0], tr_ref[0, 0, 0, 0], stb_ref, False)


def _ssd_scan(xbc, dt, ecs, tot, d_skip, d_inner, n_groups):
    bsz, s, _ = xbc.shape
    nc = s // SSD_CHUNK
    hg = SSD_HEADS_PER_GROUP
    gw = hg * SSD_HEAD_DIM
    assert gw % LANES == 0 and d_inner == n_groups * gw
    b0 = d_inner // D_STATE
    c0 = b0 + n_groups

    def col_layout(a):
        return a.reshape(bsz, s, 2, n_groups, hg).transpose(0, 2, 3, 1, 4)

    def row_layout(a):
        return a.reshape(bsz, s, 2, n_groups, hg).transpose(0, 2, 3, 4, 1)

    dt_c, dt_r, e_c, e_r = col_layout(dt), row_layout(dt), col_layout(ecs), row_layout(ecs)
    tot6 = tot.reshape(bsz, nc, 2, n_groups, 1, hg)
    dskip = jnp.repeat(d_skip.astype(F32), SSD_HEAD_DIM).reshape(1, d_inner)

    def specs(direction, cidx):
        return [
            pl.BlockSpec((1, SSD_CHUNK, gw), lambda b, g, c: (b, cidx(c), g)),
            pl.BlockSpec((1, SSD_CHUNK, D_STATE), lambda b, g, c: (b, cidx(c), b0 + g)),
            pl.BlockSpec((1, SSD_CHUNK, D_STATE), lambda b, g, c: (b, cidx(c), c0 + g)),
            pl.BlockSpec((1, 1, 1, SSD_CHUNK, hg), lambda b, g, c: (b, direction, g, cidx(c), 0)),
            pl.BlockSpec((1, 1, 1, hg, SSD_CHUNK), lambda b, g, c: (b, direction, g, 0, cidx(c))),
            pl.BlockSpec((1, 1, 1, SSD_CHUNK, hg), lambda b, g, c: (b, direction, g, cidx(c), 0)),
            pl.BlockSpec((1, 1, 1, hg, SSD_CHUNK), lambda b, g, c: (b, direction, g, 0, cidx(c))),
            pl.BlockSpec((1, 1, 1, 1, 1, hg), lambda b, g, c: (b, cidx(c), direction, g, 0, 0)),
        ]

    fwd = lambda c: c
    bwd = lambda c: nc - 1 - c
    y_shape = jax.ShapeDtypeStruct((bsz, s, d_inner), F32)
    return pl.pallas_call(
        _ssd_scan_kernel,
        grid=(bsz, n_groups, nc),
        in_specs=specs(0, fwd) + specs(1, bwd) + [pl.BlockSpec((1, gw), lambda b, g, c: (0, g))],
        out_specs=[pl.BlockSpec((1, SSD_CHUNK, gw), lambda b, g, c: (b, c, g)),
                   pl.BlockSpec((1, SSD_CHUNK, gw), lambda b, g, c: (b, nc - 1 - c, g))],
        out_shape=[y_shape, y_shape],
        scratch_shapes=[pltpu.VMEM((D_STATE, gw), F32), pltpu.VMEM((D_STATE, gw), F32)],
        compiler_params=_params("parallel", "parallel", "arbitrary"),
        name="ssd_scan",
    )(xbc, xbc, xbc, dt_c, dt_r, e_c, e_r, tot6,
      xbc, xbc, xbc, dt_c, dt_r, e_c, e_r, tot6, dskip)


def _gate_kernel(yf_ref, yb_ref, z_ref, nw_ref, o_ref):
    z = z_ref[...]
    y = (yf_ref[...] + yb_ref[...]) * (z / (1.0 + jnp.exp(-z)))
    y = y * lax.rsqrt(jnp.mean(y * y, axis=-1, keepdims=True) + RMS_EPS) * nw_ref[...]
    o_ref[...] = y.astype(BF16)


def _ssd_gate(yf2d, yb2d, zx2d, norm_w):
    n, d_inner = yf2d.shape
    tm = _pick(n, 256)
    blk = pl.BlockSpec((tm, d_inner), lambda i: (i, 0))
    return pl.pallas_call(
        _gate_kernel,
        grid=(n // tm,),
        in_specs=[blk, blk, blk, pl.BlockSpec((1, d_inner), lambda i: (0, 0))],
        out_specs=blk,
        out_shape=jax.ShapeDtypeStruct((n, d_inner), BF16),
        compiler_params=_params("parallel"),
        name="ssd_gate_norm",
    )(yf2d, yb2d, zx2d, norm_w.reshape(1, d_inner))


def _ffn_kernel(xs_ref, wg_ref, wu_ref, wd_ref, gate_ref, o_ref):
    xs = xs_ref[0]
    hg = jnp.dot(xs, wg_ref[0], preferred_element_type=F32)
    hu = jnp.dot(xs, wu_ref[0], preferred_element_type=F32)
    h = (hg / (1.0 + jnp.exp(-hg)) * hu).astype(BF16)
    o_ref[0] = jnp.dot(h, wd_ref[0], preferred_element_type=F32) * gate_ref[0]


def _moe_ffn(xs, gate, wg, wu, wd):
    n_exp, cap, d = xs.shape
    f = wg.shape[2]
    tile = _pick(cap, 256)
    return pl.pallas_call(
        _ffn_kernel,
        grid=(n_exp, cap // tile),
        in_specs=[pl.BlockSpec((1, tile, d), lambda e, t: (e, t, 0)),
                  pl.BlockSpec((1, d, f), lambda e, t: (e, 0, 0)),
                  pl.BlockSpec((1, d, f), lambda e, t: (e, 0, 0)),
                  pl.BlockSpec((1, f, d), lambda e, t: (e, 0, 0)),
                  pl.BlockSpec((1, tile, 1), lambda e, t: (e, t, 0))],
        out_specs=pl.BlockSpec((1, tile, d), lambda e, t: (e, t, 0)),
        out_shape=jax.ShapeDtypeStruct((n_exp, cap, d), F32),
        compiler_params=_params("parallel", "arbitrary"),
        name="moe_ffn",
    )(xs, wg, wu, wd, gate)


def _ec_moe(x2d, aff, groups, wg, wu, wd):
    n, d = x2d.shape
    n_exp = aff.shape[1]
    gates, idxs = [], []
    for start, cnt in groups:
        cap = EC_CAPACITY_FACTOR * cnt // n_exp
        gt, ix = lax.top_k(aff[start:start + cnt].T, cap)
        gates.append(gt)
        idxs.append(ix + start)
    gate = jnp.concatenate(gates, axis=1)
    idx = jnp.concatenate(idxs, axis=1)
    xs = x2d.astype(BF16)[idx]
    o = _moe_ffn(xs, gate[..., None], wg, wu, wd)
    return jnp.zeros((n, d), F32).at[idx.reshape(-1)].add(o.reshape(-1, d))


def _split2_bf16(w):
    hi = w.astype(BF16)
    lo = (w - hi.astype(F32)).astype(BF16)
    return jnp.concatenate([hi, lo], axis=1)


def kernel(x_prompt, x_sample, attn_w_qkv, attn_q_norm, attn_k_norm, attn_w_o, pool_w, pool_scale, ssd_w_in,
           ssd_conv_w, ssd_conv_b, ssd_dt_bias, ssd_A_log, ssd_D, ssd_norm, ssd_w_out, moe_w_router, moe_w_gate,
           moe_w_up, moe_w_down, ln_g, ln_b):
    bp, s, d = x_prompt.shape
    bs = x_sample.shape[0]
    assert x_sample.shape[1] == s
    bsz = bp + bs
    n = bsz * s
    groups = [(0, bp * s), (bp * s, bs * s)]
    depth = ln_g.shape[0]
    alpha = (2 * depth) ** 0.25
    n_heads = attn_w_o.shape[1] // HEAD_DIM
    n_kv = (attn_w_qkv.shape[2] // HEAD_DIM - n_heads) // 2
    d_inner = ssd_w_out.shape[1]
    n_ssd_heads = ssd_A_log.shape[-1]
    assert d_inner == n_ssd_heads * SSD_HEAD_DIM
    conv_dim = ssd_conv_w.shape[2]
    n_groups = (conv_dim - d_inner) // (2 * D_STATE)
    rope = _rope_tables(s)

    x = jnp.concatenate([x_prompt, x_sample], axis=0).reshape(n, d)
    ia = ip = isd = 0
    for i in range(depth):
        wr2 = _split2_bf16(moe_w_router[i])
        g1, b1, g2, b2 = ln_g[i, 0], ln_b[i, 0], ln_g[i, 1], ln_b[i, 1]
        kind = i % 3
        if kind == 0:
            qkv = _qkv_proj(x, attn_w_qkv[ia].astype(BF16), attn_q_norm[ia], attn_k_norm[ia], rope, s, n_heads, n_kv)
            o = _flash_attention(qkv.reshape(bsz, s, -1), n_heads, n_kv)
            x, aff = _mm_res_ln_router(o.reshape(n, -1), attn_w_o[ia].astype(BF16), x, g1, b1, wr2, alpha)
            ia += 1
        elif kind == 1:
            x3, aff3 = _pool_layer(x.reshape(bsz, s, d), pool_w[ip].astype(BF16), pool_scale[ip], g1, b1, wr2, alpha)
            x, aff = x3.reshape(n, d), aff3.reshape(n, -1)
            ip += 1
        else:
            zx = _matmul_f32(x, ssd_w_in[isd].astype(BF16), 1152)
            zx3 = zx.reshape(bsz, s, -1)
            xbc = _ssd_conv(zx3, ssd_conv_w[isd], ssd_conv_b[isd], d_inner, conv_dim)
            a_neg = -jnp.exp(ssd_A_log[isd].astype(F32)).reshape(-1)
            dt, ecs, tot = _ssd_dt(zx3, ssd_dt_bias[isd].reshape(-1), a_neg, d_inner + conv_dim)
            yf, yb = _ssd_scan(xbc, dt, ecs, tot, ssd_D[isd], d_inner, n_groups)
            yn = _ssd_gate(yf.reshape(n, d_inner), yb.reshape(n, d_inner), zx, ssd_norm[isd])
            x, aff = _mm_res_ln_router(yn, ssd_w_out[isd].astype(BF16), x, g1, b1, wr2, alpha)
            isd += 1
        m = _ec_moe(x, aff, groups, moe_w_gate[i].astype(BF16), moe_w_up[i].astype(BF16),
                    moe_w_down[i].astype(BF16))
        x = _res_ln_call(x, m, g2, b2, alpha)
    y = x.reshape(bsz, s, d)
    return y[:bp], y[bp:]
```

```python
import functools
import math

import jax
import jax.numpy as jnp
from jax import lax
from jax.experimental import pallas as pl
from jax.experimental.pallas import tpu as pltpu

F32 = jnp.float32
BF16 = jnp.bfloat16

HEAD_DIM = 128
GRID_W = 64
ROPE_THETA = 10000.0
POOL_WINDOWS = (2, 4, 8, 16)
POOL_HALO = 8
D_STATE = 128
SSD_CHUNK = 128
SSD_HEAD_DIM = 64
SSD_HEADS_PER_GROUP = 8
D_CONV = 4
CONV_LEFT = D_CONV // 2
EC_CAPACITY_FACTOR = 2
LN_EPS = 1e-5
RMS_EPS = 1e-6
LOG2E = 1.4426950408889634

V7X_VMEM_LIMIT_BYTES = 52 * 1024 * 1024
LANES = 128


def _params(*sem):
    return pltpu.CompilerParams(dimension_semantics=sem, vmem_limit_bytes=V7X_VMEM_LIMIT_BYTES)


def _pick(n, pref):
    t = min(n, pref)
    while n % t:
        t //= 2
    return t


def _res_ln(x, h, g, b, alpha):
    y = alpha * x + h
    mu = jnp.mean(y, axis=-1, keepdims=True)
    yc = y - mu
    var = jnp.mean(yc * yc, axis=-1, keepdims=True)
    return yc * lax.rsqrt(var + LN_EPS) * g + b


def _router_affinity(xn, wr_ref, n_exp):
    xh = xn.astype(BF16)
    xl = (xn - xh.astype(F32)).astype(BF16)
    wr = wr_ref[...]
    r1 = jnp.dot(xh, wr, preferred_element_type=F32)
    r2 = jnp.dot(xl, wr[:, :n_exp], preferred_element_type=F32)
    logits = r1[:, :n_exp] + (r1[:, n_exp:] + r2)
    m = jnp.max(logits, axis=-1, keepdims=True)
    e = jnp.exp(logits - m)
    return e / jnp.sum(e, axis=-1, keepdims=True)


def _qkv_kernel(x_ref, w_ref, cos_ref, sa_ref, sb_ref, qn_ref, kn_ref, o_ref, xb_ref, *,
                n_q_tiles, n_k_tiles, heads_per_tile, q_scale):
    j = pl.program_id(1)

    @pl.when(j == 0)
    def _():
        xb_ref[...] = x_ref[...].astype(BF16)

    acc = jnp.dot(xb_ref[...], w_ref[...], preferred_element_type=F32)

    def norm_rope(gain_ref, scale):
        cos = cos_ref[...]
        sa = sa_ref[...]
        sb = sb_ref[...]
        g = gain_ref[...]
        for h in range(heads_per_tile):
            a = acc[:, h * HEAD_DIM:(h + 1) * HEAD_DIM]
            a = a * lax.rsqrt(jnp.mean(a * a, axis=-1, keepdims=True) + RMS_EPS) * g
            r = a * cos + pltpu.roll(a, HEAD_DIM - 32, 1) * sa + pltpu.roll(a, 32, 1) * sb
            o_ref[:, h * HEAD_DIM:(h + 1) * HEAD_DIM] = (r * scale).astype(BF16)

    @pl.when(j < n_q_tiles)
    def _():
        norm_rope(qn_ref, q_scale)

    @pl.when((j >= n_q_tiles) & (j < n_q_tiles + n_k_tiles))
    def _():
        norm_rope(kn_ref, 1.0)

    @pl.when(j >= n_q_tiles + n_k_tiles)
    def _():
        o_ref[...] = acc.astype(BF16)


def _rope_tables(seq_len):
    rows = seq_len // GRID_W
    row = jnp.repeat(jnp.arange(rows, dtype=F32), GRID_W)
    col = jnp.tile(jnp.arange(GRID_W, dtype=F32), rows)
    half = HEAD_DIM // 4
    inv_freq = ROPE_THETA ** (-jnp.arange(0, HEAD_DIM // 2, 2, dtype=F32) / (HEAD_DIM // 2))
    ang_r = row[:, None] * inv_freq
    ang_c = col[:, None] * inv_freq
    zeros = jnp.zeros((seq_len, half), F32)
    cos = jnp.concatenate([jnp.cos(ang_r)] * 2 + [jnp.cos(ang_c)] * 2, axis=-1)
    sin_a = jnp.concatenate([-jnp.sin(ang_r), zeros, -jnp.sin(ang_c), zeros], axis=-1)
    sin_b = jnp.concatenate([zeros, jnp.sin(ang_r), zeros, jnp.sin(ang_c)], axis=-1)
    return cos, sin_a, sin_b


def _qkv_proj(x2d, w_bf16, q_norm, k_norm, rope, seq_len, n_heads, n_kv):
    n, d = x2d.shape
    qkv_dim = w_bf16.shape[1]
    tn = n_kv * HEAD_DIM
    tm = _pick(seq_len, 512)
    cos, sa, sb = rope
    nsb = seq_len // tm
    kern = functools.partial(
        _qkv_kernel, n_q_tiles=n_heads // n_kv, n_k_tiles=1, heads_per_tile=n_kv,
        q_scale=HEAD_DIM ** -0.5 * LOG2E)
    tab = pl.BlockSpec((tm, HEAD_DIM), lambda i, j: (i % nsb, 0))
    vec = pl.BlockSpec((1, HEAD_DIM), lambda i, j: (0, 0))
    return pl.pallas_call(
        kern,
        grid=(n // tm, qkv_dim // tn),
        in_specs=[pl.BlockSpec((tm, d), lambda i, j: (i, 0)),
                  pl.BlockSpec((d, tn), lambda i, j: (0, j)),
                  tab, tab, tab, vec, vec],
        out_specs=pl.BlockSpec((tm, tn), lambda i, j: (i, j)),
        out_shape=jax.ShapeDtypeStruct((n, qkv_dim), BF16),
        scratch_shapes=[pltpu.VMEM((tm, d), BF16)],
        compiler_params=_params("parallel", "arbitrary"),
        name="qkv_proj",
    )(x2d, w_bf16, cos, sa, sb, q_norm.reshape(1, HEAD_DIM), k_norm.reshape(1, HEAD_DIM))


FLASH_TQ = 128
FLASH_TK = 512


def _flash_kernel(q_ref, k_ref, vt_ref, o_ref, s_scr, p_scr, acc_scr, *, tk, group):
    tq = q_ref.shape[1]
    seq = k_ref.shape[1]
    q = jnp.concatenate([q_ref[0, :, g * HEAD_DIM:(g + 1) * HEAD_DIM] for g in range(group)], axis=0)
    rows = group * tq
    nc = seq // tk

    def scores(c, slot):
        k = k_ref[0, pl.ds(c * tk, tk), :]
        s_scr[slot] = lax.dot_general(k, q, (((1,), (1,)), ((), ())), preferred_element_type=F32)

    def pv(c, slot, alpha):
        vt = vt_ref[0, :, pl.ds(c * tk, tk)]
        acc_scr[...] = acc_scr[...] * alpha + jnp.dot(vt, p_scr[slot], preferred_element_type=F32)

    def softmax(slot, m, l):
        s = s_scr[slot]
        m_new = jnp.maximum(m, jnp.max(s, axis=0, keepdims=True))
        alpha = jnp.exp2(m - m_new)
        p = jnp.exp2(s - m_new)
        l = alpha * l + jnp.sum(p, axis=0, keepdims=True)
        p_scr[slot] = p.astype(BF16)
        return m_new, l, alpha

    m = jnp.full((1, rows), -jnp.inf, F32)
    l = jnp.zeros((1, rows), F32)
    acc_scr[...] = jnp.zeros_like(acc_scr)
    scores(0, 0)
    if nc > 1:
        scores(1, 1)
    m, l, alpha = softmax(0, m, l)
    for c in range(1, nc):
        if c + 1 < nc:
            scores(c + 1, (c + 1) % 2)
        pv(c - 1, (c - 1) % 2, alpha)
        m, l, alpha = softmax(c % 2, m, l)
    pv(nc - 1, (nc - 1) % 2, alpha)
    o = (acc_scr[...] / l).T
    for g in range(group):
        o_ref[0, :, g * HEAD_DIM:(g + 1) * HEAD_DIM] = o[g * tq:(g + 1) * tq].astype(BF16)


def _flash_attention(qkv, vt, n_heads, n_kv):
    b, s, _ = qkv.shape
    group = n_heads // n_kv
    tq = _pick(s, FLASH_TQ)
    tk = _pick(s, FLASH_TK)
    gw = group * HEAD_DIM
    rows = group * tq
    kern = functools.partial(_flash_kernel, tk=tk, group=group)
    return pl.pallas_call(
        kern,
        grid=(b, n_kv, s // tq),
        in_specs=[pl.BlockSpec((1, tq, gw), lambda bi, h, i: (bi, i, h)),
                  pl.BlockSpec((1, s, HEAD_DIM), lambda bi, h, i: (bi, 0, n_heads + h)),
                  pl.BlockSpec((1, HEAD_DIM, s), lambda bi, h, i: (bi, h, 0))],
        out_specs=pl.BlockSpec((1, tq, gw), lambda bi, h, i: (bi, i, h)),
        out_shape=jax.ShapeDtypeStruct((b, s, n_heads * HEAD_DIM), BF16),
        scratch_shapes=[pltpu.VMEM((2, tk, rows), F32), pltpu.VMEM((2, tk, rows), BF16),
                        pltpu.VMEM((HEAD_DIM, rows), F32)],
        compiler_params=_params("parallel", "parallel", "arbitrary"),
        name="flash_attention",
    )(qkv, qkv, vt)


def _mm_ln_kernel(a_ref, w_ref, x_ref, g_ref, b_ref, wr_ref, o_ref, aff_ref, acc_ref, *, alpha, nk, n_exp):
    k = pl.program_id(1)

    @pl.when(k == 0)
    def _():
        acc_ref[...] = jnp.zeros_like(acc_ref)

    acc_ref[...] += jnp.dot(a_ref[...], w_ref[...], preferred_element_type=F32)

    @pl.when(k == nk - 1)
    def _():
        xn = _res_ln(x_ref[...], acc_ref[...], g_ref[...], b_ref[...], alpha)
        o_ref[...] = xn
        aff_ref[...] = _router_affinity(xn, wr_ref, n_exp)


def _mm_res_ln_router(a_bf16, w_bf16, x2d, g, b, wr2, alpha):
    n, kdim = a_bf16.shape
    d = w_bf16.shape[1]
    n_exp = wr2.shape[1] // 2
    tm = _pick(n, 512)
    tk = _pick(kdim, 512)
    nk = kdim // tk
    kern = functools.partial(_mm_ln_kernel, alpha=alpha, nk=nk, n_exp=n_exp)
    row = pl.BlockSpec((1, d), lambda i, k: (0, 0))
    return pl.pallas_call(
        kern,
        grid=(n // tm, nk),
        in_specs=[pl.BlockSpec((tm, tk), lambda i, k: (i, k)),
                  pl.BlockSpec((tk, d), lambda i, k: (k, 0)),
                  pl.BlockSpec((tm, d), lambda i, k: (i, 0)),
                  row, row,
                  pl.BlockSpec((d, 2 * n_exp), lambda i, k: (0, 0))],
        out_specs=[pl.BlockSpec((tm, d), lambda i, k: (i, 0)),
                   pl.BlockSpec((tm, n_exp), lambda i, k: (i, 0))],
        out_shape=[jax.ShapeDtypeStruct((n, d), F32), jax.ShapeDtypeStruct((n, n_exp), F32)],
        scratch_shapes=[pltpu.VMEM((tm, d), F32)],
        compiler_params=_params("parallel", "arbitrary"),
        name="mm_res_ln_router",
    )(a_bf16, w_bf16, x2d, g.reshape(1, d), b.reshape(1, d), wr2)


def _pool_kernel(prev_ref, cur_ref, next_ref, w_ref, sc_ref, g_ref, b_ref, wr_ref, o_ref, aff_ref, ext_ref, *,
                 alpha, nt, seq_len, n_exp):
    i = pl.program_id(1)
    t = cur_ref.shape[1]
    d = cur_ref.shape[2]
    pg = d // len(POOL_WINDOWS)
    x = cur_ref[0]
    ext_ref[0:POOL_HALO, :] = jnp.where(i == 0, 0.0, prev_ref[0])
    ext_ref[POOL_HALO:POOL_HALO + t, :] = x
    ext_ref[POOL_HALO + t:2 * POOL_HALO + t, :] = jnp.where(i == nt - 1, 0.0, next_ref[0])
    pos = i * t + lax.broadcasted_iota(jnp.int32, (t, 1), 0)
    hs = []
    for gi, w in enumerate(POOL_WINDOWS):
        half = w // 2
        cols = slice(gi * pg, (gi + 1) * pg)
        acc = ext_ref[pl.ds(POOL_HALO - half, t), cols]
        for jj in range(1, w):
            acc = acc + ext_ref[pl.ds(POOL_HALO - half + jj, t), cols]
        cnt = (jnp.minimum(pos + half, seq_len) - jnp.maximum(pos - half, 0)).astype(F32)
        mixed = (acc / cnt - x[:, cols]).astype(BF16)
        hs.append(jnp.dot(mixed, w_ref[gi], preferred_element_type=F32))
    h = jnp.concatenate(hs, axis=-1) * sc_ref[...]
    xn = _res_ln(x, h, g_ref[...], b_ref[...], alpha)
    o_ref[0] = xn
    aff_ref[0] = _router_affinity(xn, wr_ref, n_exp)


def _pool_layer(x3d, w_bf16, scale, g, b, wr2, alpha):
    bsz, s, d = x3d.shape
    n_exp = wr2.shape[1] // 2
    t = _pick(s, 256)
    nt = s // t
    hb = t // POOL_HALO
    last_hb = s // POOL_HALO - 1
    pg = d // len(POOL_WINDOWS)
    kern = functools.partial(_pool_kernel, alpha=alpha, nt=nt, seq_len=s, n_exp=n_exp)
    row = pl.BlockSpec((1, d), lambda bi, i: (0, 0))
    return pl.pallas_call(
        kern,
        grid=(bsz, nt),
        in_specs=[pl.BlockSpec((1, POOL_HALO, d), lambda bi, i: (bi, jnp.maximum(i * hb - 1, 0), 0)),
                  pl.BlockSpec((1, t, d), lambda bi, i: (bi, i, 0)),
                  pl.BlockSpec((1, POOL_HALO, d), lambda bi, i: (bi, jnp.minimum((i + 1) * hb, last_hb), 0)),
                  pl.BlockSpec((len(POOL_WINDOWS), pg, pg), lambda bi, i: (0, 0, 0)),
                  row, row, row,
                  pl.BlockSpec((d, 2 * n_exp), lambda bi, i: (0, 0))],
        out_specs=[pl.BlockSpec((1, t, d), lambda bi, i: (bi, i, 0)),
                   pl.BlockSpec((1, t, n_exp), lambda bi, i: (bi, i, 0))],
        out_shape=[jax.ShapeDtypeStruct((bsz, s, d), F32), jax.ShapeDtypeStruct((bsz, s, n_exp), F32)],
        scratch_shapes=[pltpu.VMEM((t + 2 * POOL_HALO, d), F32)],
        compiler_params=_params("parallel", "parallel"),
        name="pool_mixer",
    )(x3d, x3d, x3d, w_bf16, scale.reshape(1, d), g.reshape(1, d), b.reshape(1, d), wr2)


def _mm_kernel(x_ref, w_ref, o_ref, xb_ref):
    @pl.when(pl.program_id(1) == 0)
    def _():
        xb_ref[...] = x_ref[...].astype(BF16)

    o_ref[...] = jnp.dot(xb_ref[...], w_ref[...], preferred_element_type=F32)


def _matmul_f32(x2d, w_bf16, tn_pref):
    n, d = x2d.shape
    nout = w_bf16.shape[1]
    tm = _pick(n, 512)
    tn = tn_pref
    assert nout % tn == 0
    return pl.pallas_call(
        _mm_kernel,
        grid=(n // tm, nout // tn),
        in_specs=[pl.BlockSpec((tm, d), lambda i, j: (i, 0)),
                  pl.BlockSpec((d, tn), lambda i, j: (0, j))],
        out_specs=pl.BlockSpec((tm, tn), lambda i, j: (i, j)),
        out_shape=jax.ShapeDtypeStruct((n, nout), F32),
        scratch_shapes=[pltpu.VMEM((tm, d), BF16)],
        compiler_params=_params("parallel", "arbitrary"),
        name="ssd_in_proj",
    )(x2d, w_bf16)


def _conv_kernel(prev_ref, cur_ref, next_ref, w_ref, b_ref, o_ref, ext_ref, *, nt):
    i = pl.program_id(1)
    t = cur_ref.shape[1]
    ext_ref[0:POOL_HALO, :] = jnp.where(i == 0, 0.0, prev_ref[0])
    ext_ref[POOL_HALO:POOL_HALO + t, :] = cur_ref[0]
    ext_ref[POOL_HALO + t:2 * POOL_HALO + t, :] = jnp.where(i == nt - 1, 0.0, next_ref[0])
    acc = ext_ref[pl.ds(POOL_HALO - CONV_LEFT, t), :] * w_ref[0:1, :]
    for kk in range(1, D_CONV):
        acc = acc + ext_ref[pl.ds(POOL_HALO - CONV_LEFT + kk, t), :] * w_ref[kk:kk + 1, :]
    acc = acc + b_ref[...]
    o_ref[0] = acc / (1.0 + jnp.exp(-acc))


def _ssd_conv(zx3d, conv_w, conv_b, d_inner, conv_dim):
    bsz, s, _ = zx3d.shape
    tc = 512
    t = _pick(s, 512)
    nt = s // t
    hb = t // POOL_HALO
    last_hb = s // POOL_HALO - 1
    c0 = d_inner // tc
    return pl.pallas_call(
        functools.partial(_conv_kernel, nt=nt),
        grid=(bsz, nt, conv_dim // tc),
        in_specs=[pl.BlockSpec((1, POOL_HALO, tc), lambda bi, i, j: (bi, jnp.maximum(i * hb - 1, 0), c0 + j)),
                  pl.BlockSpec((1, t, tc), lambda bi, i, j: (bi, i, c0 + j)),
                  pl.BlockSpec((1, POOL_HALO, tc), lambda bi, i, j: (bi, jnp.minimum((i + 1) * hb, last_hb), c0 + j)),
                  pl.BlockSpec((D_CONV, tc), lambda bi, i, j: (0, j)),
                  pl.BlockSpec((1, tc), lambda bi, i, j: (0, j))],
        out_specs=pl.BlockSpec((1, t, tc), lambda bi, i, j: (bi, i, j)),
        out_shape=jax.ShapeDtypeStruct((bsz, s, conv_dim), F32),
        scratch_shapes=[pltpu.VMEM((t + 2 * POOL_HALO, tc), F32)],
        compiler_params=_params("parallel", "parallel", "parallel"),
        name="ssd_conv",
    )(zx3d, zx3d, zx3d, conv_w, conv_b.reshape(1, conv_dim))


def _split3(x):
    hi = x.astype(BF16)
    r1 = x - hi.astype(F32)
    mid = r1.astype(BF16)
    lo = (r1 - mid.astype(F32)).astype(BF16)
    return hi, mid, lo


def _dt_kernel(raw_ref, bias_ref, a_ref, dt_ref, e_ref, tot_ref):
    v = raw_ref[0] + bias_ref[...]
    dt = jnp.maximum(v, 0.0) + jnp.log1p(jnp.exp(-jnp.abs(v)))
    dt_ref[0] = dt
    a = dt * a_ref[...]
    q, w = a.shape
    li = lax.broadcasted_iota(jnp.int32, (q, q), 0)
    si = lax.broadcasted_iota(jnp.int32, (q, q), 1)
    tri = jnp.where(li >= si, 1.0, 0.0).astype(BF16)
    hi, mid, lo = _split3(a)
    cs = (jnp.dot(tri, lo, preferred_element_type=F32) + jnp.dot(tri, mid, preferred_element_type=F32)
          + jnp.dot(tri, hi, preferred_element_type=F32))
    lane = lax.broadcasted_iota(jnp.int32, (q, w), 1)
    e_ref[0] = jnp.where(lane < w // 2, cs, cs - a)
    tot_ref[0, 0] = cs[q - 1:q, :]


def _ssd_dt(zx3d, dt_bias, a_neg, col0):
    bsz, s, _ = zx3d.shape
    w = dt_bias.shape[-1]
    assert w == LANES and col0 % LANES == 0
    nc = s // SSD_CHUNK
    blk = pl.BlockSpec((1, SSD_CHUNK, w), lambda bi, c: (bi, c, 0))
    row = pl.BlockSpec((1, w), lambda bi, c: (0, 0))
    return pl.pallas_call(
        _dt_kernel,
        grid=(bsz, nc),
        in_specs=[pl.BlockSpec((1, SSD_CHUNK, w), lambda bi, c: (bi, c, col0 // LANES)), row, row],
        out_specs=[blk, blk, pl.BlockSpec((1, 1, 1, w), lambda bi, c: (bi, c, 0, 0))],
        out_shape=[jax.ShapeDtypeStruct((bsz, s, w), F32), jax.ShapeDtypeStruct((bsz, s, w), F32),
                   jax.ShapeDtypeStruct((bsz, nc, 1, w), F32)],
        compiler_params=_params("parallel", "parallel"),
        name="ssd_dt",
    )(zx3d, dt_bias.reshape(1, w), a_neg.reshape(1, w))


def _expand_heads(v, width):
    m = v.shape[0]
    lane = lax.broadcasted_iota(jnp.int32, (m, LANES), 1)
    parts = []
    for pr in range(width // LANES):
        parts.append(jnp.where(lane < SSD_HEAD_DIM, v[:, 2 * pr:2 * pr + 1], v[:, 2 * pr + 1:2 * pr + 2]))
    return jnp.concatenate(parts, axis=1)


def _ssd_direction(x, bmat, cmat, dt_col, dt_row, e_col, e_row, tot, st_ref, forward):
    q, width = x.shape
    li = lax.broadcasted_iota(jnp.int32, (q, q), 0)
    si = lax.broadcasted_iota(jnp.int32, (q, q), 1)
    lane = lax.broadcasted_iota(jnp.int32, (q, LANES), 1)
    if forward:
        mask = li >= si
        out_dec = jnp.exp(e_col)
        st_w = dt_col * jnp.exp(tot - e_col)
    else:
        mask = si >= li
        out_dec = jnp.exp(tot - e_col)
        st_w = dt_col * jnp.exp(e_col)
    cb = lax.dot_general(cmat.astype(BF16), bmat.astype(BF16), (((1,), (1,)), ((), ())),
                         preferred_element_type=F32)
    xb = x.astype(BF16)
    y_parts = []
    for pr in range(width // LANES):
        ms = []
        for r in (2 * pr, 2 * pr + 1):
            if forward:
                diff = e_col[:, r:r + 1] - e_row[r:r + 1, :]
            else:
                diff = e_row[r:r + 1, :] - e_col[:, r:r + 1]
            decay = jnp.exp(jnp.where(mask, diff, -jnp.inf))
            ms.append((decay * cb * dt_row[r:r + 1, :]).astype(BF16))
        xp = xb[:, pr * LANES:(pr + 1) * LANES]
        zero = jnp.zeros_like(xp)
        rhs = jnp.concatenate([jnp.where(lane < SSD_HEAD_DIM, xp, zero),
                               jnp.where(lane >= SSD_HEAD_DIM, xp, zero)], axis=0)
        y_parts.append(jnp.dot(jnp.concatenate(ms, axis=1), rhs, preferred_element_type=F32))
    y = jnp.concatenate(y_parts, axis=1)
    st = st_ref[...]
    y = y + jnp.dot(cmat.astype(BF16), st.astype(BF16), preferred_element_type=F32) * _expand_heads(out_dec, width)
    xd = (x * _expand_heads(st_w, width)).astype(BF16)
    st_new = lax.dot_general(bmat.astype(BF16), xd, (((0,), (0,)), ((), ())), preferred_element_type=F32)
    chunk_dec = _expand_heads(jnp.broadcast_to(jnp.exp(tot), (8, tot.shape[1])), width)[0:1, :]
    st_ref[...] = st * chunk_dec + st_new
    return y


def _ssd_scan_kernel(xf_ref, bf_ref, cf_ref, dcf_ref, drf_ref, ecf_ref, erf_ref, tf_ref,
                     xr_ref, br_ref, cr_ref, dcr_ref, drr_ref, ecr_ref, err_ref, tr_ref,
                     dskip_ref, yf_ref, yb_ref, stf_ref, stb_ref):
    @pl.when(pl.program_id(2) == 0)
    def _():
        stf_ref[...] = jnp.zeros_like(stf_ref)
        stb_ref[...] = jnp.zeros_like(stb_ref)

    xf = xf_ref[0]
    yf = _ssd_direction(xf, bf_ref[0], cf_ref[0], dcf_ref[0, 0, 0], drf_ref[0, 0, 0], ecf_ref[0, 0, 0],
                        erf_ref[0, 0, 0], tf_ref[0, 0, 0, 0], stf_ref, True)
    yf_ref[0] = yf + dskip_ref[...] * xf
    yb_ref[0] = _ssd_direction(xr_ref[0], br_ref[0], cr_ref[0], dcr_ref[0, 0, 0], drr_ref[0, 0, 0], ecr_ref[0, 0, 0],
                               err_ref[0, 0, 0], tr_ref[0, 0, 0, 0], stb_ref, False)


def _ssd_scan(xbc, dt, ecs, tot, d_skip, d_inner, n_groups):
    bsz, s, _ = xbc.shape
    nc = s // SSD_CHUNK
    hg = SSD_HEADS_PER_GROUP
    gw = hg * SSD_HEAD_DIM
    assert gw % LANES == 0 and d_inner == n_groups * gw
    b0 = d_inner // D_STATE
    c0 = b0 + n_groups

    def col_layout(a):
        return a.reshape(bsz, s, 2, n_groups, hg).transpose(0, 2, 3, 1, 4)

    def row_layout(a):
        return a.reshape(bsz, s, 2, n_groups, hg).transpose(0, 2, 3, 4, 1)

    dt_c, dt_r, e_c, e_r = col_layout(dt), row_layout(dt), col_layout(ecs), row_layout(ecs)
    tot6 = tot.reshape(bsz, nc, 2, n_groups, 1, hg)
    dskip = jnp.repeat(d_skip.astype(F32), SSD_HEAD_DIM).reshape(1, d_inner)

    def specs(direction, cidx):
        return [
            pl.BlockSpec((1, SSD_CHUNK, gw), lambda b, g, c: (b, cidx(c), g)),
            pl.BlockSpec((1, SSD_CHUNK, D_STATE), lambda b, g, c: (b, cidx(c), b0 + g)),
            pl.BlockSpec((1, SSD_CHUNK, D_STATE), lambda b, g, c: (b, cidx(c), c0 + g)),
            pl.BlockSpec((1, 1, 1, SSD_CHUNK, hg), lambda b, g, c: (b, direction, g, cidx(c), 0)),
            pl.BlockSpec((1, 1, 1, hg, SSD_CHUNK), lambda b, g, c: (b, direction, g, 0, cidx(c))),
            pl.BlockSpec((1, 1, 1, SSD_CHUNK, hg), lambda b, g, c: (b, direction, g, cidx(c), 0)),
            pl.BlockSpec((1, 1, 1, hg, SSD_CHUNK), lambda b, g, c: (b, direction, g, 0, cidx(c))),
            pl.BlockSpec((1, 1, 1, 1, 1, hg), lambda b, g, c: (b, cidx(c), direction, g, 0, 0)),
        ]

    fwd = lambda c: c
    bwd = lambda c: nc - 1 - c
    y_shape = jax.ShapeDtypeStruct((bsz, s, d_inner), F32)
    return pl.pallas_call(
        _ssd_scan_kernel,
        grid=(bsz, n_groups, nc),
        in_specs=specs(0, fwd) + specs(1, bwd) + [pl.BlockSpec((1, gw), lambda b, g, c: (0, g))],
        out_specs=[pl.BlockSpec((1, SSD_CHUNK, gw), lambda b, g, c: (b, c, g)),
                   pl.BlockSpec((1, SSD_CHUNK, gw), lambda b, g, c: (b, nc - 1 - c, g))],
        out_shape=[y_shape, y_shape],
        scratch_shapes=[pltpu.VMEM((D_STATE, gw), F32), pltpu.VMEM((D_STATE, gw), F32)],
        compiler_params=_params("parallel", "parallel", "arbitrary"),
        name="ssd_scan",
    )(xbc, xbc, xbc, dt_c, dt_r, e_c, e_r, tot6,
      xbc, xbc, xbc, dt_c, dt_r, e_c, e_r, tot6, dskip)


def _gate_kernel(yf_ref, yb_ref, z_ref, nw_ref, o_ref):
    z = z_ref[...]
    y = (yf_ref[...] + yb_ref[...]) * (z / (1.0 + jnp.exp(-z)))
    y = y * lax.rsqrt(jnp.mean(y * y, axis=-1, keepdims=True) + RMS_EPS) * nw_ref[...]
    o_ref[...] = y.astype(BF16)


def _ssd_gate(yf2d, yb2d, zx2d, norm_w):
    n, d_inner = yf2d.shape
    tm = _pick(n, 256)
    blk = pl.BlockSpec((tm, d_inner), lambda i: (i, 0))
    return pl.pallas_call(
        _gate_kernel,
        grid=(n // tm,),
        in_specs=[blk, blk, blk, pl.BlockSpec((1, d_inner), lambda i: (0, 0))],
        out_specs=blk,
        out_shape=jax.ShapeDtypeStruct((n, d_inner), BF16),
        compiler_params=_params("parallel"),
        name="ssd_gate_norm",
    )(yf2d, yb2d, zx2d, norm_w.reshape(1, d_inner))


def _ffn_kernel(xs_ref, wg_ref, wu_ref, wd_ref, o_ref):
    xs = xs_ref[0]
    hg = jnp.dot(xs, wg_ref[0], preferred_element_type=F32)
    hu = jnp.dot(xs, wu_ref[0], preferred_element_type=F32)
    h = (hg / (1.0 + jnp.exp(-hg)) * hu).astype(BF16)
    o_ref[0] = jnp.dot(h, wd_ref[0], preferred_element_type=F32).astype(BF16)


def _moe_ffn(xs, wg, wu, wd):
    n_exp, cap, d = xs.shape
    f = wg.shape[2]
    tile = _pick(cap, 256)
    return pl.pallas_call(
        _ffn_kernel,
        grid=(n_exp, cap // tile),
        in_specs=[pl.BlockSpec((1, tile, d), lambda e, t: (e, t, 0)),
                  pl.BlockSpec((1, d, f), lambda e, t: (e, 0, 0)),
                  pl.BlockSpec((1, d, f), lambda e, t: (e, 0, 0)),
                  pl.BlockSpec((1, f, d), lambda e, t: (e, 0, 0))],
        out_specs=pl.BlockSpec((1, tile, d), lambda e, t: (e, t, 0)),
        out_shape=jax.ShapeDtypeStruct((n_exp, cap, d), BF16),
        compiler_params=_params("parallel", "arbitrary"),
        name="moe_ffn",
    )(xs, wg, wu, wd)


SELECT_ROW_TILE = 512


def _select_kernel(aff_ref, pos_ref, cnt_ref, *, cap, n_exp):
    a = aff_ref[...]
    r = a.shape[0]
    bits = pltpu.bitcast(a, jnp.int32)

    def fold(v):
        sh = n_exp
        while sh < LANES:
            v = v + pltpu.roll(v, sh, 1)
            sh *= 2
        return v

    def count(mask):
        return fold(jnp.sum(jnp.where(mask, 1.0, 0.0), axis=0, keepdims=True))

    def search(i, thr):
        cand = thr | jnp.left_shift(jnp.int32(1), 30 - i)
        return jnp.where(count(bits >= cand) >= cap, cand, thr)

    thr = lax.fori_loop(0, 31, search, jnp.zeros((1, LANES), jnp.int32))
    above = bits > thr
    tied = bits == thr
    need = cap - count(above)

    li = lax.broadcasted_iota(jnp.int32, (LANES, 2 * LANES), 0)
    ci = lax.broadcasted_iota(jnp.int32, (LANES, 2 * LANES), 1)
    same_exp = (li & (n_exp - 1)) == (ci & (n_exp - 1))
    earlier = (li // n_exp) < ((ci & (LANES - 1)) // n_exp)
    w2 = jnp.where(same_exp & ((ci >= LANES) | earlier), 1.0, 0.0).astype(BF16)
    tr = min(SELECT_ROW_TILE, r)
    rr = lax.broadcasted_iota(jnp.int32, (tr, tr), 0)
    rc = lax.broadcasted_iota(jnp.int32, (tr, tr), 1)
    rows_before = jnp.where(rr > rc, 1.0, 0.0).astype(BF16)

    def prefix(mask):
        lw = jnp.dot(jnp.where(mask, 1.0, 0.0).astype(BF16), w2, preferred_element_type=F32)
        within, row_tot = lw[:, :LANES], lw[:, LANES:]
        carry = jnp.zeros((1, LANES), F32)
        outs = []
        for t in range(r // tr):
            rt = row_tot[t * tr:(t + 1) * tr]
            outs.append(jnp.dot(rows_before, rt.astype(BF16), preferred_element_type=F32) + carry
                        + within[t * tr:(t + 1) * tr])
            carry = carry + jnp.sum(rt, axis=0, keepdims=True)
        return jnp.concatenate(outs, axis=0)

    sel = above | (tied & (prefix(tied) < need))
    cnt = prefix(sel).astype(jnp.int32)
    pos_ref[...] = jnp.where(sel, cnt, -1)
    cnt_ref[...] = cnt


def _ec_select(aff_group, cap):
    n_g, n_exp = aff_group.shape
    assert LANES % n_exp == 0 and (n_exp & (n_exp - 1)) == 0
    r = n_g * n_exp // LANES
    assert r % min(SELECT_ROW_TILE, r) == 0
    shp = jax.ShapeDtypeStruct((r, LANES), jnp.int32)
    pos, cnt = pl.pallas_call(
        functools.partial(_select_kernel, cap=cap, n_exp=n_exp),
        out_shape=[shp, shp],
        compiler_params=pltpu.CompilerParams(vmem_limit_bytes=V7X_VMEM_LIMIT_BYTES),
        name="ec_select",
    )(aff_group.reshape(r, LANES))
    return pos.reshape(n_g, n_exp), cnt.reshape(n_g, n_exp)


COMBINE_TOKENS = 256
COMBINE_WINDOW = 64
BF16_ROWS = 16


def _combine_kernel(st_ref, en_ref, x_ref, pos_ref, aff_ref, g_ref, b_ref, o_hbm, out_ref, buf, sem, xbuf, xsem,
                    acc_ref, *, alpha, n_exp, win, cap_tot, n_tiles):
    i = pl.program_id(0)
    slot = i % 2

    def window(tile, e):
        s0 = st_ref[tile * n_exp + e]
        return jnp.minimum(lax.shift_left(lax.shift_right_logical(s0, 4), 4), cap_tot - win)

    def fetch(tile, sl, e):
        w = pl.multiple_of(window(tile, e), BF16_ROWS)
        return pltpu.make_async_copy(o_hbm.at[e, pl.ds(w, win), :], buf.at[sl, pl.ds(e * win, win), :], sem.at[sl, e])

    @pl.when(i == 0)
    def _():
        for e in range(n_exp):
            fetch(0, 0, e).start()

    @pl.when(i + 1 < n_tiles)
    def _():
        for e in range(n_exp):
            fetch(i + 1, 1 - slot, e).start()

    pos = pos_ref[...]
    aff = aff_ref[...]
    t = pos.shape[0]
    lane = lax.broadcasted_iota(jnp.int32, (t, 2 * win), 1)
    first = lane < win
    lane_in = jnp.where(first, lane, lane - win)
    parts = []
    for e in range(0, n_exp, 2):
        rel = jnp.where(first, pos[:, e:e + 1] - window(i, e), pos[:, e + 1:e + 2] - window(i, e + 1))
        gate = jnp.where(first, aff[:, e:e + 1], aff[:, e + 1:e + 2])
        parts.append(jnp.where(rel == lane_in, gate, 0.0))
    pmat = jnp.concatenate(parts, axis=1)
    p_hi = pmat.astype(BF16)
    p_lo = (pmat - p_hi.astype(F32)).astype(BF16)
    for e in range(n_exp):
        fetch(i, slot, e).wait()
    rows = buf[slot]
    acc_ref[...] = jnp.dot(p_hi, rows, preferred_element_type=F32) + jnp.dot(p_lo, rows, preferred_element_type=F32)

    lane1 = lax.broadcasted_iota(jnp.int32, (t, win), 1)
    for e in range(n_exp):
        w = window(i, e)
        s1 = en_ref[i * n_exp + e]
        n_extra = jnp.maximum(s1 - w - 1, 0) // win

        def extra(k, carry, e=e, w=w):
            lo = w + win * (k + 1)
            wk = pl.multiple_of(jnp.minimum(lo, cap_tot - win), BF16_ROWS)
            cp = pltpu.make_async_copy(o_hbm.at[e, pl.ds(wk, win), :], xbuf, xsem)
            cp.start()
            cp.wait()
            pe = pos[:, e:e + 1]
            oh = jnp.where((pe - wk == lane1) & (pe >= lo), 1.0, 0.0).astype(BF16)
            acc_ref[...] += jnp.dot(oh, xbuf[...], preferred_element_type=F32) * aff[:, e:e + 1]
            return carry

        lax.fori_loop(0, n_extra, extra, 0)
    out_ref[...] = _res_ln(x_ref[...], acc_ref[...], g_ref[...], b_ref[...], alpha)


def _moe_combine_ln(x2d, pos, aff, starts, ends, o, g, b, alpha):
    n, d = x2d.shape
    n_exp, cap_tot, _ = o.shape
    t = COMBINE_TOKENS
    win = COMBINE_WINDOW
    assert n % t == 0 and cap_tot % BF16_ROWS == 0 and cap_tot >= win and n_exp % 2 == 0 and 2 * win == LANES
    n_tiles = n // t
    kern = functools.partial(_combine_kernel, alpha=alpha, n_exp=n_exp, win=win, cap_tot=cap_tot, n_tiles=n_tiles)
    row = pl.BlockSpec((1, d), lambda i, st, en: (0, 0))
    tok = lambda w: pl.BlockSpec((t, w), lambda i, st, en: (i, 0))
    grid_spec = pltpu.PrefetchScalarGridSpec(
        num_scalar_prefetch=2,
        grid=(n_tiles,),
        in_specs=[tok(d), tok(n_exp), tok(n_exp), row, row, pl.BlockSpec(memory_space=pl.ANY)],
        out_specs=tok(d),
        scratch_shapes=[pltpu.VMEM((2, n_exp * win, d), BF16), pltpu.SemaphoreType.DMA((2, n_exp)),
                        pltpu.VMEM((win, d), BF16), pltpu.SemaphoreType.DMA(()), pltpu.VMEM((t, d), F32)],
    )
    return pl.pallas_call(
        kern,
        grid_spec=grid_spec,
        out_shape=jax.ShapeDtypeStruct((n, d), F32),
        compiler_params=_params("arbitrary"),
        name="moe_combine_ln",
    )(starts, ends, x2d, pos, aff, g.reshape(1, d), b.reshape(1, d), o)


def _ec_moe_ln(x2d, aff, groups, wg, wu, wd, g, b, alpha):
    n, d = x2d.shape
    n_exp = aff.shape[1]
    t = COMBINE_TOKENS
    pos_l, st_l, en_l = [], [], []
    off = 0
    for start, cnt_tok in groups:
        assert start % t == 0 and cnt_tok % t == 0
        cap = EC_CAPACITY_FACTOR * cnt_tok // n_exp
        pos, cnt = _ec_select(aff[start:start + cnt_tok], cap)
        st = cnt[::t] + off
        en = jnp.concatenate([st[1:], jnp.full((1, n_exp), off + cap, jnp.int32)], axis=0)
        pos_l.append(jnp.where(pos >= 0, pos + off, -1))
        st_l.append(st)
        en_l.append(en)
        off += cap
    cap_tot = off
    pos = jnp.concatenate(pos_l, axis=0)
    starts = jnp.concatenate(st_l, axis=0).reshape(-1)
    ends = jnp.concatenate(en_l, axis=0).reshape(-1)
    flat = jnp.where(pos >= 0, pos + jnp.arange(n_exp, dtype=jnp.int32)[None, :] * cap_tot, n_exp * cap_tot)
    tok = jnp.broadcast_to(jnp.arange(n, dtype=jnp.int32)[:, None], (n, n_exp))
    idx = jnp.zeros((n_exp * cap_tot,), jnp.int32).at[flat.reshape(-1)].set(tok.reshape(-1), mode="drop")
    xs = x2d.astype(BF16)[idx].reshape(n_exp, cap_tot, d)
    o = _moe_ffn(xs, wg, wu, wd)
    return _moe_combine_ln(x2d, pos, aff, starts, ends, o, g, b, alpha)


def _split2_bf16(w):
    hi = w.astype(BF16)
    lo = (w - hi.astype(F32)).astype(BF16)
    return jnp.concatenate([hi, lo], axis=1)


def kernel(x_prompt, x_sample, attn_w_qkv, attn_q_norm, attn_k_norm, attn_w_o, pool_w, pool_scale, ssd_w_in,
           ssd_conv_w, ssd_conv_b, ssd_dt_bias, ssd_A_log, ssd_D, ssd_norm, ssd_w_out, moe_w_router, moe_w_gate,
           moe_w_up, moe_w_down, ln_g, ln_b):
    bp, s, d = x_prompt.shape
    bs = x_sample.shape[0]
    assert x_sample.shape[1] == s
    bsz = bp + bs
    n = bsz * s
    groups = [(0, bp * s), (bp * s, bs * s)]
    depth = ln_g.shape[0]
    alpha = (2 * depth) ** 0.25
    n_heads = attn_w_o.shape[1] // HEAD_DIM
    n_kv = (attn_w_qkv.shape[2] // HEAD_DIM - n_heads) // 2
    d_inner = ssd_w_out.shape[1]
    n_ssd_heads = ssd_A_log.shape[-1]
    assert d_inner == n_ssd_heads * SSD_HEAD_DIM
    conv_dim = ssd_conv_w.shape[2]
    n_groups = (conv_dim - d_inner) // (2 * D_STATE)
    rope = _rope_tables(s)

    x = jnp.concatenate([x_prompt, x_sample], axis=0).reshape(n, d)
    ia = ip = isd = 0
    for i in range(depth):
        wr2 = _split2_bf16(moe_w_router[i])
        g1, b1, g2, b2 = ln_g[i, 0], ln_b[i, 0], ln_g[i, 1], ln_b[i, 1]
        kind = i % 3
        if kind == 0:
            qkv = _qkv_proj(x, attn_w_qkv[ia].astype(BF16), attn_q_norm[ia], attn_k_norm[ia], rope, s, n_heads, n_kv)
            qkv3 = qkv.reshape(bsz, s, -1)
            vt = jnp.swapaxes(qkv3[:, :, (n_heads + n_kv) * HEAD_DIM:], 1, 2)
            o = _flash_attention(qkv3, vt, n_heads, n_kv)
            x, aff = _mm_res_ln_router(o.reshape(n, -1), attn_w_o[ia].astype(BF16), x, g1, b1, wr2, alpha)
            ia += 1
        elif kind == 1:
            x3, aff3 = _pool_layer(x.reshape(bsz, s, d), pool_w[ip].astype(BF16), pool_scale[ip], g1, b1, wr2, alpha)
            x, aff = x3.reshape(n, d), aff3.reshape(n, -1)
            ip += 1
        else:
            zx = _matmul_f32(x, ssd_w_in[isd].astype(BF16), 1152)
            zx3 = zx.reshape(bsz, s, -1)
            xbc = _ssd_conv(zx3, ssd_conv_w[isd], ssd_conv_b[isd], d_inner, conv_dim)
            a_neg = -jnp.exp(ssd_A_log[isd].astype(F32)).reshape(-1)
            dt, ecs, tot = _ssd_dt(zx3, ssd_dt_bias[isd].reshape(-1), a_neg, d_inner + conv_dim)
            yf, yb = _ssd_scan(xbc, dt, ecs, tot, ssd_D[isd], d_inner, n_groups)
            yn = _ssd_gate(yf.reshape(n, d_inner), yb.reshape(n, d_inner), zx, ssd_norm[isd])
            x, aff = _mm_res_ln_router(yn, ssd_w_out[isd].astype(BF16), x, g1, b1, wr2, alpha)
            isd += 1
        x = _ec_moe_ln(x, aff, groups, moe_w_gate[i].astype(BF16), moe_w_up[i].astype(BF16),
                       moe_w_down[i].astype(BF16), g2, b2, alpha)
    y = x.reshape(bsz, s, d)
    return y[:bp], y[bp:]
```

```python
import functools
import math

import jax
import jax.numpy as jnp
from jax import lax
from jax.experimental import pallas as pl
from jax.experimental.pallas import tpu as pltpu

F32 = jnp.float32
BF16 = jnp.bfloat16

HEAD_DIM = 128
GRID_W = 64
ROPE_THETA = 10000.0
POOL_WINDOWS = (2, 4, 8, 16)
POOL_HALO = 8
D_STATE = 128
SSD_CHUNK = 128
SSD_HEAD_DIM = 64
SSD_HEADS_PER_GROUP = 8
D_CONV = 4
CONV_LEFT = D_CONV // 2
EC_CAPACITY_FACTOR = 2
LN_EPS = 1e-5
RMS_EPS = 1e-6
LOG2E = 1.4426950408889634

V7X_VMEM_LIMIT_BYTES = 52 * 1024 * 1024
LANES = 128


def _params(*sem):
    return pltpu.CompilerParams(dimension_semantics=sem, vmem_limit_bytes=V7X_VMEM_LIMIT_BYTES)


def _pick(n, pref):
    t = min(n, pref)
    while n % t:
        t //= 2
    return t


def _res_ln(x, h, g, b, alpha):
    y = alpha * x + h
    mu = jnp.mean(y, axis=-1, keepdims=True)
    yc = y - mu
    var = jnp.mean(yc * yc, axis=-1, keepdims=True)
    return yc * lax.rsqrt(var + LN_EPS) * g + b


def _router_affinity(xn, wr_ref, n_exp):
    xh = xn.astype(BF16)
    xl = (xn - xh.astype(F32)).astype(BF16)
    wr = wr_ref[...]
    r1 = jnp.dot(xh, wr, preferred_element_type=F32)
    r2 = jnp.dot(xl, wr[:, :n_exp], preferred_element_type=F32)
    logits = r1[:, :n_exp] + (r1[:, n_exp:] + r2)
    m = jnp.max(logits, axis=-1, keepdims=True)
    e = jnp.exp(logits - m)
    return e / jnp.sum(e, axis=-1, keepdims=True)


def _qkv_kernel(x_ref, w_ref, cos_ref, sa_ref, sb_ref, qn_ref, kn_ref, o_ref, xb_ref, *,
                n_q_tiles, n_k_tiles, heads_per_tile, q_scale):
    j = pl.program_id(1)

    @pl.when(j == 0)
    def _():
        xb_ref[...] = x_ref[...].astype(BF16)

    acc = jnp.dot(xb_ref[...], w_ref[...], preferred_element_type=F32)

    def norm_rope(gain_ref, scale):
        cos = cos_ref[...]
        sa = sa_ref[...]
        sb = sb_ref[...]
        g = gain_ref[...]
        for h in range(heads_per_tile):
            a = acc[:, h * HEAD_DIM:(h + 1) * HEAD_DIM]
            a = a * lax.rsqrt(jnp.mean(a * a, axis=-1, keepdims=True) + RMS_EPS) * g
            r = a * cos + pltpu.roll(a, HEAD_DIM - 32, 1) * sa + pltpu.roll(a, 32, 1) * sb
            o_ref[:, h * HEAD_DIM:(h + 1) * HEAD_DIM] = (r * scale).astype(BF16)

    @pl.when(j < n_q_tiles)
    def _():
        norm_rope(qn_ref, q_scale)

    @pl.when((j >= n_q_tiles) & (j < n_q_tiles + n_k_tiles))
    def _():
        norm_rope(kn_ref, 1.0)

    @pl.when(j >= n_q_tiles + n_k_tiles)
    def _():
        o_ref[...] = acc.astype(BF16)


def _rope_tables(seq_len):
    rows = seq_len // GRID_W
    row = jnp.repeat(jnp.arange(rows, dtype=F32), GRID_W)
    col = jnp.tile(jnp.arange(GRID_W, dtype=F32), rows)
    half = HEAD_DIM // 4
    inv_freq = ROPE_THETA ** (-jnp.arange(0, HEAD_DIM // 2, 2, dtype=F32) / (HEAD_DIM // 2))
    ang_r = row[:, None] * inv_freq
    ang_c = col[:, None] * inv_freq
    zeros = jnp.zeros((seq_len, half), F32)
    cos = jnp.concatenate([jnp.cos(ang_r)] * 2 + [jnp.cos(ang_c)] * 2, axis=-1)
    sin_a = jnp.concatenate([-jnp.sin(ang_r), zeros, -jnp.sin(ang_c), zeros], axis=-1)
    sin_b = jnp.concatenate([zeros, jnp.sin(ang_r), zeros, jnp.sin(ang_c)], axis=-1)
    return cos, sin_a, sin_b


def _qkv_proj(x2d, w_bf16, q_norm, k_norm, rope, seq_len, n_heads, n_kv):
    n, d = x2d.shape
    qkv_dim = w_bf16.shape[1]
    tn = n_kv * HEAD_DIM
    tm = _pick(seq_len, 512)
    cos, sa, sb = rope
    nsb = seq_len // tm
    kern = functools.partial(
        _qkv_kernel, n_q_tiles=n_heads // n_kv, n_k_tiles=1, heads_per_tile=n_kv,
        q_scale=HEAD_DIM ** -0.5 * LOG2E)
    tab = pl.BlockSpec((tm, HEAD_DIM), lambda i, j: (i % nsb, 0))
    vec = pl.BlockSpec((1, HEAD_DIM), lambda i, j: (0, 0))
    return pl.pallas_call(
        kern,
        grid=(n // tm, qkv_dim // tn),
        in_specs=[pl.BlockSpec((tm, d), lambda i, j: (i, 0)),
                  pl.BlockSpec((d, tn), lambda i, j: (0, j)),
                  tab, tab, tab, vec, vec],
        out_specs=pl.BlockSpec((tm, tn), lambda i, j: (i, j)),
        out_shape=jax.ShapeDtypeStruct((n, qkv_dim), BF16),
        scratch_shapes=[pltpu.VMEM((tm, d), BF16)],
        compiler_params=_params("parallel", "arbitrary"),
        name="qkv_proj",
    )(x2d, w_bf16, cos, sa, sb, q_norm.reshape(1, HEAD_DIM), k_norm.reshape(1, HEAD_DIM))


FLASH_TQ = 128
FLASH_TK = 512


def _flash_kernel(q_ref, k_ref, vt_ref, o_ref, s_scr, p_scr, acc_scr, *, tk, group):
    tq = q_ref.shape[1]
    seq = k_ref.shape[1]
    q = jnp.concatenate([q_ref[0, :, g * HEAD_DIM:(g + 1) * HEAD_DIM] for g in range(group)], axis=0)
    rows = group * tq
    nc = seq // tk

    def scores(c, slot):
        k = k_ref[0, pl.ds(c * tk, tk), :]
        s_scr[slot] = lax.dot_general(k, q, (((1,), (1,)), ((), ())), preferred_element_type=F32)

    def pv(c, slot, alpha):
        vt = vt_ref[0, :, pl.ds(c * tk, tk)]
        acc_scr[...] = acc_scr[...] * alpha + jnp.dot(vt, p_scr[slot], preferred_element_type=F32)

    def softmax(slot, m, l):
        s = s_scr[slot]
        m_new = jnp.maximum(m, jnp.max(s, axis=0, keepdims=True))
        alpha = jnp.exp2(m - m_new)
        p = jnp.exp2(s - m_new)
        l = alpha * l + jnp.sum(p, axis=0, keepdims=True)
        p_scr[slot] = p.astype(BF16)
        return m_new, l, alpha

    m = jnp.full((1, rows), -jnp.inf, F32)
    l = jnp.zeros((1, rows), F32)
    acc_scr[...] = jnp.zeros_like(acc_scr)
    scores(0, 0)
    if nc > 1:
        scores(1, 1)
    m, l, alpha = softmax(0, m, l)
    for c in range(1, nc):
        if c + 1 < nc:
            scores(c + 1, (c + 1) % 2)
        pv(c - 1, (c - 1) % 2, alpha)
        m, l, alpha = softmax(c % 2, m, l)
    pv(nc - 1, (nc - 1) % 2, alpha)
    o = (acc_scr[...] / l).T
    for g in range(group):
        o_ref[0, :, g * HEAD_DIM:(g + 1) * HEAD_DIM] = o[g * tq:(g + 1) * tq].astype(BF16)


def _flash_attention(qkv, vt, n_heads, n_kv):
    b, s, _ = qkv.shape
    group = n_heads // n_kv
    tq = _pick(s, FLASH_TQ)
    tk = _pick(s, FLASH_TK)
    gw = group * HEAD_DIM
    rows = group * tq
    kern = functools.partial(_flash_kernel, tk=tk, group=group)
    return pl.pallas_call(
        kern,
        grid=(b, n_kv, s // tq),
        in_specs=[pl.BlockSpec((1, tq, gw), lambda bi, h, i: (bi, i, h)),
                  pl.BlockSpec((1, s, HEAD_DIM), lambda bi, h, i: (bi, 0, n_heads + h)),
                  pl.BlockSpec((1, HEAD_DIM, s), lambda bi, h, i: (bi, h, 0))],
        out_specs=pl.BlockSpec((1, tq, gw), lambda bi, h, i: (bi, i, h)),
        out_shape=jax.ShapeDtypeStruct((b, s, n_heads * HEAD_DIM), BF16),
        scratch_shapes=[pltpu.VMEM((2, tk, rows), F32), pltpu.VMEM((2, tk, rows), BF16),
                        pltpu.VMEM((HEAD_DIM, rows), F32)],
        compiler_params=_params("parallel", "parallel", "arbitrary"),
        name="flash_attention",
    )(qkv, qkv, vt)


def _mm_ln_kernel(a_ref, w_ref, x_ref, g_ref, b_ref, wr_ref, o_ref, aff_ref, acc_ref, *, alpha, nk, n_exp):
    k = pl.program_id(1)

    @pl.when(k == 0)
    def _():
        acc_ref[...] = jnp.zeros_like(acc_ref)

    acc_ref[...] += jnp.dot(a_ref[...], w_ref[...], preferred_element_type=F32)

    @pl.when(k == nk - 1)
    def _():
        xn = _res_ln(x_ref[...], acc_ref[...], g_ref[...], b_ref[...], alpha)
        o_ref[...] = xn
        aff_ref[...] = _router_affinity(xn, wr_ref, n_exp)


def _mm_res_ln_router(a_bf16, w_bf16, x2d, g, b, wr2, alpha):
    n, kdim = a_bf16.shape
    d = w_bf16.shape[1]
    n_exp = wr2.shape[1] // 2
    tm = _pick(n, 512)
    tk = _pick(kdim, 512)
    nk = kdim // tk
    kern = functools.partial(_mm_ln_kernel, alpha=alpha, nk=nk, n_exp=n_exp)
    row = pl.BlockSpec((1, d), lambda i, k: (0, 0))
    return pl.pallas_call(
        kern,
        grid=(n // tm, nk),
        in_specs=[pl.BlockSpec((tm, tk), lambda i, k: (i, k)),
                  pl.BlockSpec((tk, d), lambda i, k: (k, 0)),
                  pl.BlockSpec((tm, d), lambda i, k: (i, 0)),
                  row, row,
                  pl.BlockSpec((d, 2 * n_exp), lambda i, k: (0, 0))],
        out_specs=[pl.BlockSpec((tm, d), lambda i, k: (i, 0)),
                   pl.BlockSpec((tm, n_exp), lambda i, k: (i, 0))],
        out_shape=[jax.ShapeDtypeStruct((n, d), F32), jax.ShapeDtypeStruct((n, n_exp), F32)],
        scratch_shapes=[pltpu.VMEM((tm, d), F32)],
        compiler_params=_params("parallel", "arbitrary"),
        name="mm_res_ln_router",
    )(a_bf16, w_bf16, x2d, g.reshape(1, d), b.reshape(1, d), wr2)


def _pool_kernel(prev_ref, cur_ref, next_ref, w_ref, sc_ref, g_ref, b_ref, wr_ref, o_ref, aff_ref, ext_ref, *,
                 alpha, nt, seq_len, n_exp):
    i = pl.program_id(1)
    t = cur_ref.shape[1]
    d = cur_ref.shape[2]
    pg = d // len(POOL_WINDOWS)
    x = cur_ref[0]
    ext_ref[0:POOL_HALO, :] = jnp.where(i == 0, 0.0, prev_ref[0])
    ext_ref[POOL_HALO:POOL_HALO + t, :] = x
    ext_ref[POOL_HALO + t:2 * POOL_HALO + t, :] = jnp.where(i == nt - 1, 0.0, next_ref[0])
    pos = i * t + lax.broadcasted_iota(jnp.int32, (t, 1), 0)
    hs = []
    for gi, w in enumerate(POOL_WINDOWS):
        half = w // 2
        cols = slice(gi * pg, (gi + 1) * pg)
        acc = ext_ref[pl.ds(POOL_HALO - half, t), cols]
        for jj in range(1, w):
            acc = acc + ext_ref[pl.ds(POOL_HALO - half + jj, t), cols]
        cnt = (jnp.minimum(pos + half, seq_len) - jnp.maximum(pos - half, 0)).astype(F32)
        mixed = (acc / cnt - x[:, cols]).astype(BF16)
        hs.append(jnp.dot(mixed, w_ref[gi], preferred_element_type=F32))
    h = jnp.concatenate(hs, axis=-1) * sc_ref[...]
    xn = _res_ln(x, h, g_ref[...], b_ref[...], alpha)
    o_ref[0] = xn
    aff_ref[0] = _router_affinity(xn, wr_ref, n_exp)


def _pool_layer(x3d, w_bf16, scale, g, b, wr2, alpha):
    bsz, s, d = x3d.shape
    n_exp = wr2.shape[1] // 2
    t = _pick(s, 256)
    nt = s // t
    hb = t // POOL_HALO
    last_hb = s // POOL_HALO - 1
    pg = d // len(POOL_WINDOWS)
    kern = functools.partial(_pool_kernel, alpha=alpha, nt=nt, seq_len=s, n_exp=n_exp)
    row = pl.BlockSpec((1, d), lambda bi, i: (0, 0))
    return pl.pallas_call(
        kern,
        grid=(bsz, nt),
        in_specs=[pl.BlockSpec((1, POOL_HALO, d), lambda bi, i: (bi, jnp.maximum(i * hb - 1, 0), 0)),
                  pl.BlockSpec((1, t, d), lambda bi, i: (bi, i, 0)),
                  pl.BlockSpec((1, POOL_HALO, d), lambda bi, i: (bi, jnp.minimum((i + 1) * hb, last_hb), 0)),
                  pl.BlockSpec((len(POOL_WINDOWS), pg, pg), lambda bi, i: (0, 0, 0)),
                  row, row, row,
                  pl.BlockSpec((d, 2 * n_exp), lambda bi, i: (0, 0))],
        out_specs=[pl.BlockSpec((1, t, d), lambda bi, i: (bi, i, 0)),
                   pl.BlockSpec((1, t, n_exp), lambda bi, i: (bi, i, 0))],
        out_shape=[jax.ShapeDtypeStruct((bsz, s, d), F32), jax.ShapeDtypeStruct((bsz, s, n_exp), F32)],
        scratch_shapes=[pltpu.VMEM((t + 2 * POOL_HALO, d), F32)],
        compiler_params=_params("parallel", "parallel"),
        name="pool_mixer",
    )(x3d, x3d, x3d, w_bf16, scale.reshape(1, d), g.reshape(1, d), b.reshape(1, d), wr2)


def _mm_kernel(x_ref, w_ref, o_ref, xb_ref):
    @pl.when(pl.program_id(1) == 0)
    def _():
        xb_ref[...] = x_ref[...].astype(BF16)

    o_ref[...] = jnp.dot(xb_ref[...], w_ref[...], preferred_element_type=F32)


def _matmul_f32(x2d, w_bf16, tn_pref):
    n, d = x2d.shape
    nout = w_bf16.shape[1]
    tm = _pick(n, 512)
    tn = tn_pref
    assert nout % tn == 0
    return pl.pallas_call(
        _mm_kernel,
        grid=(n // tm, nout // tn),
        in_specs=[pl.BlockSpec((tm, d), lambda i, j: (i, 0)),
                  pl.BlockSpec((d, tn), lambda i, j: (0, j))],
        out_specs=pl.BlockSpec((tm, tn), lambda i, j: (i, j)),
        out_shape=jax.ShapeDtypeStruct((n, nout), F32),
        scratch_shapes=[pltpu.VMEM((tm, d), BF16)],
        compiler_params=_params("parallel", "arbitrary"),
        name="ssd_in_proj",
    )(x2d, w_bf16)


def _conv_kernel(prev_ref, cur_ref, next_ref, w_ref, b_ref, o_ref, ext_ref, *, nt):
    i = pl.program_id(1)
    t = cur_ref.shape[1]
    ext_ref[0:POOL_HALO, :] = jnp.where(i == 0, 0.0, prev_ref[0])
    ext_ref[POOL_HALO:POOL_HALO + t, :] = cur_ref[0]
    ext_ref[POOL_HALO + t:2 * POOL_HALO + t, :] = jnp.where(i == nt - 1, 0.0, next_ref[0])
    acc = ext_ref[pl.ds(POOL_HALO - CONV_LEFT, t), :] * w_ref[0:1, :]
    for kk in range(1, D_CONV):
        acc = acc + ext_ref[pl.ds(POOL_HALO - CONV_LEFT + kk, t), :] * w_ref[kk:kk + 1, :]
    acc = acc + b_ref[...]
    o_ref[0] = acc / (1.0 + jnp.exp(-acc))


def _ssd_conv(zx3d, conv_w, conv_b, d_inner, conv_dim):
    bsz, s, _ = zx3d.shape
    tc = 512
    t = _pick(s, 512)
    nt = s // t
    hb = t // POOL_HALO
    last_hb = s // POOL_HALO - 1
    c0 = d_inner // tc
    return pl.pallas_call(
        functools.partial(_conv_kernel, nt=nt),
        grid=(bsz, nt, conv_dim // tc),
        in_specs=[pl.BlockSpec((1, POOL_HALO, tc), lambda bi, i, j: (bi, jnp.maximum(i * hb - 1, 0), c0 + j)),
                  pl.BlockSpec((1, t, tc), lambda bi, i, j: (bi, i, c0 + j)),
                  pl.BlockSpec((1, POOL_HALO, tc), lambda bi, i, j: (bi, jnp.minimum((i + 1) * hb, last_hb), c0 + j)),
                  pl.BlockSpec((D_CONV, tc), lambda bi, i, j: (0, j)),
                  pl.BlockSpec((1, tc), lambda bi, i, j: (0, j))],
        out_specs=pl.BlockSpec((1, t, tc), lambda bi, i, j: (bi, i, j)),
        out_shape=jax.ShapeDtypeStruct((bsz, s, conv_dim), F32),
        scratch_shapes=[pltpu.VMEM((t + 2 * POOL_HALO, tc), F32)],
        compiler_params=_params("parallel", "parallel", "parallel"),
        name="ssd_conv",
    )(zx3d, zx3d, zx3d, conv_w, conv_b.reshape(1, conv_dim))


def _split3(x):
    hi = x.astype(BF16)
    r1 = x - hi.astype(F32)
    mid = r1.astype(BF16)
    lo = (r1 - mid.astype(F32)).astype(BF16)
    return hi, mid, lo


def _dt_kernel(raw_ref, bias_ref, a_ref, dt_ref, e_ref, tot_ref):
    v = raw_ref[0] + bias_ref[...]
    dt = jnp.maximum(v, 0.0) + jnp.log1p(jnp.exp(-jnp.abs(v)))
    dt_ref[0] = dt
    a = dt * a_ref[...]
    q, w = a.shape
    li = lax.broadcasted_iota(jnp.int32, (q, q), 0)
    si = lax.broadcasted_iota(jnp.int32, (q, q), 1)
    tri = jnp.where(li >= si, 1.0, 0.0).astype(BF16)
    hi, mid, lo = _split3(a)
    cs = (jnp.dot(tri, lo, preferred_element_type=F32) + jnp.dot(tri, mid, preferred_element_type=F32)
          + jnp.dot(tri, hi, preferred_element_type=F32))
    lane = lax.broadcasted_iota(jnp.int32, (q, w), 1)
    e_ref[0] = jnp.where(lane < w // 2, cs, cs - a)
    tot_ref[0, 0] = cs[q - 1:q, :]


def _ssd_dt(zx3d, dt_bias, a_neg, col0):
    bsz, s, _ = zx3d.shape
    w = dt_bias.shape[-1]
    assert w == LANES and col0 % LANES == 0
    nc = s // SSD_CHUNK
    blk = pl.BlockSpec((1, SSD_CHUNK, w), lambda bi, c: (bi, c, 0))
    row = pl.BlockSpec((1, w), lambda bi, c: (0, 0))
    return pl.pallas_call(
        _dt_kernel,
        grid=(bsz, nc),
        in_specs=[pl.BlockSpec((1, SSD_CHUNK, w), lambda bi, c: (bi, c, col0 // LANES)), row, row],
        out_specs=[blk, blk, pl.BlockSpec((1, 1, 1, w), lambda bi, c: (bi, c, 0, 0))],
        out_shape=[jax.ShapeDtypeStruct((bsz, s, w), F32), jax.ShapeDtypeStruct((bsz, s, w), F32),
                   jax.ShapeDtypeStruct((bsz, nc, 1, w), F32)],
        compiler_params=_params("parallel", "parallel"),
        name="ssd_dt",
    )(zx3d, dt_bias.reshape(1, w), a_neg.reshape(1, w))


def _expand_heads(v, width):
    m = v.shape[0]
    lane = lax.broadcasted_iota(jnp.int32, (m, LANES), 1)
    parts = []
    for pr in range(width // LANES):
        parts.append(jnp.where(lane < SSD_HEAD_DIM, v[:, 2 * pr:2 * pr + 1], v[:, 2 * pr + 1:2 * pr + 2]))
    return jnp.concatenate(parts, axis=1)


def _ssd_direction(x, bmat, cmat, dt_col, dt_row, e_col, e_row, tot, st_ref, forward):
    q, width = x.shape
    li = lax.broadcasted_iota(jnp.int32, (q, q), 0)
    si = lax.broadcasted_iota(jnp.int32, (q, q), 1)
    lane = lax.broadcasted_iota(jnp.int32, (q, LANES), 1)
    if forward:
        mask = li >= si
        out_dec = jnp.exp(e_col)
        st_w = dt_col * jnp.exp(tot - e_col)
    else:
        mask = si >= li
        out_dec = jnp.exp(tot - e_col)
        st_w = dt_col * jnp.exp(e_col)
    cb = lax.dot_general(cmat.astype(BF16), bmat.astype(BF16), (((1,), (1,)), ((), ())),
                         preferred_element_type=F32)
    xb = x.astype(BF16)
    y_parts = []
    for pr in range(width // LANES):
        ms = []
        for r in (2 * pr, 2 * pr + 1):
            if forward:
                diff = e_col[:, r:r + 1] - e_row[r:r + 1, :]
            else:
                diff = e_row[r:r + 1, :] - e_col[:, r:r + 1]
            decay = jnp.exp(jnp.where(mask, diff, -jnp.inf))
            ms.append((decay * cb * dt_row[r:r + 1, :]).astype(BF16))
        xp = xb[:, pr * LANES:(pr + 1) * LANES]
        zero = jnp.zeros_like(xp)
        rhs = jnp.concatenate([jnp.where(lane < SSD_HEAD_DIM, xp, zero),
                               jnp.where(lane >= SSD_HEAD_DIM, xp, zero)], axis=0)
        y_parts.append(jnp.dot(jnp.concatenate(ms, axis=1), rhs, preferred_element_type=F32))
    y = jnp.concatenate(y_parts, axis=1)
    st = st_ref[...]
    y = y + jnp.dot(cmat.astype(BF16), st.astype(BF16), preferred_element_type=F32) * _expand_heads(out_dec, width)
    xd = (x * _expand_heads(st_w, width)).astype(BF16)
    st_new = lax.dot_general(bmat.astype(BF16), xd, (((0,), (0,)), ((), ())), preferred_element_type=F32)
    chunk_dec = _expand_heads(jnp.broadcast_to(jnp.exp(tot), (8, tot.shape[1])), width)[0:1, :]
    st_ref[...] = st * chunk_dec + st_new
    return y


def _ssd_scan_kernel(xf_ref, bf_ref, cf_ref, dcf_ref, drf_ref, ecf_ref, erf_ref, tf_ref,
                     xr_ref, br_ref, cr_ref, dcr_ref, drr_ref, ecr_ref, err_ref, tr_ref,
                     dskip_ref, yf_ref, yb_ref, stf_ref, stb_ref):
    @pl.when(pl.program_id(2) == 0)
    def _():
        stf_ref[...] = jnp.zeros_like(stf_ref)
        stb_ref[...] = jnp.zeros_like(stb_ref)

    xf = xf_ref[0]
    yf = _ssd_direction(xf, bf_ref[0], cf_ref[0], dcf_ref[0, 0, 0], drf_ref[0, 0, 0], ecf_ref[0, 0, 0],
                        erf_ref[0, 0, 0], tf_ref[0, 0, 0, 0], stf_ref, True)
    yf_ref[0] = yf + dskip_ref[...] * xf
    yb_ref[0] = _ssd_direction(xr_ref[0], br_ref[0], cr_ref[0], dcr_ref[0, 0, 0], drr_ref[0, 0, 0], ecr_ref[0, 0, 0],
                               err_ref[0, 0, 0], tr_ref[0, 0, 0, 0], stb_ref, False)


def _ssd_scan(xbc, dt, ecs, tot, d_skip, d_inner, n_groups):
    bsz, s, _ = xbc.shape
    nc = s // SSD_CHUNK
    hg = SSD_HEADS_PER_GROUP
    gw = hg * SSD_HEAD_DIM
    assert gw % LANES == 0 and d_inner == n_groups * gw
    b0 = d_inner // D_STATE
    c0 = b0 + n_groups

    def col_layout(a):
        return a.reshape(bsz, s, 2, n_groups, hg).transpose(0, 2, 3, 1, 4)

    def row_layout(a):
        return a.reshape(bsz, s, 2, n_groups, hg).transpose(0, 2, 3, 4, 1)

    dt_c, dt_r, e_c, e_r = col_layout(dt), row_layout(dt), col_layout(ecs), row_layout(ecs)
    tot6 = tot.reshape(bsz, nc, 2, n_groups, 1, hg)
    dskip = jnp.repeat(d_skip.astype(F32), SSD_HEAD_DIM).reshape(1, d_inner)

    def specs(direction, cidx):
        return [
            pl.BlockSpec((1, SSD_CHUNK, gw), lambda b, g, c: (b, cidx(c), g)),
            pl.BlockSpec((1, SSD_CHUNK, D_STATE), lambda b, g, c: (b, cidx(c), b0 + g)),
            pl.BlockSpec((1, SSD_CHUNK, D_STATE), lambda b, g, c: (b, cidx(c), c0 + g)),
            pl.BlockSpec((1, 1, 1, SSD_CHUNK, hg), lambda b, g, c: (b, direction, g, cidx(c), 0)),
            pl.BlockSpec((1, 1, 1, hg, SSD_CHUNK), lambda b, g, c: (b, direction, g, 0, cidx(c))),
            pl.BlockSpec((1, 1, 1, SSD_CHUNK, hg), lambda b, g, c: (b, direction, g, cidx(c), 0)),
            pl.BlockSpec((1, 1, 1, hg, SSD_CHUNK), lambda b, g, c: (b, direction, g, 0, cidx(c))),
            pl.BlockSpec((1, 1, 1, 1, 1, hg), lambda b, g, c: (b, cidx(c), direction, g, 0, 0)),
        ]

    fwd = lambda c: c
    bwd = lambda c: nc - 1 - c
    y_shape = jax.ShapeDtypeStruct((bsz, s, d_inner), F32)
    return pl.pallas_call(
        _ssd_scan_kernel,
        grid=(bsz, n_groups, nc),
        in_specs=specs(0, fwd) + specs(1, bwd) + [pl.BlockSpec((1, gw), lambda b, g, c: (0, g))],
        out_specs=[pl.BlockSpec((1, SSD_CHUNK, gw), lambda b, g, c: (b, c, g)),
                   pl.BlockSpec((1, SSD_CHUNK, gw), lambda b, g, c: (b, nc - 1 - c, g))],
        out_shape=[y_shape, y_shape],
        scratch_shapes=[pltpu.VMEM((D_STATE, gw), F32), pltpu.VMEM((D_STATE, gw), F32)],
        compiler_params=_params("parallel", "parallel", "arbitrary"),
        name="ssd_scan",
    )(xbc, xbc, xbc, dt_c, dt_r, e_c, e_r, tot6,
      xbc, xbc, xbc, dt_c, dt_r, e_c, e_r, tot6, dskip)


def _gate_kernel(yf_ref, yb_ref, z_ref, nw_ref, o_ref):
    z = z_ref[...]
    y = (yf_ref[...] + yb_ref[...]) * (z / (1.0 + jnp.exp(-z)))
    y = y * lax.rsqrt(jnp.mean(y * y, axis=-1, keepdims=True) + RMS_EPS) * nw_ref[...]
    o_ref[...] = y.astype(BF16)


def _ssd_gate(yf2d, yb2d, zx2d, norm_w):
    n, d_inner = yf2d.shape
    tm = _pick(n, 256)
    blk = pl.BlockSpec((tm, d_inner), lambda i: (i, 0))
    return pl.pallas_call(
        _gate_kernel,
        grid=(n // tm,),
        in_specs=[blk, blk, blk, pl.BlockSpec((1, d_inner), lambda i: (0, 0))],
        out_specs=blk,
        out_shape=jax.ShapeDtypeStruct((n, d_inner), BF16),
        compiler_params=_params("parallel"),
        name="ssd_gate_norm",
    )(yf2d, yb2d, zx2d, norm_w.reshape(1, d_inner))


FFN_TILE = 256


def _ffn_kernel(idx_ref, x_hbm, wg_ref, wu_ref, wd_ref, o_ref, xa, xb, sem, *, tile, n_steps):
    step = pl.program_id(0) * pl.num_programs(1) + pl.program_id(1)

    def issue(tile_idx, buf, s):
        base = tile_idx * tile
        for r in range(tile):
            tok = idx_ref[base + r]
            pltpu.make_async_copy(x_hbm.at[pl.ds(tok, 1), :], buf.at[pl.ds(r, 1), :], sem.at[s]).start()

    def wait(buf, s):
        pltpu.make_async_copy(x_hbm.at[pl.ds(0, tile), :], buf, sem.at[s]).wait()

    def ffn(buf, half):
        xs = buf[...].astype(BF16)
        hg = jnp.dot(xs, wg_ref[0], preferred_element_type=F32)
        hu = jnp.dot(xs, wu_ref[0], preferred_element_type=F32)
        h = (hg / (1.0 + jnp.exp(-hg)) * hu).astype(BF16)
        o_ref[0, half * tile:(half + 1) * tile, :] = jnp.dot(h, wd_ref[0], preferred_element_type=F32).astype(BF16)

    @pl.when(step == 0)
    def _():
        issue(0, xa, 0)

    wait(xa, 0)
    issue(2 * step + 1, xb, 1)
    ffn(xa, 0)
    wait(xb, 1)
    nxt = jnp.where(step + 1 < n_steps, 2 * step + 2, 0)
    issue(nxt, xa, 0)
    ffn(xb, 1)

    @pl.when(step == n_steps - 1)
    def _():
        wait(xa, 0)


def _moe_ffn(x2d, idx, wg, wu, wd):
    n, d = x2d.shape
    n_exp, _, f = wg.shape
    cap = idx.shape[0] // n_exp
    tile = _pick(cap // 2, FFN_TILE)
    steps_per_exp = cap // (2 * tile)
    kern = functools.partial(_ffn_kernel, tile=tile, n_steps=n_exp * steps_per_exp)
    grid_spec = pltpu.PrefetchScalarGridSpec(
        num_scalar_prefetch=1,
        grid=(n_exp, steps_per_exp),
        in_specs=[pl.BlockSpec(memory_space=pl.ANY),
                  pl.BlockSpec((1, d, f), lambda e, t, ix: (e, 0, 0)),
                  pl.BlockSpec((1, d, f), lambda e, t, ix: (e, 0, 0)),
                  pl.BlockSpec((1, f, d), lambda e, t, ix: (e, 0, 0))],
        out_specs=pl.BlockSpec((1, 2 * tile, d), lambda e, t, ix: (e, t, 0)),
        scratch_shapes=[pltpu.VMEM((tile, d), F32), pltpu.VMEM((tile, d), F32), pltpu.SemaphoreType.DMA((2,))],
    )
    return pl.pallas_call(
        kern,
        grid_spec=grid_spec,
        out_shape=jax.ShapeDtypeStruct((n_exp, cap, d), BF16),
        compiler_params=_params("arbitrary", "arbitrary"),
        name="moe_ffn",
    )(idx, x2d, wg, wu, wd)


def _invert_kernel(st_ref, en_ref, post_ref, acc_ref, *, n_exp, win, cap_tot, tokens):
    i = pl.program_id(0)

    @pl.when(i == 0)
    def _():
        acc_ref[...] = jnp.zeros_like(acc_ref)

    t = tokens
    tok = i * t + lax.broadcasted_iota(jnp.int32, (t, LANES), 0)
    lane = lax.broadcasted_iota(jnp.int32, (t, LANES), 1)
    digits = jnp.where(lane == 0, lax.shift_right_logical(tok, 8), jnp.where(lane == 1, tok & 255, 0))
    digits = digits.astype(F32).astype(BF16)
    post = post_ref[...]
    row = lax.broadcasted_iota(jnp.int32, (win, t), 0)

    def window(e):
        s0 = st_ref[i * n_exp + e]
        return jnp.minimum(lax.shift_left(lax.shift_right_logical(s0, 4), 4), cap_tot - win)

    def place(e, w, onehot):
        r = jnp.dot(onehot.astype(BF16), digits, preferred_element_type=F32)
        w = pl.multiple_of(w, BF16_ROWS)
        acc_ref[pl.ds(w, win), :] += r if e == 0 else pltpu.roll(r, 2 * e, 1)

    for e in range(n_exp):
        w = window(e)
        pe = post[e:e + 1, :]
        place(e, w, jnp.where(pe - w == row, 1.0, 0.0))
        s1 = en_ref[i * n_exp + e]
        n_extra = jnp.maximum(s1 - w - 1, 0) // win

        def extra(k, carry, e=e, w=w, pe=pe):
            lo = w + win * (k + 1)
            wk = jnp.minimum(lo, cap_tot - win)
            place(e, wk, jnp.where((pe - wk == row) & (pe >= lo), 1.0, 0.0))
            return carry

        lax.fori_loop(0, n_extra, extra, 0)


def _ec_invert(pos, starts, ends, cap_tot):
    n, n_exp = pos.shape
    t = COMBINE_TOKENS
    win = COMBINE_WINDOW
    assert 2 * n_exp <= LANES and n < 256 * 256
    kern = functools.partial(_invert_kernel, n_exp=n_exp, win=win, cap_tot=cap_tot, tokens=t)
    grid_spec = pltpu.PrefetchScalarGridSpec(
        num_scalar_prefetch=2,
        grid=(n // t,),
        in_specs=[pl.BlockSpec((n_exp, t), lambda i, st, en: (0, i))],
        out_specs=pl.BlockSpec((cap_tot, LANES), lambda i, st, en: (0, 0)),
    )
    acc = pl.pallas_call(
        kern,
        grid_spec=grid_spec,
        out_shape=jax.ShapeDtypeStruct((cap_tot, LANES), F32),
        compiler_params=_params("arbitrary"),
        name="ec_invert",
    )(starts, ends, pos.T)
    digits = acc[:, :2 * n_exp].astype(jnp.int32).reshape(cap_tot, n_exp, 2)
    return (digits[:, :, 0] * 256 + digits[:, :, 1]).T.reshape(-1)


SELECT_ROW_TILE = 512


def _select_kernel(aff_ref, pos_ref, cnt_ref, *, cap, n_exp):
    a = aff_ref[...]
    r = a.shape[0]
    bits = pltpu.bitcast(a, jnp.int32)

    def fold(v):
        sh = n_exp
        while sh < LANES:
            v = v + pltpu.roll(v, sh, 1)
            sh *= 2
        return v

    def count(mask):
        return fold(jnp.sum(jnp.where(mask, 1.0, 0.0), axis=0, keepdims=True))

    def search(i, thr):
        cand = thr | jnp.left_shift(jnp.int32(1), 30 - i)
        return jnp.where(count(bits >= cand) >= cap, cand, thr)

    thr = lax.fori_loop(0, 31, search, jnp.zeros((1, LANES), jnp.int32))
    above = bits > thr
    tied = bits == thr
    need = cap - count(above)

    li = lax.broadcasted_iota(jnp.int32, (LANES, 2 * LANES), 0)
    ci = lax.broadcasted_iota(jnp.int32, (LANES, 2 * LANES), 1)
    same_exp = (li & (n_exp - 1)) == (ci & (n_exp - 1))
    earlier = (li // n_exp) < ((ci & (LANES - 1)) // n_exp)
    w2 = jnp.where(same_exp & ((ci >= LANES) | earlier), 1.0, 0.0).astype(BF16)
    tr = min(SELECT_ROW_TILE, r)
    rr = lax.broadcasted_iota(jnp.int32, (tr, tr), 0)
    rc = lax.broadcasted_iota(jnp.int32, (tr, tr), 1)
    rows_before = jnp.where(rr > rc, 1.0, 0.0).astype(BF16)

    def prefix(mask):
        lw = jnp.dot(jnp.where(mask, 1.0, 0.0).astype(BF16), w2, preferred_element_type=F32)
        within, row_tot = lw[:, :LANES], lw[:, LANES:]
        carry = jnp.zeros((1, LANES), F32)
        outs = []
        for t in range(r // tr):
            rt = row_tot[t * tr:(t + 1) * tr]
            outs.append(jnp.dot(rows_before, rt.astype(BF16), preferred_element_type=F32) + carry
                        + within[t * tr:(t + 1) * tr])
            carry = carry + jnp.sum(rt, axis=0, keepdims=True)
        return jnp.concatenate(outs, axis=0)

    sel = above | (tied & (prefix(tied) < need))
    cnt = prefix(sel).astype(jnp.int32)
    pos_ref[...] = jnp.where(sel, cnt, -1)
    cnt_ref[...] = cnt


def _ec_select(aff_group, cap):
    n_g, n_exp = aff_group.shape
    assert LANES % n_exp == 0 and (n_exp & (n_exp - 1)) == 0
    r = n_g * n_exp // LANES
    assert r % min(SELECT_ROW_TILE, r) == 0
    shp = jax.ShapeDtypeStruct((r, LANES), jnp.int32)
    pos, cnt = pl.pallas_call(
        functools.partial(_select_kernel, cap=cap, n_exp=n_exp),
        out_shape=[shp, shp],
        compiler_params=pltpu.CompilerParams(vmem_limit_bytes=V7X_VMEM_LIMIT_BYTES),
        name="ec_select",
    )(aff_group.reshape(r, LANES))
    return pos.reshape(n_g, n_exp), cnt.reshape(n_g, n_exp)


COMBINE_TOKENS = 256
COMBINE_WINDOW = 64
BF16_ROWS = 16


def _combine_kernel(st_ref, en_ref, x_ref, pos_ref, aff_ref, g_ref, b_ref, o_hbm, out_ref, buf, sem, xbuf, xsem,
                    acc_ref, *, alpha, n_exp, win, cap_tot, n_tiles):
    i = pl.program_id(0)
    slot = i % 2

    def window(tile, e):
        s0 = st_ref[tile * n_exp + e]
        return jnp.minimum(lax.shift_left(lax.shift_right_logical(s0, 4), 4), cap_tot - win)

    def fetch(tile, sl, e):
        w = pl.multiple_of(window(tile, e), BF16_ROWS)
        return pltpu.make_async_copy(o_hbm.at[e, pl.ds(w, win), :], buf.at[sl, pl.ds(e * win, win), :], sem.at[sl, e])

    @pl.when(i == 0)
    def _():
        for e in range(n_exp):
            fetch(0, 0, e).start()

    @pl.when(i + 1 < n_tiles)
    def _():
        for e in range(n_exp):
            fetch(i + 1, 1 - slot, e).start()

    pos = pos_ref[...]
    aff = aff_ref[...]
    t = pos.shape[0]
    lane = lax.broadcasted_iota(jnp.int32, (t, 2 * win), 1)
    first = lane < win
    lane_in = jnp.where(first, lane, lane - win)
    parts = []
    for e in range(0, n_exp, 2):
        rel = jnp.where(first, pos[:, e:e + 1] - window(i, e), pos[:, e + 1:e + 2] - window(i, e + 1))
        gate = jnp.where(first, aff[:, e:e + 1], aff[:, e + 1:e + 2])
        parts.append(jnp.where(rel == lane_in, gate, 0.0))
    pmat = jnp.concatenate(parts, axis=1)
    p_hi = pmat.astype(BF16)
    p_lo = (pmat - p_hi.astype(F32)).astype(BF16)
    for e in range(n_exp):
        fetch(i, slot, e).wait()
    rows = buf[slot]
    acc_ref[...] = jnp.dot(p_hi, rows, preferred_element_type=F32) + jnp.dot(p_lo, rows, preferred_element_type=F32)

    lane1 = lax.broadcasted_iota(jnp.int32, (t, win), 1)
    for e in range(n_exp):
        w = window(i, e)
        s1 = en_ref[i * n_exp + e]
        n_extra = jnp.maximum(s1 - w - 1, 0) // win

        def extra(k, carry, e=e, w=w):
            lo = w + win * (k + 1)
            wk = pl.multiple_of(jnp.minimum(lo, cap_tot - win), BF16_ROWS)
            cp = pltpu.make_async_copy(o_hbm.at[e, pl.ds(wk, win), :], xbuf, xsem)
            cp.start()
            cp.wait()
            pe = pos[:, e:e + 1]
            oh = jnp.where((pe - wk == lane1) & (pe >= lo), 1.0, 0.0).astype(BF16)
            acc_ref[...] += jnp.dot(oh, xbuf[...], preferred_element_type=F32) * aff[:, e:e + 1]
            return carry

        lax.fori_loop(0, n_extra, extra, 0)
    out_ref[...] = _res_ln(x_ref[...], acc_ref[...], g_ref[...], b_ref[...], alpha)


def _moe_combine_ln(x2d, pos, aff, starts, ends, o, g, b, alpha):
    n, d = x2d.shape
    n_exp, cap_tot, _ = o.shape
    t = COMBINE_TOKENS
    win = COMBINE_WINDOW
    assert n % t == 0 and cap_tot % BF16_ROWS == 0 and cap_tot >= win and n_exp % 2 == 0 and 2 * win == LANES
    n_tiles = n // t
    kern = functools.partial(_combine_kernel, alpha=alpha, n_exp=n_exp, win=win, cap_tot=cap_tot, n_tiles=n_tiles)
    row = pl.BlockSpec((1, d), lambda i, st, en: (0, 0))
    tok = lambda w: pl.BlockSpec((t, w), lambda i, st, en: (i, 0))
    grid_spec = pltpu.PrefetchScalarGridSpec(
        num_scalar_prefetch=2,
        grid=(n_tiles,),
        in_specs=[tok(d), tok(n_exp), tok(n_exp), row, row, pl.BlockSpec(memory_space=pl.ANY)],
        out_specs=tok(d),
        scratch_shapes=[pltpu.VMEM((2, n_exp * win, d), BF16), pltpu.SemaphoreType.DMA((2, n_exp)),
                        pltpu.VMEM((win, d), BF16), pltpu.SemaphoreType.DMA(()), pltpu.VMEM((t, d), F32)],
    )
    return pl.pallas_call(
        kern,
        grid_spec=grid_spec,
        out_shape=jax.ShapeDtypeStruct((n, d), F32),
        compiler_params=_params("arbitrary"),
        name="moe_combine_ln",
    )(starts, ends, x2d, pos, aff, g.reshape(1, d), b.reshape(1, d), o)


def _ec_moe_ln(x2d, aff, groups, wg, wu, wd, g, b, alpha):
    n, d = x2d.shape
    n_exp = aff.shape[1]
    t = COMBINE_TOKENS
    pos_l, st_l, en_l = [], [], []
    off = 0
    for start, cnt_tok in groups:
        assert start % t == 0 and cnt_tok % t == 0
        cap = EC_CAPACITY_FACTOR * cnt_tok // n_exp
        pos, cnt = _ec_select(aff[start:start + cnt_tok], cap)
        st = cnt[::t] + off
        en = jnp.concatenate([st[1:], jnp.full((1, n_exp), off + cap, jnp.int32)], axis=0)
        pos_l.append(jnp.where(pos >= 0, pos + off, -1))
        st_l.append(st)
        en_l.append(en)
        off += cap
    cap_tot = off
    pos = jnp.concatenate(pos_l, axis=0)
    starts = jnp.concatenate(st_l, axis=0).reshape(-1)
    ends = jnp.concatenate(en_l, axis=0).reshape(-1)
    idx = _ec_invert(pos, starts, ends, cap_tot)
    o = _moe_ffn(x2d, idx, wg, wu, wd)
    return _moe_combine_ln(x2d, pos, aff, starts, ends, o, g, b, alpha)


def _split2_bf16(w):
    hi = w.astype(BF16)
    lo = (w - hi.astype(F32)).astype(BF16)
    return jnp.concatenate([hi, lo], axis=1)


def kernel(x_prompt, x_sample, attn_w_qkv, attn_q_norm, attn_k_norm, attn_w_o, pool_w, pool_scale, ssd_w_in,
           ssd_conv_w, ssd_conv_b, ssd_dt_bias, ssd_A_log, ssd_D, ssd_norm, ssd_w_out, moe_w_router, moe_w_gate,
           moe_w_up, moe_w_down, ln_g, ln_b):
    bp, s, d = x_prompt.shape
    bs = x_sample.shape[0]
    assert x_sample.shape[1] == s
    bsz = bp + bs
    n = bsz * s
    groups = [(0, bp * s), (bp * s, bs * s)]
    depth = ln_g.shape[0]
    alpha = (2 * depth) ** 0.25
    n_heads = attn_w_o.shape[1] // HEAD_DIM
    n_kv = (attn_w_qkv.shape[2] // HEAD_DIM - n_heads) // 2
    d_inner = ssd_w_out.shape[1]
    n_ssd_heads = ssd_A_log.shape[-1]
    assert d_inner == n_ssd_heads * SSD_HEAD_DIM
    conv_dim = ssd_conv_w.shape[2]
    n_groups = (conv_dim - d_inner) // (2 * D_STATE)
    rope = _rope_tables(s)

    x = jnp.concatenate([x_prompt, x_sample], axis=0).reshape(n, d)
    ia = ip = isd = 0
    for i in range(depth):
        wr2 = _split2_bf16(moe_w_router[i])
        g1, b1, g2, b2 = ln_g[i, 0], ln_b[i, 0], ln_g[i, 1], ln_b[i, 1]
        kind = i % 3
        if kind == 0:
            qkv = _qkv_proj(x, attn_w_qkv[ia].astype(BF16), attn_q_norm[ia], attn_k_norm[ia], rope, s, n_heads, n_kv)
            qkv3 = qkv.reshape(bsz, s, -1)
            vt = jnp.swapaxes(qkv3[:, :, (n_heads + n_kv) * HEAD_DIM:], 1, 2)
            o = _flash_attention(qkv3, vt, n_heads, n_kv)
            x, aff = _mm_res_ln_router(o.reshape(n, -1), attn_w_o[ia].astype(BF16), x, g1, b1, wr2, alpha)
            ia += 1
        elif kind == 1:
            x3, aff3 = _pool_layer(x.reshape(bsz, s, d), pool_w[ip].astype(BF16), pool_scale[ip], g1, b1, wr2, alpha)
            x, aff = x3.reshape(n, d), aff3.reshape(n, -1)
            ip += 1
        else:
            zx = _matmul_f32(x, ssd_w_in[isd].astype(BF16), 1152)
            zx3 = zx.reshape(bsz, s, -1)
            xbc = _ssd_conv(zx3, ssd_conv_w[isd], ssd_conv_b[isd], d_inner, conv_dim)
            a_neg = -jnp.exp(ssd_A_log[isd].astype(F32)).reshape(-1)
            dt, ecs, tot = _ssd_dt(zx3, ssd_dt_bias[isd].reshape(-1), a_neg, d_inner + conv_dim)
            yf, yb = _ssd_scan(xbc, dt, ecs, tot, ssd_D[isd], d_inner, n_groups)
            yn = _ssd_gate(yf.reshape(n, d_inner), yb.reshape(n, d_inner), zx, ssd_norm[isd])
            x, aff = _mm_res_ln_router(yn, ssd_w_out[isd].astype(BF16), x, g1, b1, wr2, alpha)
            isd += 1
        x = _ec_moe_ln(x, aff, groups, moe_w_gate[i].astype(BF16), moe_w_up[i].astype(BF16),
                       moe_w_down[i].astype(BF16), g2, b2, alpha)
    y = x.reshape(bsz, s, d)
    return y[:bp], y[bp:]
```

```python
import functools
import math

import jax
import jax.numpy as jnp
from jax import lax
from jax.experimental import pallas as pl
from jax.experimental.pallas import tpu as pltpu

F32 = jnp.float32
BF16 = jnp.bfloat16

HEAD_DIM = 128
GRID_W = 64
ROPE_THETA = 10000.0
POOL_WINDOWS = (2, 4, 8, 16)
POOL_HALO = 8
D_STATE = 128
SSD_CHUNK = 128
SSD_HEAD_DIM = 64
SSD_HEADS_PER_GROUP = 8
D_CONV = 4
CONV_LEFT = D_CONV // 2
EC_CAPACITY_FACTOR = 2
LN_EPS = 1e-5
RMS_EPS = 1e-6
LOG2E = 1.4426950408889634

V7X_VMEM_LIMIT_BYTES = 52 * 1024 * 1024
LANES = 128


def _params(*sem):
    return pltpu.CompilerParams(dimension_semantics=sem, vmem_limit_bytes=V7X_VMEM_LIMIT_BYTES)


def _pick(n, pref):
    t = min(n, pref)
    while n % t:
        t //= 2
    return t


def _res_ln(x, h, g, b, alpha):
    y = alpha * x + h
    mu = jnp.mean(y, axis=-1, keepdims=True)
    yc = y - mu
    var = jnp.mean(yc * yc, axis=-1, keepdims=True)
    return yc * lax.rsqrt(var + LN_EPS) * g + b


def _router_affinity(xn, wr_ref, n_exp):
    xh = xn.astype(BF16)
    xl = (xn - xh.astype(F32)).astype(BF16)
    wr = wr_ref[...]
    r1 = jnp.dot(xh, wr, preferred_element_type=F32)
    r2 = jnp.dot(xl, wr[:, :n_exp], preferred_element_type=F32)
    logits = r1[:, :n_exp] + (r1[:, n_exp:] + r2)
    m = jnp.max(logits, axis=-1, keepdims=True)
    e = jnp.exp(logits - m)
    return e / jnp.sum(e, axis=-1, keepdims=True)


def _qkv_kernel(x_ref, w_ref, cos_ref, sa_ref, sb_ref, qn_ref, kn_ref, o_ref, xb_ref, *,
                n_q_tiles, n_k_tiles, heads_per_tile, q_scale):
    j = pl.program_id(1)

    @pl.when(j == 0)
    def _():
        xb_ref[...] = x_ref[...].astype(BF16)

    acc = jnp.dot(xb_ref[...], w_ref[...], preferred_element_type=F32)

    def norm_rope(gain_ref, scale):
        cos = cos_ref[...]
        sa = sa_ref[...]
        sb = sb_ref[...]
        g = gain_ref[...]
        for h in range(heads_per_tile):
            a = acc[:, h * HEAD_DIM:(h + 1) * HEAD_DIM]
            a = a * lax.rsqrt(jnp.mean(a * a, axis=-1, keepdims=True) + RMS_EPS) * g
            r = a * cos + pltpu.roll(a, HEAD_DIM - 32, 1) * sa + pltpu.roll(a, 32, 1) * sb
            o_ref[:, h * HEAD_DIM:(h + 1) * HEAD_DIM] = (r * scale).astype(BF16)

    @pl.when(j < n_q_tiles)
    def _():
        norm_rope(qn_ref, q_scale)

    @pl.when((j >= n_q_tiles) & (j < n_q_tiles + n_k_tiles))
    def _():
        norm_rope(kn_ref, 1.0)

    @pl.when(j >= n_q_tiles + n_k_tiles)
    def _():
        o_ref[...] = acc.astype(BF16)


def _rope_tables(seq_len):
    rows = seq_len // GRID_W
    row = jnp.repeat(jnp.arange(rows, dtype=F32), GRID_W)
    col = jnp.tile(jnp.arange(GRID_W, dtype=F32), rows)
    half = HEAD_DIM // 4
    inv_freq = ROPE_THETA ** (-jnp.arange(0, HEAD_DIM // 2, 2, dtype=F32) / (HEAD_DIM // 2))
    ang_r = row[:, None] * inv_freq
    ang_c = col[:, None] * inv_freq
    zeros = jnp.zeros((seq_len, half), F32)
    cos = jnp.concatenate([jnp.cos(ang_r)] * 2 + [jnp.cos(ang_c)] * 2, axis=-1)
    sin_a = jnp.concatenate([-jnp.sin(ang_r), zeros, -jnp.sin(ang_c), zeros], axis=-1)
    sin_b = jnp.concatenate([zeros, jnp.sin(ang_r), zeros, jnp.sin(ang_c)], axis=-1)
    return cos, sin_a, sin_b


def _qkv_proj(x2d, w_bf16, q_norm, k_norm, rope, seq_len, n_heads, n_kv):
    n, d = x2d.shape
    qkv_dim = w_bf16.shape[1]
    tn = n_kv * HEAD_DIM
    tm = _pick(seq_len, 512)
    cos, sa, sb = rope
    nsb = seq_len // tm
    kern = functools.partial(
        _qkv_kernel, n_q_tiles=n_heads // n_kv, n_k_tiles=1, heads_per_tile=n_kv,
        q_scale=HEAD_DIM ** -0.5 * LOG2E)
    tab = pl.BlockSpec((tm, HEAD_DIM), lambda i, j: (i % nsb, 0))
    vec = pl.BlockSpec((1, HEAD_DIM), lambda i, j: (0, 0))
    return pl.pallas_call(
        kern,
        grid=(n // tm, qkv_dim // tn),
        in_specs=[pl.BlockSpec((tm, d), lambda i, j: (i, 0)),
                  pl.BlockSpec((d, tn), lambda i, j: (0, j)),
                  tab, tab, tab, vec, vec],
        out_specs=pl.BlockSpec((tm, tn), lambda i, j: (i, j)),
        out_shape=jax.ShapeDtypeStruct((n, qkv_dim), BF16),
        scratch_shapes=[pltpu.VMEM((tm, d), BF16)],
        compiler_params=_params("parallel", "arbitrary"),
        name="qkv_proj",
    )(x2d, w_bf16, cos, sa, sb, q_norm.reshape(1, HEAD_DIM), k_norm.reshape(1, HEAD_DIM))


FLASH_TQ = 128
FLASH_TK = 512


def _flash_kernel(q_ref, k_ref, vt_ref, o_ref, s_scr, p_scr, acc_scr, *, tk, group):
    tq = q_ref.shape[1]
    seq = k_ref.shape[1]
    q = jnp.concatenate([q_ref[0, :, g * HEAD_DIM:(g + 1) * HEAD_DIM] for g in range(group)], axis=0)
    rows = group * tq
    nc = seq // tk

    def scores(c, slot):
        k = k_ref[0, pl.ds(c * tk, tk), :]
        s_scr[slot] = lax.dot_general(k, q, (((1,), (1,)), ((), ())), preferred_element_type=F32)

    def pv(c, slot, alpha):
        vt = vt_ref[0, 0, :, pl.ds(c * tk, tk)]
        acc_scr[...] = acc_scr[...] * alpha + jnp.dot(vt, p_scr[slot], preferred_element_type=F32)

    def softmax(slot, m, l):
        s = s_scr[slot]
        m_new = jnp.maximum(m, jnp.max(s, axis=0, keepdims=True))
        alpha = jnp.exp2(m - m_new)
        p = jnp.exp2(s - m_new)
        l = alpha * l + jnp.sum(p, axis=0, keepdims=True)
        p_scr[slot] = p.astype(BF16)
        return m_new, l, alpha

    m = jnp.full((1, rows), -jnp.inf, F32)
    l = jnp.zeros((1, rows), F32)
    acc_scr[...] = jnp.zeros_like(acc_scr)
    scores(0, 0)
    if nc > 1:
        scores(1, 1)
    m, l, alpha = softmax(0, m, l)
    for c in range(1, nc):
        if c + 1 < nc:
            scores(c + 1, (c + 1) % 2)
        pv(c - 1, (c - 1) % 2, alpha)
        m, l, alpha = softmax(c % 2, m, l)
    pv(nc - 1, (nc - 1) % 2, alpha)
    o = (acc_scr[...] / l).T
    for g in range(group):
        o_ref[0, :, g * HEAD_DIM:(g + 1) * HEAD_DIM] = o[g * tq:(g + 1) * tq].astype(BF16)


def _flash_attention(qkv, vt, n_heads, n_kv):
    b, s, _ = qkv.shape
    group = n_heads // n_kv
    tq = _pick(s, FLASH_TQ)
    tk = _pick(s, FLASH_TK)
    gw = group * HEAD_DIM
    rows = group * tq
    vrows = vt.shape[2]
    kern = functools.partial(_flash_kernel, tk=tk, group=group)
    return pl.pallas_call(
        kern,
        grid=(b, n_kv, s // tq),
        in_specs=[pl.BlockSpec((1, tq, gw), lambda bi, h, i: (bi, i, h)),
                  pl.BlockSpec((1, s, HEAD_DIM), lambda bi, h, i: (bi, 0, n_heads + h)),
                  pl.BlockSpec((1, 1, vrows, s), lambda bi, h, i: (bi, h, 0, 0))],
        out_specs=pl.BlockSpec((1, tq, gw), lambda bi, h, i: (bi, i, h)),
        out_shape=jax.ShapeDtypeStruct((b, s, n_heads * HEAD_DIM), BF16),
        scratch_shapes=[pltpu.VMEM((2, tk, rows), F32), pltpu.VMEM((2, tk, rows), BF16),
                        pltpu.VMEM((vrows, rows), F32)],
        compiler_params=_params("parallel", "parallel", "arbitrary"),
        name="flash_attention",
    )(qkv, qkv, vt)


def _mm_ln_kernel(a_ref, w_ref, x_ref, g_ref, b_ref, wr_ref, o_ref, aff_ref, acc_ref, *, alpha, nk, n_exp):
    k = pl.program_id(1)

    @pl.when(k == 0)
    def _():
        acc_ref[...] = jnp.zeros_like(acc_ref)

    acc_ref[...] += jnp.dot(a_ref[...], w_ref[...], preferred_element_type=F32)

    @pl.when(k == nk - 1)
    def _():
        xn = _res_ln(x_ref[...], acc_ref[...], g_ref[...], b_ref[...], alpha)
        o_ref[...] = xn
        aff_ref[...] = _router_affinity(xn, wr_ref, n_exp)


def _mm_res_ln_router(a_bf16, w_bf16, x2d, g, b, wr2, alpha):
    n, kdim = a_bf16.shape
    d = w_bf16.shape[1]
    n_exp = wr2.shape[1] // 2
    tm = _pick(n, 512)
    tk = _pick(kdim, 512)
    nk = kdim // tk
    kern = functools.partial(_mm_ln_kernel, alpha=alpha, nk=nk, n_exp=n_exp)
    row = pl.BlockSpec((1, d), lambda i, k: (0, 0))
    return pl.pallas_call(
        kern,
        grid=(n // tm, nk),
        in_specs=[pl.BlockSpec((tm, tk), lambda i, k: (i, k)),
                  pl.BlockSpec((tk, d), lambda i, k: (k, 0)),
                  pl.BlockSpec((tm, d), lambda i, k: (i, 0)),
                  row, row,
                  pl.BlockSpec((d, 2 * n_exp), lambda i, k: (0, 0))],
        out_specs=[pl.BlockSpec((tm, d), lambda i, k: (i, 0)),
                   pl.BlockSpec((tm, n_exp), lambda i, k: (i, 0))],
        out_shape=[jax.ShapeDtypeStruct((n, d), F32), jax.ShapeDtypeStruct((n, n_exp), F32)],
        scratch_shapes=[pltpu.VMEM((tm, d), F32)],
        compiler_params=_params("parallel", "arbitrary"),
        name="mm_res_ln_router",
    )(a_bf16, w_bf16, x2d, g.reshape(1, d), b.reshape(1, d), wr2)


def _pool_kernel(prev_ref, cur_ref, next_ref, w_ref, sc_ref, g_ref, b_ref, wr_ref, o_ref, aff_ref, ext_ref, *,
                 alpha, nt, seq_len, n_exp):
    i = pl.program_id(1)
    t = cur_ref.shape[1]
    d = cur_ref.shape[2]
    pg = d // len(POOL_WINDOWS)
    x = cur_ref[0]
    ext_ref[0:POOL_HALO, :] = jnp.where(i == 0, 0.0, prev_ref[0])
    ext_ref[POOL_HALO:POOL_HALO + t, :] = x
    ext_ref[POOL_HALO + t:2 * POOL_HALO + t, :] = jnp.where(i == nt - 1, 0.0, next_ref[0])
    pos = i * t + lax.broadcasted_iota(jnp.int32, (t, 1), 0)
    hs = []
    for gi, w in enumerate(POOL_WINDOWS):
        half = w // 2
        cols = slice(gi * pg, (gi + 1) * pg)
        acc = ext_ref[pl.ds(POOL_HALO - half, t), cols]
        for jj in range(1, w):
            acc = acc + ext_ref[pl.ds(POOL_HALO - half + jj, t), cols]
        cnt = (jnp.minimum(pos + half, seq_len) - jnp.maximum(pos - half, 0)).astype(F32)
        mixed = (acc / cnt - x[:, cols]).astype(BF16)
        hs.append(jnp.dot(mixed, w_ref[gi], preferred_element_type=F32))
    h = jnp.concatenate(hs, axis=-1) * sc_ref[...]
    xn = _res_ln(x, h, g_ref[...], b_ref[...], alpha)
    o_ref[0] = xn
    aff_ref[0] = _router_affinity(xn, wr_ref, n_exp)


def _pool_layer(x3d, w_bf16, scale, g, b, wr2, alpha):
    bsz, s, d = x3d.shape
    n_exp = wr2.shape[1] // 2
    t = _pick(s, 256)
    nt = s // t
    hb = t // POOL_HALO
    last_hb = s // POOL_HALO - 1
    pg = d // len(POOL_WINDOWS)
    kern = functools.partial(_pool_kernel, alpha=alpha, nt=nt, seq_len=s, n_exp=n_exp)
    row = pl.BlockSpec((1, d), lambda bi, i: (0, 0))
    return pl.pallas_call(
        kern,
        grid=(bsz, nt),
        in_specs=[pl.BlockSpec((1, POOL_HALO, d), lambda bi, i: (bi, jnp.maximum(i * hb - 1, 0), 0)),
                  pl.BlockSpec((1, t, d), lambda bi, i: (bi, i, 0)),
                  pl.BlockSpec((1, POOL_HALO, d), lambda bi, i: (bi, jnp.minimum((i + 1) * hb, last_hb), 0)),
                  pl.BlockSpec((len(POOL_WINDOWS), pg, pg), lambda bi, i: (0, 0, 0)),
                  row, row, row,
                  pl.BlockSpec((d, 2 * n_exp), lambda bi, i: (0, 0))],
        out_specs=[pl.BlockSpec((1, t, d), lambda bi, i: (bi, i, 0)),
                   pl.BlockSpec((1, t, n_exp), lambda bi, i: (bi, i, 0))],
        out_shape=[jax.ShapeDtypeStruct((bsz, s, d), F32), jax.ShapeDtypeStruct((bsz, s, n_exp), F32)],
        scratch_shapes=[pltpu.VMEM((t + 2 * POOL_HALO, d), F32)],
        compiler_params=_params("parallel", "parallel"),
        name="pool_mixer",
    )(x3d, x3d, x3d, w_bf16, scale.reshape(1, d), g.reshape(1, d), b.reshape(1, d), wr2)


def _mm_kernel(x_ref, w_ref, o_ref, xb_ref):
    @pl.when(pl.program_id(1) == 0)
    def _():
        xb_ref[...] = x_ref[...].astype(BF16)

    o_ref[...] = jnp.dot(xb_ref[...], w_ref[...], preferred_element_type=F32)


def _matmul_f32(x2d, w_bf16, tn_pref):
    n, d = x2d.shape
    nout = w_bf16.shape[1]
    tm = _pick(n, 1024)
    tn = tn_pref
    assert nout % tn == 0
    return pl.pallas_call(
        _mm_kernel,
        grid=(n // tm, nout // tn),
        in_specs=[pl.BlockSpec((tm, d), lambda i, j: (i, 0)),
                  pl.BlockSpec((d, tn), lambda i, j: (0, j))],
        out_specs=pl.BlockSpec((tm, tn), lambda i, j: (i, j)),
        out_shape=jax.ShapeDtypeStruct((n, nout), F32),
        scratch_shapes=[pltpu.VMEM((tm, d), BF16)],
        compiler_params=_params("parallel", "arbitrary"),
        name="ssd_in_proj",
    )(x2d, w_bf16)


def _conv_kernel(prev_ref, cur_ref, next_ref, w_ref, b_ref, o_ref, ext_ref, *, nt):
    i = pl.program_id(1)
    t = cur_ref.shape[1]
    ext_ref[0:POOL_HALO, :] = jnp.where(i == 0, 0.0, prev_ref[0])
    ext_ref[POOL_HALO:POOL_HALO + t, :] = cur_ref[0]
    ext_ref[POOL_HALO + t:2 * POOL_HALO + t, :] = jnp.where(i == nt - 1, 0.0, next_ref[0])
    acc = ext_ref[pl.ds(POOL_HALO - CONV_LEFT, t), :] * w_ref[0:1, :]
    for kk in range(1, D_CONV):
        acc = acc + ext_ref[pl.ds(POOL_HALO - CONV_LEFT + kk, t), :] * w_ref[kk:kk + 1, :]
    acc = acc + b_ref[...]
    o_ref[0] = acc / (1.0 + jnp.exp(-acc))


def _ssd_conv(zx3d, conv_w, conv_b, d_inner, conv_dim):
    bsz, s, _ = zx3d.shape
    tc = 512
    t = _pick(s, 512)
    nt = s // t
    hb = t // POOL_HALO
    last_hb = s // POOL_HALO - 1
    c0 = d_inner // tc
    return pl.pallas_call(
        functools.partial(_conv_kernel, nt=nt),
        grid=(bsz, nt, conv_dim // tc),
        in_specs=[pl.BlockSpec((1, POOL_HALO, tc), lambda bi, i, j: (bi, jnp.maximum(i * hb - 1, 0), c0 + j)),
                  pl.BlockSpec((1, t, tc), lambda bi, i, j: (bi, i, c0 + j)),
                  pl.BlockSpec((1, POOL_HALO, tc), lambda bi, i, j: (bi, jnp.minimum((i + 1) * hb, last_hb), c0 + j)),
                  pl.BlockSpec((D_CONV, tc), lambda bi, i, j: (0, j)),
                  pl.BlockSpec((1, tc), lambda bi, i, j: (0, j))],
        out_specs=pl.BlockSpec((1, t, tc), lambda bi, i, j: (bi, i, j)),
        out_shape=jax.ShapeDtypeStruct((bsz, s, conv_dim), F32),
        scratch_shapes=[pltpu.VMEM((t + 2 * POOL_HALO, tc), F32)],
        compiler_params=_params("parallel", "parallel", "parallel"),
        name="ssd_conv",
    )(zx3d, zx3d, zx3d, conv_w, conv_b.reshape(1, conv_dim))


def _split3(x):
    hi = x.astype(BF16)
    r1 = x - hi.astype(F32)
    mid = r1.astype(BF16)
    lo = (r1 - mid.astype(F32)).astype(BF16)
    return hi, mid, lo


def _dt_kernel(raw_ref, bias_ref, a_ref, dt_ref, e_ref, tot_ref):
    v = raw_ref[0] + bias_ref[...]
    dt = jnp.maximum(v, 0.0) + jnp.log1p(jnp.exp(-jnp.abs(v)))
    dt_ref[0] = dt
    a = dt * a_ref[...]
    q, w = a.shape
    li = lax.broadcasted_iota(jnp.int32, (q, q), 0)
    si = lax.broadcasted_iota(jnp.int32, (q, q), 1)
    tri = jnp.where(li >= si, 1.0, 0.0).astype(BF16)
    hi, mid, lo = _split3(a)
    cs = (jnp.dot(tri, lo, preferred_element_type=F32) + jnp.dot(tri, mid, preferred_element_type=F32)
          + jnp.dot(tri, hi, preferred_element_type=F32))
    lane = lax.broadcasted_iota(jnp.int32, (q, w), 1)
    e_ref[0] = jnp.where(lane < w // 2, cs, cs - a)
    tot_ref[0, 0] = cs[q - 1:q, :]


def _ssd_dt(zx3d, dt_bias, a_neg, col0):
    bsz, s, _ = zx3d.shape
    w = dt_bias.shape[-1]
    assert w == LANES and col0 % LANES == 0
    nc = s // SSD_CHUNK
    blk = pl.BlockSpec((1, SSD_CHUNK, w), lambda bi, c: (bi, c, 0))
    row = pl.BlockSpec((1, w), lambda bi, c: (0, 0))
    return pl.pallas_call(
        _dt_kernel,
        grid=(bsz, nc),
        in_specs=[pl.BlockSpec((1, SSD_CHUNK, w), lambda bi, c: (bi, c, col0 // LANES)), row, row],
        out_specs=[blk, blk, pl.BlockSpec((1, 1, 1, w), lambda bi, c: (bi, c, 0, 0))],
        out_shape=[jax.ShapeDtypeStruct((bsz, s, w), F32), jax.ShapeDtypeStruct((bsz, s, w), F32),
                   jax.ShapeDtypeStruct((bsz, nc, 1, w), F32)],
        compiler_params=_params("parallel", "parallel"),
        name="ssd_dt",
    )(zx3d, dt_bias.reshape(1, w), a_neg.reshape(1, w))


def _expand_heads(v, width):
    m = v.shape[0]
    lane = lax.broadcasted_iota(jnp.int32, (m, LANES), 1)
    parts = []
    for pr in range(width // LANES):
        parts.append(jnp.where(lane < SSD_HEAD_DIM, v[:, 2 * pr:2 * pr + 1], v[:, 2 * pr + 1:2 * pr + 2]))
    return jnp.concatenate(parts, axis=1)


def _ssd_direction(x, bmat, cmat, dt_col, dt_row, e_col, e_row, tot, st_ref, forward):
    q, width = x.shape
    li = lax.broadcasted_iota(jnp.int32, (q, q), 0)
    si = lax.broadcasted_iota(jnp.int32, (q, q), 1)
    lane = lax.broadcasted_iota(jnp.int32, (q, LANES), 1)
    if forward:
        mask = li >= si
        out_dec = jnp.exp(e_col)
        st_w = dt_col * jnp.exp(tot - e_col)
    else:
        mask = si >= li
        out_dec = jnp.exp(tot - e_col)
        st_w = dt_col * jnp.exp(e_col)
    cb = lax.dot_general(cmat.astype(BF16), bmat.astype(BF16), (((1,), (1,)), ((), ())),
                         preferred_element_type=F32)
    xb = x.astype(BF16)
    y_parts = []
    for pr in range(width // LANES):
        ms = []
        for r in (2 * pr, 2 * pr + 1):
            if forward:
                diff = e_col[:, r:r + 1] - e_row[r:r + 1, :]
            else:
                diff = e_row[r:r + 1, :] - e_col[:, r:r + 1]
            decay = jnp.exp(jnp.where(mask, diff, -jnp.inf))
            ms.append((decay * cb * dt_row[r:r + 1, :]).astype(BF16))
        xp = xb[:, pr * LANES:(pr + 1) * LANES]
        zero = jnp.zeros_like(xp)
        rhs = jnp.concatenate([jnp.where(lane < SSD_HEAD_DIM, xp, zero),
                               jnp.where(lane >= SSD_HEAD_DIM, xp, zero)], axis=0)
        y_parts.append(jnp.dot(jnp.concatenate(ms, axis=1), rhs, preferred_element_type=F32))
    y = jnp.concatenate(y_parts, axis=1)
    st = st_ref[...]
    y = y + jnp.dot(cmat.astype(BF16), st.astype(BF16), preferred_element_type=F32) * _expand_heads(out_dec, width)
    xd = (x * _expand_heads(st_w, width)).astype(BF16)
    st_new = lax.dot_general(bmat.astype(BF16), xd, (((0,), (0,)), ((), ())), preferred_element_type=F32)
    chunk_dec = _expand_heads(jnp.broadcast_to(jnp.exp(tot), (8, tot.shape[1])), width)[0:1, :]
    st_ref[...] = st * chunk_dec + st_new
    return y


def _ssd_scan_kernel(xf_ref, bf_ref, cf_ref, dcf_ref, drf_ref, ecf_ref, erf_ref, tf_ref,
                     xr_ref, br_ref, cr_ref, dcr_ref, drr_ref, ecr_ref, err_ref, tr_ref,
                     dskip_ref, yf_ref, yb_ref, stf_ref, stb_ref):
    @pl.when(pl.program_id(2) == 0)
    def _():
        stf_ref[...] = jnp.zeros_like(stf_ref)
        stb_ref[...] = jnp.zeros_like(stb_ref)

    xf = xf_ref[0]
    yf = _ssd_direction(xf, bf_ref[0], cf_ref[0], dcf_ref[0, 0, 0], drf_ref[0, 0, 0], ecf_ref[0, 0, 0],
                        erf_ref[0, 0, 0], tf_ref[0, 0, 0, 0], stf_ref, True)
    yf_ref[0] = yf + dskip_ref[...] * xf
    yb_ref[0] = _ssd_direction(xr_ref[0], br_ref[0], cr_ref[0], dcr_ref[0, 0, 0], drr_ref[0, 0, 0], ecr_ref[0, 0, 0],
                               err_ref[0, 0, 0], tr_ref[0, 0, 0, 0], stb_ref, False)


def _ssd_scan(xbc, dt, ecs, tot, d_skip, d_inner, n_groups):
    bsz, s, _ = xbc.shape
    nc = s // SSD_CHUNK
    hg = SSD_HEADS_PER_GROUP
    gw = hg * SSD_HEAD_DIM
    assert gw % LANES == 0 and d_inner == n_groups * gw
    b0 = d_inner // D_STATE
    c0 = b0 + n_groups

    def col_layout(a):
        return a.reshape(bsz, s, 2, n_groups, hg).transpose(0, 2, 3, 1, 4)

    def row_layout(a):
        return a.reshape(bsz, s, 2, n_groups, hg).transpose(0, 2, 3, 4, 1)

    dt_c, dt_r, e_c, e_r = col_layout(dt), row_layout(dt), col_layout(ecs), row_layout(ecs)
    tot6 = tot.reshape(bsz, nc, 2, n_groups, 1, hg)
    dskip = jnp.repeat(d_skip.astype(F32), SSD_HEAD_DIM).reshape(1, d_inner)

    def specs(direction, cidx):
        return [
            pl.BlockSpec((1, SSD_CHUNK, gw), lambda b, g, c: (b, cidx(c), g)),
            pl.BlockSpec((1, SSD_CHUNK, D_STATE), lambda b, g, c: (b, cidx(c), b0 + g)),
            pl.BlockSpec((1, SSD_CHUNK, D_STATE), lambda b, g, c: (b, cidx(c), c0 + g)),
            pl.BlockSpec((1, 1, 1, SSD_CHUNK, hg), lambda b, g, c: (b, direction, g, cidx(c), 0)),
            pl.BlockSpec((1, 1, 1, hg, SSD_CHUNK), lambda b, g, c: (b, direction, g, 0, cidx(c))),
            pl.BlockSpec((1, 1, 1, SSD_CHUNK, hg), lambda b, g, c: (b, direction, g, cidx(c), 0)),
            pl.BlockSpec((1, 1, 1, hg, SSD_CHUNK), lambda b, g, c: (b, direction, g, 0, cidx(c))),
            pl.BlockSpec((1, 1, 1, 1, 1, hg), lambda b, g, c: (b, cidx(c), direction, g, 0, 0)),
        ]

    fwd = lambda c: c
    bwd = lambda c: nc - 1 - c
    y_shape = jax.ShapeDtypeStruct((bsz, s, d_inner), F32)
    return pl.pallas_call(
        _ssd_scan_kernel,
        grid=(bsz, n_groups, nc),
        in_specs=specs(0, fwd) + specs(1, bwd) + [pl.BlockSpec((1, gw), lambda b, g, c: (0, g))],
        out_specs=[pl.BlockSpec((1, SSD_CHUNK, gw), lambda b, g, c: (b, c, g)),
                   pl.BlockSpec((1, SSD_CHUNK, gw), lambda b, g, c: (b, nc - 1 - c, g))],
        out_shape=[y_shape, y_shape],
        scratch_shapes=[pltpu.VMEM((D_STATE, gw), F32), pltpu.VMEM((D_STATE, gw), F32)],
        compiler_params=_params("parallel", "parallel", "arbitrary"),
        name="ssd_scan",
    )(xbc, xbc, xbc, dt_c, dt_r, e_c, e_r, tot6,
      xbc, xbc, xbc, dt_c, dt_r, e_c, e_r, tot6, dskip)


def _gate_kernel(yf_ref, yb_ref, z_ref, nw_ref, o_ref):
    z = z_ref[...]
    y = (yf_ref[...] + yb_ref[...]) * (z / (1.0 + jnp.exp(-z)))
    y = y * lax.rsqrt(jnp.mean(y * y, axis=-1, keepdims=True) + RMS_EPS) * nw_ref[...]
    o_ref[...] = y.astype(BF16)


def _ssd_gate(yf2d, yb2d, zx2d, norm_w):
    n, d_inner = yf2d.shape
    tm = _pick(n, 256)
    blk = pl.BlockSpec((tm, d_inner), lambda i: (i, 0))
    return pl.pallas_call(
        _gate_kernel,
        grid=(n // tm,),
        in_specs=[blk, blk, blk, pl.BlockSpec((1, d_inner), lambda i: (0, 0))],
        out_specs=blk,
        out_shape=jax.ShapeDtypeStruct((n, d_inner), BF16),
        compiler_params=_params("parallel"),
        name="ssd_gate_norm",
    )(yf2d, yb2d, zx2d, norm_w.reshape(1, d_inner))


FFN_TILE = 256


def _ffn_kernel(idx_ref, x_hbm, wg_ref, wu_ref, wd_ref, o_ref, xa, xb, sem, *, tile, n_steps):
    step = pl.program_id(0) * pl.num_programs(1) + pl.program_id(1)

    def issue(tile_idx, buf, s):
        base = tile_idx * tile
        for r in range(tile):
            tok = idx_ref[base + r]
            pltpu.make_async_copy(x_hbm.at[pl.ds(tok, 1), :], buf.at[pl.ds(r, 1), :], sem.at[s]).start(priority=r % 2)

    def wait(buf, s):
        pltpu.make_async_copy(x_hbm.at[pl.ds(0, tile), :], buf, sem.at[s]).wait()

    def ffn(buf, half):
        xs = buf[...].astype(BF16)
        hg = jnp.dot(xs, wg_ref[0], preferred_element_type=F32)
        hu = jnp.dot(xs, wu_ref[0], preferred_element_type=F32)
        h = (hg / (1.0 + jnp.exp(-hg)) * hu).astype(BF16)
        o_ref[0, half * tile:(half + 1) * tile, :] = jnp.dot(h, wd_ref[0], preferred_element_type=F32).astype(BF16)

    @pl.when(step == 0)
    def _():
        issue(0, xa, 0)

    wait(xa, 0)
    issue(2 * step + 1, xb, 1)
    ffn(xa, 0)
    nxt = jnp.where(step + 1 < n_steps, 2 * step + 2, 0)
    issue(nxt, xa, 0)
    wait(xb, 1)
    ffn(xb, 1)

    @pl.when(step == n_steps - 1)
    def _():
        wait(xa, 0)


def _moe_ffn(x2d, idx, wg, wu, wd):
    n, d = x2d.shape
    n_exp, _, f = wg.shape
    cap = idx.shape[0] // n_exp
    tile = _pick(cap // 2, FFN_TILE)
    steps_per_exp = cap // (2 * tile)
    kern = functools.partial(_ffn_kernel, tile=tile, n_steps=n_exp * steps_per_exp)
    grid_spec = pltpu.PrefetchScalarGridSpec(
        num_scalar_prefetch=1,
        grid=(n_exp, steps_per_exp),
        in_specs=[pl.BlockSpec(memory_space=pl.ANY),
                  pl.BlockSpec((1, d, f), lambda e, t, ix: (e, 0, 0)),
                  pl.BlockSpec((1, d, f), lambda e, t, ix: (e, 0, 0)),
                  pl.BlockSpec((1, f, d), lambda e, t, ix: (e, 0, 0))],
        out_specs=pl.BlockSpec((1, 2 * tile, d), lambda e, t, ix: (e, t, 0)),
        scratch_shapes=[pltpu.VMEM((tile, d), F32), pltpu.VMEM((tile, d), F32), pltpu.SemaphoreType.DMA((2,))],
    )
    return pl.pallas_call(
        kern,
        grid_spec=grid_spec,
        out_shape=jax.ShapeDtypeStruct((n_exp, cap, d), BF16),
        compiler_params=_params("arbitrary", "arbitrary"),
        name="moe_ffn",
    )(idx, x2d, wg, wu, wd)


def _invert_kernel(st_ref, en_ref, post_ref, acc_ref, *, n_exp, win, cap_tot, tokens):
    i = pl.program_id(0)

    @pl.when(i == 0)
    def _():
        acc_ref[...] = jnp.zeros_like(acc_ref)

    t = tokens
    tok = i * t + lax.broadcasted_iota(jnp.int32, (t, LANES), 0)
    lane = lax.broadcasted_iota(jnp.int32, (t, LANES), 1)
    digits = jnp.where(lane == 0, lax.shift_right_logical(tok, 8), jnp.where(lane == 1, tok & 255, 0))
    digits = digits.astype(F32).astype(BF16)
    post = post_ref[...]
    row = lax.broadcasted_iota(jnp.int32, (win, t), 0)

    def window(e):
        s0 = st_ref[i * n_exp + e]
        return jnp.minimum(lax.shift_left(lax.shift_right_logical(s0, 4), 4), cap_tot - win)

    def place(e, w, r):
        w = pl.multiple_of(w, BF16_ROWS)
        acc_ref[pl.ds(w, win), :] += r if e == 0 else pltpu.roll(r, 2 * e, 1)

    ws = [window(e) for e in range(n_exp)]
    onehots = jnp.concatenate([jnp.where(post[e:e + 1, :] - ws[e] == row, 1.0, 0.0).astype(BF16)
                               for e in range(n_exp)], axis=0)
    res = jnp.dot(onehots, digits, preferred_element_type=F32)
    for e in range(n_exp):
        place(e, ws[e], res[e * win:(e + 1) * win])

    for e in range(n_exp):
        w = ws[e]
        pe = post[e:e + 1, :]
        s1 = en_ref[i * n_exp + e]
        n_extra = jnp.maximum(s1 - w - 1, 0) // win

        def extra(k, carry, e=e, w=w, pe=pe):
            lo = w + win * (k + 1)
            wk = jnp.minimum(lo, cap_tot - win)
            oh = jnp.where((pe - wk == row) & (pe >= lo), 1.0, 0.0).astype(BF16)
            place(e, wk, jnp.dot(oh, digits, preferred_element_type=F32))
            return carry

        lax.fori_loop(0, n_extra, extra, 0)


def _ec_invert(pos, starts, ends, cap_tot):
    n, n_exp = pos.shape
    t = COMBINE_TOKENS
    win = COMBINE_WINDOW
    assert 2 * n_exp <= LANES and n < 256 * 256
    kern = functools.partial(_invert_kernel, n_exp=n_exp, win=win, cap_tot=cap_tot, tokens=t)
    grid_spec = pltpu.PrefetchScalarGridSpec(
        num_scalar_prefetch=2,
        grid=(n // t,),
        in_specs=[pl.BlockSpec((n_exp, t), lambda i, st, en: (0, i))],
        out_specs=pl.BlockSpec((cap_tot, LANES), lambda i, st, en: (0, 0)),
    )
    acc = pl.pallas_call(
        kern,
        grid_spec=grid_spec,
        out_shape=jax.ShapeDtypeStruct((cap_tot, LANES), F32),
        compiler_params=_params("arbitrary"),
        name="ec_invert",
    )(starts, ends, pos.T)
    digits = acc[:, :2 * n_exp].astype(jnp.int32).reshape(cap_tot, n_exp, 2)
    return (digits[:, :, 0] * 256 + digits[:, :, 1]).T.reshape(-1)


SELECT_ROW_TILE = 512


def _select_kernel(aff_ref, pos_ref, cnt_ref, *, cap, n_exp):
    a = aff_ref[...]
    r = a.shape[0]
    bits = pltpu.bitcast(a, jnp.int32)

    def fold(v):
        sh = n_exp
        while sh < LANES:
            v = v + pltpu.roll(v, sh, 1)
            sh *= 2
        return v

    def count(mask):
        return fold(jnp.sum(jnp.where(mask, 1.0, 0.0), axis=0, keepdims=True))

    def search(i, thr):
        cand = thr | jnp.left_shift(jnp.int32(1), 30 - i)
        return jnp.where(count(bits >= cand) >= cap, cand, thr)

    thr = lax.fori_loop(0, 31, search, jnp.zeros((1, LANES), jnp.int32))
    above = bits > thr
    tied = bits == thr
    need = cap - count(above)

    li = lax.broadcasted_iota(jnp.int32, (LANES, 2 * LANES), 0)
    ci = lax.broadcasted_iota(jnp.int32, (LANES, 2 * LANES), 1)
    same_exp = (li & (n_exp - 1)) == (ci & (n_exp - 1))
    earlier = (li // n_exp) < ((ci & (LANES - 1)) // n_exp)
    w2 = jnp.where(same_exp & ((ci >= LANES) | earlier), 1.0, 0.0).astype(BF16)
    tr = min(SELECT_ROW_TILE, r)
    rr = lax.broadcasted_iota(jnp.int32, (tr, tr), 0)
    rc = lax.broadcasted_iota(jnp.int32, (tr, tr), 1)
    rows_before = jnp.where(rr > rc, 1.0, 0.0).astype(BF16)

    def prefix(mask):
        lw = jnp.dot(jnp.where(mask, 1.0, 0.0).astype(BF16), w2, preferred_element_type=F32)
        within, row_tot = lw[:, :LANES], lw[:, LANES:]
        carry = jnp.zeros((1, LANES), F32)
        outs = []
        for t in range(r // tr):
            rt = row_tot[t * tr:(t + 1) * tr]
            outs.append(jnp.dot(rows_before, rt.astype(BF16), preferred_element_type=F32) + carry
                        + within[t * tr:(t + 1) * tr])
            carry = carry + jnp.sum(rt, axis=0, keepdims=True)
        return jnp.concatenate(outs, axis=0)

    sel = above | (tied & (prefix(tied) < need))
    cnt = prefix(sel).astype(jnp.int32)
    pos_ref[...] = jnp.where(sel, cnt, -1)
    cnt_ref[...] = cnt


def _ec_select(aff_group, cap):
    n_g, n_exp = aff_group.shape
    assert LANES % n_exp == 0 and (n_exp & (n_exp - 1)) == 0
    r = n_g * n_exp // LANES
    assert r % min(SELECT_ROW_TILE, r) == 0
    shp = jax.ShapeDtypeStruct((r, LANES), jnp.int32)
    pos, cnt = pl.pallas_call(
        functools.partial(_select_kernel, cap=cap, n_exp=n_exp),
        out_shape=[shp, shp],
        compiler_params=pltpu.CompilerParams(vmem_limit_bytes=V7X_VMEM_LIMIT_BYTES),
        name="ec_select",
    )(aff_group.reshape(r, LANES))
    return pos.reshape(n_g, n_exp), cnt.reshape(n_g, n_exp)


COMBINE_TOKENS = 256
COMBINE_WINDOW = 64
BF16_ROWS = 16


def _combine_kernel(st_ref, en_ref, x_ref, pos_ref, aff_ref, g_ref, b_ref, o_hbm, out_ref, buf, sem, xbuf, xsem,
                    acc_ref, *, alpha, n_exp, win, cap_tot, n_tiles):
    i = pl.program_id(0)
    slot = i % 2

    def window(tile, e):
        s0 = st_ref[tile * n_exp + e]
        return jnp.minimum(lax.shift_left(lax.shift_right_logical(s0, 4), 4), cap_tot - win)

    def fetch(tile, sl, e):
        w = pl.multiple_of(window(tile, e), BF16_ROWS)
        return pltpu.make_async_copy(o_hbm.at[e, pl.ds(w, win), :], buf.at[sl, pl.ds(e * win, win), :], sem.at[sl, e])

    @pl.when(i == 0)
    def _():
        for e in range(n_exp):
            fetch(0, 0, e).start()

    @pl.when(i + 1 < n_tiles)
    def _():
        for e in range(n_exp):
            fetch(i + 1, 1 - slot, e).start()

    pos = pos_ref[...]
    aff = aff_ref[...]
    t = pos.shape[0]
    lane = lax.broadcasted_iota(jnp.int32, (t, 2 * win), 1)
    first = lane < win
    lane_in = jnp.where(first, lane, lane - win)
    parts = []
    for e in range(0, n_exp, 2):
        rel = jnp.where(first, pos[:, e:e + 1] - window(i, e), pos[:, e + 1:e + 2] - window(i, e + 1))
        gate = jnp.where(first, aff[:, e:e + 1], aff[:, e + 1:e + 2])
        parts.append(jnp.where(rel == lane_in, gate, 0.0))
    pmat = jnp.concatenate(parts, axis=1)
    p_hi = pmat.astype(BF16)
    p_lo = (pmat - p_hi.astype(F32)).astype(BF16)
    for e in range(n_exp):
        fetch(i, slot, e).wait()
    rows = buf[slot]
    acc_ref[...] = jnp.dot(p_hi, rows, preferred_element_type=F32) + jnp.dot(p_lo, rows, preferred_element_type=F32)

    lane1 = lax.broadcasted_iota(jnp.int32, (t, win), 1)
    for e in range(n_exp):
        w = window(i, e)
        s1 = en_ref[i * n_exp + e]
        n_extra = jnp.maximum(s1 - w - 1, 0) // win

        def extra(k, carry, e=e, w=w):
            lo = w + win * (k + 1)
            wk = pl.multiple_of(jnp.minimum(lo, cap_tot - win), BF16_ROWS)
            cp = pltpu.make_async_copy(o_hbm.at[e, pl.ds(wk, win), :], xbuf, xsem)
            cp.start()
            cp.wait()
            pe = pos[:, e:e + 1]
            oh = jnp.where((pe - wk == lane1) & (pe >= lo), 1.0, 0.0).astype(BF16)
            acc_ref[...] += jnp.dot(oh, xbuf[...], preferred_element_type=F32) * aff[:, e:e + 1]
            return carry

        lax.fori_loop(0, n_extra, extra, 0)
    out_ref[...] = _res_ln(x_ref[...], acc_ref[...], g_ref[...], b_ref[...], alpha)


def _moe_combine_ln(x2d, pos, aff, starts, ends, o, g, b, alpha):
    n, d = x2d.shape
    n_exp, cap_tot, _ = o.shape
    t = COMBINE_TOKENS
    win = COMBINE_WINDOW
    assert n % t == 0 and cap_tot % BF16_ROWS == 0 and cap_tot >= win and n_exp % 2 == 0 and 2 * win == LANES
    n_tiles = n // t
    kern = functools.partial(_combine_kernel, alpha=alpha, n_exp=n_exp, win=win, cap_tot=cap_tot, n_tiles=n_tiles)
    row = pl.BlockSpec((1, d), lambda i, st, en: (0, 0))
    tok = lambda w: pl.BlockSpec((t, w), lambda i, st, en: (i, 0))
    grid_spec = pltpu.PrefetchScalarGridSpec(
        num_scalar_prefetch=2,
        grid=(n_tiles,),
        in_specs=[tok(d), tok(n_exp), tok(n_exp), row, row, pl.BlockSpec(memory_space=pl.ANY)],
        out_specs=tok(d),
        scratch_shapes=[pltpu.VMEM((2, n_exp * win, d), BF16), pltpu.SemaphoreType.DMA((2, n_exp)),
                        pltpu.VMEM((win, d), BF16), pltpu.SemaphoreType.DMA(()), pltpu.VMEM((t, d), F32)],
    )
    return pl.pallas_call(
        kern,
        grid_spec=grid_spec,
        out_shape=jax.ShapeDtypeStruct((n, d), F32),
        compiler_params=_params("arbitrary"),
        name="moe_combine_ln",
    )(starts, ends, x2d, pos, aff, g.reshape(1, d), b.reshape(1, d), o)


def _ec_moe_ln(x2d, aff, groups, wg, wu, wd, g, b, alpha):
    n, d = x2d.shape
    n_exp = aff.shape[1]
    t = COMBINE_TOKENS
    pos_l, st_l, en_l = [], [], []
    off = 0
    for start, cnt_tok in groups:
        assert start % t == 0 and cnt_tok % t == 0
        cap = EC_CAPACITY_FACTOR * cnt_tok // n_exp
        pos, cnt = _ec_select(aff[start:start + cnt_tok], cap)
        st = cnt[::t] + off
        en = jnp.concatenate([st[1:], jnp.full((1, n_exp), off + cap, jnp.int32)], axis=0)
        pos_l.append(jnp.where(pos >= 0, pos + off, -1))
        st_l.append(st)
        en_l.append(en)
        off += cap
    cap_tot = off
    pos = jnp.concatenate(pos_l, axis=0)
    starts = jnp.concatenate(st_l, axis=0).reshape(-1)
    ends = jnp.concatenate(en_l, axis=0).reshape(-1)
    idx = _ec_invert(pos, starts, ends, cap_tot)
    o = _moe_ffn(x2d, idx, wg, wu, wd)
    return _moe_combine_ln(x2d, pos, aff, starts, ends, o, g, b, alpha)


def _split2_bf16(w):
    hi = w.astype(BF16)
    lo = (w - hi.astype(F32)).astype(BF16)
    return jnp.concatenate([hi, lo], axis=1)


def kernel(x_prompt, x_sample, attn_w_qkv, attn_q_norm, attn_k_norm, attn_w_o, pool_w, pool_scale, ssd_w_in,
           ssd_conv_w, ssd_conv_b, ssd_dt_bias, ssd_A_log, ssd_D, ssd_norm, ssd_w_out, moe_w_router, moe_w_gate,
           moe_w_up, moe_w_down, ln_g, ln_b):
    bp, s, d = x_prompt.shape
    bs = x_sample.shape[0]
    assert x_sample.shape[1] == s
    bsz = bp + bs
    n = bsz * s
    groups = [(0, bp * s), (bp * s, bs * s)]
    depth = ln_g.shape[0]
    alpha = (2 * depth) ** 0.25
    n_heads = attn_w_o.shape[1] // HEAD_DIM
    n_kv = (attn_w_qkv.shape[2] // HEAD_DIM - n_heads) // 2
    d_inner = ssd_w_out.shape[1]
    n_ssd_heads = ssd_A_log.shape[-1]
    assert d_inner == n_ssd_heads * SSD_HEAD_DIM
    conv_dim = ssd_conv_w.shape[2]
    n_groups = (conv_dim - d_inner) // (2 * D_STATE)
    rope = _rope_tables(s)

    x = jnp.concatenate([x_prompt, x_sample], axis=0).reshape(n, d)
    ia = ip = isd = 0
    for i in range(depth):
        wr2 = _split2_bf16(moe_w_router[i])
        g1, b1, g2, b2 = ln_g[i, 0], ln_b[i, 0], ln_g[i, 1], ln_b[i, 1]
        kind = i % 3
        if kind == 0:
            qkv = _qkv_proj(x, attn_w_qkv[ia].astype(BF16), attn_q_norm[ia], attn_k_norm[ia], rope, s, n_heads, n_kv)
            qkv3 = qkv.reshape(bsz, s, -1)
            vt = jnp.swapaxes(qkv3[:, :, (n_heads + n_kv) * HEAD_DIM:], 1, 2).reshape(bsz, n_kv, HEAD_DIM, s)
            o = _flash_attention(qkv3, vt, n_heads, n_kv)
            x, aff = _mm_res_ln_router(o.reshape(n, -1), attn_w_o[ia].astype(BF16), x, g1, b1, wr2, alpha)
            ia += 1
        elif kind == 1:
            x3, aff3 = _pool_layer(x.reshape(bsz, s, d), pool_w[ip].astype(BF16), pool_scale[ip], g1, b1, wr2, alpha)
            x, aff = x3.reshape(n, d), aff3.reshape(n, -1)
            ip += 1
        else:
            zx = _matmul_f32(x, ssd_w_in[isd].astype(BF16), 1152)
            zx3 = zx.reshape(bsz, s, -1)
            xbc = _ssd_conv(zx3, ssd_conv_w[isd], ssd_conv_b[isd], d_inner, conv_dim)
            a_neg = -jnp.exp(ssd_A_log[isd].astype(F32)).reshape(-1)
            dt, ecs, tot = _ssd_dt(zx3, ssd_dt_bias[isd].reshape(-1), a_neg, d_inner + conv_dim)
            yf, yb = _ssd_scan(xbc, dt, ecs, tot, ssd_D[isd], d_inner, n_groups)
            yn = _ssd_gate(yf.reshape(n, d_inner), yb.reshape(n, d_inner), zx, ssd_norm[isd])
            x, aff = _mm_res_ln_router(yn, ssd_w_out[isd].astype(BF16), x, g1, b1, wr2, alpha)
            isd += 1
        x = _ec_moe_ln(x, aff, groups, moe_w_gate[i].astype(BF16), moe_w_up[i].astype(BF16),
                       moe_w_down[i].astype(BF16), g2, b2, alpha)
    y = x.reshape(bsz, s, d)
    return y[:bp], y[bp:]
```

```python
import functools
import math

import jax
import jax.numpy as jnp
from jax import lax
from jax.experimental import pallas as pl
from jax.experimental.pallas import tpu as pltpu

F32 = jnp.float32
BF16 = jnp.bfloat16

HEAD_DIM = 128
GRID_W = 64
ROPE_THETA = 10000.0
POOL_WINDOWS = (2, 4, 8, 16)
POOL_HALO = 8
D_STATE = 128
SSD_CHUNK = 128
SSD_HEAD_DIM = 64
SSD_HEADS_PER_GROUP = 8
D_CONV = 4
CONV_LEFT = D_CONV // 2
EC_CAPACITY_FACTOR = 2
LN_EPS = 1e-5
RMS_EPS = 1e-6
LOG2E = 1.4426950408889634

V7X_VMEM_LIMIT_BYTES = 52 * 1024 * 1024
LANES = 128


def _params(*sem):
    return pltpu.CompilerParams(dimension_semantics=sem, vmem_limit_bytes=V7X_VMEM_LIMIT_BYTES)


def _pick(n, pref):
    t = min(n, pref)
    while n % t:
        t //= 2
    return t


def _res_ln(x, h, g, b, alpha):
    y = alpha * x + h
    mu = jnp.mean(y, axis=-1, keepdims=True)
    yc = y - mu
    var = jnp.mean(yc * yc, axis=-1, keepdims=True)
    return yc * lax.rsqrt(var + LN_EPS) * g + b


def _router_affinity(xn, wr_ref, n_exp):
    xh = xn.astype(BF16)
    xl = (xn - xh.astype(F32)).astype(BF16)
    wr = wr_ref[...]
    r1 = jnp.dot(xh, wr, preferred_element_type=F32)
    r2 = jnp.dot(xl, wr[:, :n_exp], preferred_element_type=F32)
    logits = r1[:, :n_exp] + (r1[:, n_exp:] + r2)
    m = jnp.max(logits, axis=-1, keepdims=True)
    e = jnp.exp(logits - m)
    return e / jnp.sum(e, axis=-1, keepdims=True)


def _qkv_kernel(x_ref, w_ref, cos_ref, sa_ref, sb_ref, qn_ref, kn_ref, o_ref, xb_ref, *,
                n_q_tiles, n_k_tiles, heads_per_tile, q_scale):
    j = pl.program_id(1)

    @pl.when(j == 0)
    def _():
        xb_ref[...] = x_ref[...].astype(BF16)

    acc = jnp.dot(xb_ref[...], w_ref[...], preferred_element_type=F32)

    def norm_rope(gain_ref, scale):
        cos = cos_ref[...]
        sa = sa_ref[...]
        sb = sb_ref[...]
        g = gain_ref[...]
        for h in range(heads_per_tile):
            a = acc[:, h * HEAD_DIM:(h + 1) * HEAD_DIM]
            a = a * lax.rsqrt(jnp.mean(a * a, axis=-1, keepdims=True) + RMS_EPS) * g
            r = a * cos + pltpu.roll(a, HEAD_DIM - 32, 1) * sa + pltpu.roll(a, 32, 1) * sb
            o_ref[:, h * HEAD_DIM:(h + 1) * HEAD_DIM] = (r * scale).astype(BF16)

    @pl.when(j < n_q_tiles)
    def _():
        norm_rope(qn_ref, q_scale)

    @pl.when((j >= n_q_tiles) & (j < n_q_tiles + n_k_tiles))
    def _():
        norm_rope(kn_ref, 1.0)

    @pl.when(j >= n_q_tiles + n_k_tiles)
    def _():
        o_ref[...] = acc.astype(BF16)


def _rope_tables(seq_len):
    rows = seq_len // GRID_W
    row = jnp.repeat(jnp.arange(rows, dtype=F32), GRID_W)
    col = jnp.tile(jnp.arange(GRID_W, dtype=F32), rows)
    half = HEAD_DIM // 4
    inv_freq = ROPE_THETA ** (-jnp.arange(0, HEAD_DIM // 2, 2, dtype=F32) / (HEAD_DIM // 2))
    ang_r = row[:, None] * inv_freq
    ang_c = col[:, None] * inv_freq
    zeros = jnp.zeros((seq_len, half), F32)
    cos = jnp.concatenate([jnp.cos(ang_r)] * 2 + [jnp.cos(ang_c)] * 2, axis=-1)
    sin_a = jnp.concatenate([-jnp.sin(ang_r), zeros, -jnp.sin(ang_c), zeros], axis=-1)
    sin_b = jnp.concatenate([zeros, jnp.sin(ang_r), zeros, jnp.sin(ang_c)], axis=-1)
    return cos, sin_a, sin_b


def _qkv_proj(x2d, w_bf16, q_norm, k_norm, rope, seq_len, n_heads, n_kv):
    n, d = x2d.shape
    qkv_dim = w_bf16.shape[1]
    tn = n_kv * HEAD_DIM
    tm = _pick(seq_len, 512)
    cos, sa, sb = rope
    nsb = seq_len // tm
    kern = functools.partial(
        _qkv_kernel, n_q_tiles=n_heads // n_kv, n_k_tiles=1, heads_per_tile=n_kv,
        q_scale=HEAD_DIM ** -0.5 * LOG2E)
    tab = pl.BlockSpec((tm, HEAD_DIM), lambda i, j: (i % nsb, 0))
    vec = pl.BlockSpec((1, HEAD_DIM), lambda i, j: (0, 0))
    return pl.pallas_call(
        kern,
        grid=(n // tm, qkv_dim // tn),
        in_specs=[pl.BlockSpec((tm, d), lambda i, j: (i, 0)),
                  pl.BlockSpec((d, tn), lambda i, j: (0, j)),
                  tab, tab, tab, vec, vec],
        out_specs=pl.BlockSpec((tm, tn), lambda i, j: (i, j)),
        out_shape=jax.ShapeDtypeStruct((n, qkv_dim), BF16),
        scratch_shapes=[pltpu.VMEM((tm, d), BF16)],
        compiler_params=_params("parallel", "arbitrary"),
        name="qkv_proj",
    )(x2d, w_bf16, cos, sa, sb, q_norm.reshape(1, HEAD_DIM), k_norm.reshape(1, HEAD_DIM))


FLASH_TQ = 128
FLASH_TK = 512


def _flash_kernel(q_ref, k_ref, vt_ref, o_ref, s_scr, p_scr, acc_scr, *, tk, group):
    tq = q_ref.shape[1]
    seq = k_ref.shape[1]
    q = jnp.concatenate([q_ref[0, :, g * HEAD_DIM:(g + 1) * HEAD_DIM] for g in range(group)], axis=0)
    rows = group * tq
    nc = seq // tk

    def scores(c, slot):
        k = k_ref[0, pl.ds(c * tk, tk), :]
        s_scr[slot] = lax.dot_general(k, q, (((1,), (1,)), ((), ())), preferred_element_type=F32)

    def pv(c, slot, alpha):
        vt = vt_ref[0, 0, :, pl.ds(c * tk, tk)]
        acc_scr[...] = acc_scr[...] * alpha + jnp.dot(vt, p_scr[slot], preferred_element_type=F32)

    def softmax(slot, m, l):
        s = s_scr[slot]
        m_new = jnp.maximum(m, jnp.max(s, axis=0, keepdims=True))
        alpha = jnp.exp2(m - m_new)
        p = jnp.exp2(s - m_new)
        l = alpha * l + jnp.sum(p, axis=0, keepdims=True)
        p_scr[slot] = p.astype(BF16)
        return m_new, l, alpha

    m = jnp.full((1, rows), -jnp.inf, F32)
    l = jnp.zeros((1, rows), F32)
    acc_scr[...] = jnp.zeros_like(acc_scr)
    scores(0, 0)
    if nc > 1:
        scores(1, 1)
    m, l, alpha = softmax(0, m, l)
    for c in range(1, nc):
        if c + 1 < nc:
            scores(c + 1, (c + 1) % 2)
        pv(c - 1, (c - 1) % 2, alpha)
        m, l, alpha = softmax(c % 2, m, l)
    pv(nc - 1, (nc - 1) % 2, alpha)
    o = (acc_scr[...] / l).T
    for g in range(group):
        o_ref[0, :, g * HEAD_DIM:(g + 1) * HEAD_DIM] = o[g * tq:(g + 1) * tq].astype(BF16)


def _flash_attention(qkv, vt, n_heads, n_kv):
    b, s, _ = qkv.shape
    group = n_heads // n_kv
    tq = _pick(s, FLASH_TQ)
    tk = _pick(s, FLASH_TK)
    gw = group * HEAD_DIM
    rows = group * tq
    vrows = vt.shape[2]
    kern = functools.partial(_flash_kernel, tk=tk, group=group)
    return pl.pallas_call(
        kern,
        grid=(b, n_kv, s // tq),
        in_specs=[pl.BlockSpec((1, tq, gw), lambda bi, h, i: (bi, i, h)),
                  pl.BlockSpec((1, s, HEAD_DIM), lambda bi, h, i: (bi, 0, n_heads + h)),
                  pl.BlockSpec((1, 1, vrows, s), lambda bi, h, i: (bi, h, 0, 0))],
        out_specs=pl.BlockSpec((1, tq, gw), lambda bi, h, i: (bi, i, h)),
        out_shape=jax.ShapeDtypeStruct((b, s, n_heads * HEAD_DIM), BF16),
        scratch_shapes=[pltpu.VMEM((2, tk, rows), F32), pltpu.VMEM((2, tk, rows), BF16),
                        pltpu.VMEM((vrows, rows), F32)],
        compiler_params=_params("parallel", "parallel", "arbitrary"),
        name="flash_attention",
    )(qkv, qkv, vt)


MM_LN_TM = 512
MM_LN_TK = 2048


def _mm_ln_kernel(a_ref, w_ref, x_ref, g_ref, b_ref, wr_ref, o_ref, aff_ref, acc_ref, *, alpha, nk, n_exp):
    k = pl.program_id(1)
    part = jnp.dot(a_ref[...], w_ref[...], preferred_element_type=F32)

    def finish(h):
        xn = _res_ln(x_ref[...], h, g_ref[...], b_ref[...], alpha)
        o_ref[...] = xn
        aff_ref[...] = _router_affinity(xn, wr_ref, n_exp)

    if nk == 1:
        finish(part)
        return

    @pl.when(k == 0)
    def _():
        acc_ref[...] = part

    @pl.when((k > 0) & (k < nk - 1))
    def _():
        acc_ref[...] += part

    @pl.when(k == nk - 1)
    def _():
        finish(acc_ref[...] + part)


def _mm_res_ln_router(a_bf16, w_bf16, x2d, g, b, wr2, alpha):
    n, kdim = a_bf16.shape
    d = w_bf16.shape[1]
    n_exp = wr2.shape[1] // 2
    tm = _pick(n, MM_LN_TM)
    tk = _pick(kdim, MM_LN_TK)
    nk = kdim // tk
    kern = functools.partial(_mm_ln_kernel, alpha=alpha, nk=nk, n_exp=n_exp)
    row = pl.BlockSpec((1, d), lambda i, k: (0, 0))
    return pl.pallas_call(
        kern,
        grid=(n // tm, nk),
        in_specs=[pl.BlockSpec((tm, tk), lambda i, k: (i, k)),
                  pl.BlockSpec((tk, d), lambda i, k: (k, 0)),
                  pl.BlockSpec((tm, d), lambda i, k: (i, 0)),
                  row, row,
                  pl.BlockSpec((d, 2 * n_exp), lambda i, k: (0, 0))],
        out_specs=[pl.BlockSpec((tm, d), lambda i, k: (i, 0)),
                   pl.BlockSpec((tm, n_exp), lambda i, k: (i, 0))],
        out_shape=[jax.ShapeDtypeStruct((n, d), F32), jax.ShapeDtypeStruct((n, n_exp), F32)],
        scratch_shapes=[pltpu.VMEM((tm, d), F32)],
        compiler_params=_params("parallel", "arbitrary"),
        name="mm_res_ln_router",
    )(a_bf16, w_bf16, x2d, g.reshape(1, d), b.reshape(1, d), wr2)


def _pool_kernel(prev_ref, cur_ref, next_ref, w_ref, sc_ref, g_ref, b_ref, wr_ref, o_ref, aff_ref, ext_ref, *,
                 alpha, nt, seq_len, n_exp):
    i = pl.program_id(1)
    t = cur_ref.shape[1]
    d = cur_ref.shape[2]
    pg = d // len(POOL_WINDOWS)
    x = cur_ref[0]
    ext_ref[0:POOL_HALO, :] = jnp.where(i == 0, 0.0, prev_ref[0])
    ext_ref[POOL_HALO:POOL_HALO + t, :] = x
    ext_ref[POOL_HALO + t:2 * POOL_HALO + t, :] = jnp.where(i == nt - 1, 0.0, next_ref[0])
    pos = i * t + lax.broadcasted_iota(jnp.int32, (t, 1), 0)
    hs = []
    for gi, w in enumerate(POOL_WINDOWS):
        half = w // 2
        cols = slice(gi * pg, (gi + 1) * pg)
        acc = ext_ref[pl.ds(POOL_HALO - half, t), cols]
        for jj in range(1, w):
            acc = acc + ext_ref[pl.ds(POOL_HALO - half + jj, t), cols]
        cnt = (jnp.minimum(pos + half, seq_len) - jnp.maximum(pos - half, 0)).astype(F32)
        mixed = (acc / cnt - x[:, cols]).astype(BF16)
        hs.append(jnp.dot(mixed, w_ref[gi], preferred_element_type=F32))
    h = jnp.concatenate(hs, axis=-1) * sc_ref[...]
    xn = _res_ln(x, h, g_ref[...], b_ref[...], alpha)
    o_ref[0] = xn
    aff_ref[0] = _router_affinity(xn, wr_ref, n_exp)


def _pool_layer(x3d, w_bf16, scale, g, b, wr2, alpha):
    bsz, s, d = x3d.shape
    n_exp = wr2.shape[1] // 2
    t = _pick(s, 256)
    nt = s // t
    hb = t // POOL_HALO
    last_hb = s // POOL_HALO - 1
    pg = d // len(POOL_WINDOWS)
    kern = functools.partial(_pool_kernel, alpha=alpha, nt=nt, seq_len=s, n_exp=n_exp)
    row = pl.BlockSpec((1, d), lambda bi, i: (0, 0))
    return pl.pallas_call(
        kern,
        grid=(bsz, nt),
        in_specs=[pl.BlockSpec((1, POOL_HALO, d), lambda bi, i: (bi, jnp.maximum(i * hb - 1, 0), 0)),
                  pl.BlockSpec((1, t, d), lambda bi, i: (bi, i, 0)),
                  pl.BlockSpec((1, POOL_HALO, d), lambda bi, i: (bi, jnp.minimum((i + 1) * hb, last_hb), 0)),
                  pl.BlockSpec((len(POOL_WINDOWS), pg, pg), lambda bi, i: (0, 0, 0)),
                  row, row, row,
                  pl.BlockSpec((d, 2 * n_exp), lambda bi, i: (0, 0))],
        out_specs=[pl.BlockSpec((1, t, d), lambda bi, i: (bi, i, 0)),
                   pl.BlockSpec((1, t, n_exp), lambda bi, i: (bi, i, 0))],
        out_shape=[jax.ShapeDtypeStruct((bsz, s, d), F32), jax.ShapeDtypeStruct((bsz, s, n_exp), F32)],
        scratch_shapes=[pltpu.VMEM((t + 2 * POOL_HALO, d), F32)],
        compiler_params=_params("parallel", "parallel"),
        name="pool_mixer",
    )(x3d, x3d, x3d, w_bf16, scale.reshape(1, d), g.reshape(1, d), b.reshape(1, d), wr2)


def _mm_kernel(x_ref, w_ref, o_ref, xb_ref):
    @pl.when(pl.program_id(1) == 0)
    def _():
        xb_ref[...] = x_ref[...].astype(BF16)

    o_ref[...] = jnp.dot(xb_ref[...], w_ref[...], preferred_element_type=F32)


def _matmul_f32(x2d, w_bf16, tn_pref):
    n, d = x2d.shape
    nout = w_bf16.shape[1]
    tm = _pick(n, 1024)
    tn = tn_pref
    assert nout % tn == 0
    return pl.pallas_call(
        _mm_kernel,
        grid=(n // tm, nout // tn),
        in_specs=[pl.BlockSpec((tm, d), lambda i, j: (i, 0)),
                  pl.BlockSpec((d, tn), lambda i, j: (0, j))],
        out_specs=pl.BlockSpec((tm, tn), lambda i, j: (i, j)),
        out_shape=jax.ShapeDtypeStruct((n, nout), F32),
        scratch_shapes=[pltpu.VMEM((tm, d), BF16)],
        compiler_params=_params("parallel", "arbitrary"),
        name="ssd_in_proj",
    )(x2d, w_bf16)


def _conv_kernel(prev_ref, cur_ref, next_ref, w_ref, b_ref, o_ref, ext_ref, *, nt):
    i = pl.program_id(1)
    t = cur_ref.shape[1]
    ext_ref[0:POOL_HALO, :] = jnp.where(i == 0, 0.0, prev_ref[0])
    ext_ref[POOL_HALO:POOL_HALO + t, :] = cur_ref[0]
    ext_ref[POOL_HALO + t:2 * POOL_HALO + t, :] = jnp.where(i == nt - 1, 0.0, next_ref[0])
    acc = ext_ref[pl.ds(POOL_HALO - CONV_LEFT, t), :] * w_ref[0:1, :]
    for kk in range(1, D_CONV):
        acc = acc + ext_ref[pl.ds(POOL_HALO - CONV_LEFT + kk, t), :] * w_ref[kk:kk + 1, :]
    acc = acc + b_ref[...]
    o_ref[0] = acc / (1.0 + jnp.exp(-acc))


def _ssd_conv(zx3d, conv_w, conv_b, d_inner, conv_dim):
    bsz, s, _ = zx3d.shape
    tc = 512
    t = _pick(s, 512)
    nt = s // t
    hb = t // POOL_HALO
    last_hb = s // POOL_HALO - 1
    c0 = d_inner // tc
    return pl.pallas_call(
        functools.partial(_conv_kernel, nt=nt),
        grid=(bsz, nt, conv_dim // tc),
        in_specs=[pl.BlockSpec((1, POOL_HALO, tc), lambda bi, i, j: (bi, jnp.maximum(i * hb - 1, 0), c0 + j)),
                  pl.BlockSpec((1, t, tc), lambda bi, i, j: (bi, i, c0 + j)),
                  pl.BlockSpec((1, POOL_HALO, tc), lambda bi, i, j: (bi, jnp.minimum((i + 1) * hb, last_hb), c0 + j)),
                  pl.BlockSpec((D_CONV, tc), lambda bi, i, j: (0, j)),
                  pl.BlockSpec((1, tc), lambda bi, i, j: (0, j))],
        out_specs=pl.BlockSpec((1, t, tc), lambda bi, i, j: (bi, i, j)),
        out_shape=jax.ShapeDtypeStruct((bsz, s, conv_dim), F32),
        scratch_shapes=[pltpu.VMEM((t + 2 * POOL_HALO, tc), F32)],
        compiler_params=_params("parallel", "parallel", "parallel"),
        name="ssd_conv",
    )(zx3d, zx3d, zx3d, conv_w, conv_b.reshape(1, conv_dim))


def _split3(x):
    hi = x.astype(BF16)
    r1 = x - hi.astype(F32)
    mid = r1.astype(BF16)
    lo = (r1 - mid.astype(F32)).astype(BF16)
    return hi, mid, lo


def _dt_kernel(raw_ref, bias_ref, a_ref, dt_ref, e_ref, tot_ref):
    v = raw_ref[0] + bias_ref[...]
    dt = jnp.maximum(v, 0.0) + jnp.log1p(jnp.exp(-jnp.abs(v)))
    dt_ref[0] = dt
    a = dt * a_ref[...]
    q, w = a.shape
    li = lax.broadcasted_iota(jnp.int32, (q, q), 0)
    si = lax.broadcasted_iota(jnp.int32, (q, q), 1)
    tri = jnp.where(li >= si, 1.0, 0.0).astype(BF16)
    hi, mid, lo = _split3(a)
    cs = (jnp.dot(tri, lo, preferred_element_type=F32) + jnp.dot(tri, mid, preferred_element_type=F32)
          + jnp.dot(tri, hi, preferred_element_type=F32))
    lane = lax.broadcasted_iota(jnp.int32, (q, w), 1)
    e_ref[0] = jnp.where(lane < w // 2, cs, cs - a)
    tot_ref[0, 0] = cs[q - 1:q, :]


def _ssd_dt(zx3d, dt_bias, a_neg, col0):
    bsz, s, _ = zx3d.shape
    w = dt_bias.shape[-1]
    assert w == LANES and col0 % LANES == 0
    nc = s // SSD_CHUNK
    blk = pl.BlockSpec((1, SSD_CHUNK, w), lambda bi, c: (bi, c, 0))
    row = pl.BlockSpec((1, w), lambda bi, c: (0, 0))
    return pl.pallas_call(
        _dt_kernel,
        grid=(bsz, nc),
        in_specs=[pl.BlockSpec((1, SSD_CHUNK, w), lambda bi, c: (bi, c, col0 // LANES)), row, row],
        out_specs=[blk, blk, pl.BlockSpec((1, 1, 1, w), lambda bi, c: (bi, c, 0, 0))],
        out_shape=[jax.ShapeDtypeStruct((bsz, s, w), F32), jax.ShapeDtypeStruct((bsz, s, w), F32),
                   jax.ShapeDtypeStruct((bsz, nc, 1, w), F32)],
        compiler_params=_params("parallel", "parallel"),
        name="ssd_dt",
    )(zx3d, dt_bias.reshape(1, w), a_neg.reshape(1, w))


def _expand_heads(v, width):
    m = v.shape[0]
    lane = lax.broadcasted_iota(jnp.int32, (m, LANES), 1)
    parts = []
    for pr in range(width // LANES):
        parts.append(jnp.where(lane < SSD_HEAD_DIM, v[:, 2 * pr:2 * pr + 1], v[:, 2 * pr + 1:2 * pr + 2]))
    return jnp.concatenate(parts, axis=1)


def _ssd_direction(x, bmat, cmat, dt_col, dt_row, e_col, e_row, tot, st_ref, forward):
    q, width = x.shape
    li = lax.broadcasted_iota(jnp.int32, (q, q), 0)
    si = lax.broadcasted_iota(jnp.int32, (q, q), 1)
    lane = lax.broadcasted_iota(jnp.int32, (q, LANES), 1)
    if forward:
        mask = li >= si
        out_dec = jnp.exp(e_col)
        st_w = dt_col * jnp.exp(tot - e_col)
    else:
        mask = si >= li
        out_dec = jnp.exp(tot - e_col)
        st_w = dt_col * jnp.exp(e_col)
    cb = lax.dot_general(cmat.astype(BF16), bmat.astype(BF16), (((1,), (1,)), ((), ())),
                         preferred_element_type=F32)
    xb = x.astype(BF16)
    y_parts = []
    for pr in range(width // LANES):
        ms = []
        for r in (2 * pr, 2 * pr + 1):
            if forward:
                diff = e_col[:, r:r + 1] - e_row[r:r + 1, :]
            else:
                diff = e_row[r:r + 1, :] - e_col[:, r:r + 1]
            decay = jnp.exp(jnp.where(mask, diff, -jnp.inf))
            ms.append((decay * cb * dt_row[r:r + 1, :]).astype(BF16))
        xp = xb[:, pr * LANES:(pr + 1) * LANES]
        zero = jnp.zeros_like(xp)
        rhs = jnp.concatenate([jnp.where(lane < SSD_HEAD_DIM, xp, zero),
                               jnp.where(lane >= SSD_HEAD_DIM, xp, zero)], axis=0)
        y_parts.append(jnp.dot(jnp.concatenate(ms, axis=1), rhs, preferred_element_type=F32))
    y = jnp.concatenate(y_parts, axis=1)
    st = st_ref[...]
    y = y + jnp.dot(cmat.astype(BF16), st.astype(BF16), preferred_element_type=F32) * _expand_heads(out_dec, width)
    xd = (x * _expand_heads(st_w, width)).astype(BF16)
    st_new = lax.dot_general(bmat.astype(BF16), xd, (((0,), (0,)), ((), ())), preferred_element_type=F32)
    chunk_dec = _expand_heads(jnp.broadcast_to(jnp.exp(tot), (8, tot.shape[1])), width)[0:1, :]
    st_ref[...] = st * chunk_dec + st_new
    return y


SCAN_GROUPS_PER_STEP = 8


def _ssd_scan_kernel(xf_ref, bf_ref, cf_ref, dcf_ref, drf_ref, ecf_ref, erf_ref, tf_ref,
                     xr_ref, br_ref, cr_ref, dcr_ref, drr_ref, ecr_ref, err_ref, tr_ref,
                     dskip_ref, yf_ref, yb_ref, stf_ref, stb_ref, *, gps, gw):
    @pl.when(pl.program_id(2) == 0)
    def _():
        stf_ref[...] = jnp.zeros_like(stf_ref)
        stb_ref[...] = jnp.zeros_like(stb_ref)

    for gi in range(gps):
        xs = slice(gi * gw, (gi + 1) * gw)
        ns = slice(gi * D_STATE, (gi + 1) * D_STATE)
        xf = xf_ref[0, :, xs]
        yf = _ssd_direction(xf, bf_ref[0, :, ns], cf_ref[0, :, ns], dcf_ref[0, 0, gi], drf_ref[0, 0, gi],
                            ecf_ref[0, 0, gi], erf_ref[0, 0, gi], tf_ref[0, 0, 0, gi], stf_ref.at[gi], True)
        yf_ref[0, :, xs] = yf + dskip_ref[:, xs] * xf
        yb_ref[0, :, xs] = _ssd_direction(xr_ref[0, :, xs], br_ref[0, :, ns], cr_ref[0, :, ns], dcr_ref[0, 0, gi],
                                          drr_ref[0, 0, gi], ecr_ref[0, 0, gi], err_ref[0, 0, gi],
                                          tr_ref[0, 0, 0, gi], stb_ref.at[gi], False)


def _ssd_scan(xbc, dt, ecs, tot, d_skip, d_inner, n_groups):
    bsz, s, _ = xbc.shape
    nc = s // SSD_CHUNK
    hg = SSD_HEADS_PER_GROUP
    gw = hg * SSD_HEAD_DIM
    gps = min(SCAN_GROUPS_PER_STEP, n_groups)
    assert gw % LANES == 0 and d_inner == n_groups * gw and n_groups % gps == 0
    b0 = d_inner // (D_STATE * gps)
    c0 = b0 + n_groups // gps
    assert b0 * D_STATE * gps == d_inner

    def col_layout(a):
        return a.reshape(bsz, s, 2, n_groups, hg).transpose(0, 2, 3, 1, 4)

    def row_layout(a):
        return a.reshape(bsz, s, 2, n_groups, hg).transpose(0, 2, 3, 4, 1)

    dt_c, dt_r, e_c, e_r = col_layout(dt), row_layout(dt), col_layout(ecs), row_layout(ecs)
    tot6 = tot.reshape(bsz, nc, 2, n_groups, 1, hg)
    dskip = jnp.repeat(d_skip.astype(F32), SSD_HEAD_DIM).reshape(1, d_inner)

    def specs(direction, cidx):
        return [
            pl.BlockSpec((1, SSD_CHUNK, gw * gps), lambda b, g, c: (b, cidx(c), g)),
            pl.BlockSpec((1, SSD_CHUNK, D_STATE * gps), lambda b, g, c: (b, cidx(c), b0 + g)),
            pl.BlockSpec((1, SSD_CHUNK, D_STATE * gps), lambda b, g, c: (b, cidx(c), c0 + g)),
            pl.BlockSpec((1, 1, gps, SSD_CHUNK, hg), lambda b, g, c: (b, direction, g, cidx(c), 0)),
            pl.BlockSpec((1, 1, gps, hg, SSD_CHUNK), lambda b, g, c: (b, direction, g, 0, cidx(c))),
            pl.BlockSpec((1, 1, gps, SSD_CHUNK, hg), lambda b, g, c: (b, direction, g, cidx(c), 0)),
            pl.BlockSpec((1, 1, gps, hg, SSD_CHUNK), lambda b, g, c: (b, direction, g, 0, cidx(c))),
            pl.BlockSpec((1, 1, 1, gps, 1, hg), lambda b, g, c: (b, cidx(c), direction, g, 0, 0)),
        ]

    fwd = lambda c: c
    bwd = lambda c: nc - 1 - c
    y_shape = jax.ShapeDtypeStruct((bsz, s, d_inner), F32)
    return pl.pallas_call(
        functools.partial(_ssd_scan_kernel, gps=gps, gw=gw),
        grid=(bsz, n_groups // gps, nc),
        in_specs=specs(0, fwd) + specs(1, bwd) + [pl.BlockSpec((1, gw * gps), lambda b, g, c: (0, g))],
        out_specs=[pl.BlockSpec((1, SSD_CHUNK, gw * gps), lambda b, g, c: (b, c, g)),
                   pl.BlockSpec((1, SSD_CHUNK, gw * gps), lambda b, g, c: (b, nc - 1 - c, g))],
        out_shape=[y_shape, y_shape],
        scratch_shapes=[pltpu.VMEM((gps, D_STATE, gw), F32), pltpu.VMEM((gps, D_STATE, gw), F32)],
        compiler_params=_params("parallel", "parallel", "arbitrary"),
        name="ssd_scan",
    )(xbc, xbc, xbc, dt_c, dt_r, e_c, e_r, tot6,
      xbc, xbc, xbc, dt_c, dt_r, e_c, e_r, tot6, dskip)


def _gate_kernel(yf_ref, yb_ref, z_ref, nw_ref, o_ref):
    z = z_ref[...]
    y = (yf_ref[...] + yb_ref[...]) * (z / (1.0 + jnp.exp(-z)))
    y = y * lax.rsqrt(jnp.mean(y * y, axis=-1, keepdims=True) + RMS_EPS) * nw_ref[...]
    o_ref[...] = y.astype(BF16)


def _ssd_gate(yf2d, yb2d, zx2d, norm_w):
    n, d_inner = yf2d.shape
    tm = _pick(n, 256)
    blk = pl.BlockSpec((tm, d_inner), lambda i: (i, 0))
    return pl.pallas_call(
        _gate_kernel,
        grid=(n // tm,),
        in_specs=[blk, blk, blk, pl.BlockSpec((1, d_inner), lambda i: (0, 0))],
        out_specs=blk,
        out_shape=jax.ShapeDtypeStruct((n, d_inner), BF16),
        compiler_params=_params("parallel"),
        name="ssd_gate_norm",
    )(yf2d, yb2d, zx2d, norm_w.reshape(1, d_inner))


FFN_TILE = 256


def _ffn_kernel(idx_ref, x_hbm, wg_ref, wu_ref, wd_ref, o_ref, xa, xb, sem, *, tile, n_steps):
    step = pl.program_id(0) * pl.num_programs(1) + pl.program_id(1)

    def issue(tile_idx, buf, s):
        base = tile_idx * tile
        for r in range(tile):
            tok = idx_ref[base + r]
            pltpu.make_async_copy(x_hbm.at[pl.ds(tok, 1), :], buf.at[pl.ds(r, 1), :], sem.at[s]).start(priority=r % 2)

    def wait(buf, s):
        pltpu.make_async_copy(x_hbm.at[pl.ds(0, tile), :], buf, sem.at[s]).wait()

    def ffn(buf, half):
        xs = buf[...].astype(BF16)
        hg = jnp.dot(xs, wg_ref[0], preferred_element_type=F32)
        hu = jnp.dot(xs, wu_ref[0], preferred_element_type=F32)
        h = (hg / (1.0 + jnp.exp(-hg)) * hu).astype(BF16)
        o_ref[0, half * tile:(half + 1) * tile, :] = jnp.dot(h, wd_ref[0], preferred_element_type=F32).astype(BF16)

    @pl.when(step == 0)
    def _():
        issue(0, xa, 0)

    wait(xa, 0)
    issue(2 * step + 1, xb, 1)
    ffn(xa, 0)
    nxt = jnp.where(step + 1 < n_steps, 2 * step + 2, 0)
    issue(nxt, xa, 0)
    wait(xb, 1)
    ffn(xb, 1)

    @pl.when(step == n_steps - 1)
    def _():
        wait(xa, 0)


def _moe_ffn(x2d, idx, wg, wu, wd):
    n, d = x2d.shape
    n_exp, _, f = wg.shape
    cap = idx.shape[0] // n_exp
    tile = _pick(cap // 2, FFN_TILE)
    steps_per_exp = cap // (2 * tile)
    kern = functools.partial(_ffn_kernel, tile=tile, n_steps=n_exp * steps_per_exp)
    grid_spec = pltpu.PrefetchScalarGridSpec(
        num_scalar_prefetch=1,
        grid=(n_exp, steps_per_exp),
        in_specs=[pl.BlockSpec(memory_space=pl.ANY),
                  pl.BlockSpec((1, d, f), lambda e, t, ix: (e, 0, 0)),
                  pl.BlockSpec((1, d, f), lambda e, t, ix: (e, 0, 0)),
                  pl.BlockSpec((1, f, d), lambda e, t, ix: (e, 0, 0))],
        out_specs=pl.BlockSpec((1, 2 * tile, d), lambda e, t, ix: (e, t, 0)),
        scratch_shapes=[pltpu.VMEM((tile, d), F32), pltpu.VMEM((tile, d), F32), pltpu.SemaphoreType.DMA((2,))],
    )
    return pl.pallas_call(
        kern,
        grid_spec=grid_spec,
        out_shape=jax.ShapeDtypeStruct((n_exp, cap, d), BF16),
        compiler_params=_params("arbitrary", "arbitrary"),
        name="moe_ffn",
    )(idx, x2d, wg, wu, wd)


def _invert_kernel(st_ref, en_ref, post_ref, acc_ref, *, n_exp, win, cap_tot, tokens):
    i = pl.program_id(0)

    @pl.when(i == 0)
    def _():
        acc_ref[...] = jnp.zeros_like(acc_ref)

    t = tokens
    tok = i * t + lax.broadcasted_iota(jnp.int32, (t, LANES), 0)
    lane = lax.broadcasted_iota(jnp.int32, (t, LANES), 1)
    digits = jnp.where(lane == 0, lax.shift_right_logical(tok, 8), jnp.where(lane == 1, tok & 255, 0))
    digits = digits.astype(F32).astype(BF16)
    post = post_ref[...]
    row = lax.broadcasted_iota(jnp.int32, (win, t), 0)

    def window(e):
        s0 = st_ref[i * n_exp + e]
        return jnp.minimum(lax.shift_left(lax.shift_right_logical(s0, 4), 4), cap_tot - win)

    def place(e, w, r):
        w = pl.multiple_of(w, BF16_ROWS)
        acc_ref[pl.ds(w, win), :] += r if e == 0 else pltpu.roll(r, 2 * e, 1)

    ws = [window(e) for e in range(n_exp)]
    onehots = jnp.concatenate([jnp.where(post[e:e + 1, :] - ws[e] == row, 1.0, 0.0).astype(BF16)
                               for e in range(n_exp)], axis=0)
    res = jnp.dot(onehots, digits, preferred_element_type=F32)
    for e in range(n_exp):
        place(e, ws[e], res[e * win:(e + 1) * win])

    for e in range(n_exp):
        w = ws[e]
        pe = post[e:e + 1, :]
        s1 = en_ref[i * n_exp + e]
        n_extra = jnp.maximum(s1 - w - 1, 0) // win

        def extra(k, carry, e=e, w=w, pe=pe):
            lo = w + win * (k + 1)
            wk = jnp.minimum(lo, cap_tot - win)
            oh = jnp.where((pe - wk == row) & (pe >= lo), 1.0, 0.0).astype(BF16)
            place(e, wk, jnp.dot(oh, digits, preferred_element_type=F32))
            return carry

        lax.fori_loop(0, n_extra, extra, 0)


def _ec_invert(pos, starts, ends, cap_tot):
    n, n_exp = pos.shape
    t = COMBINE_TOKENS
    win = COMBINE_WINDOW
    assert 2 * n_exp <= LANES and n < 256 * 256
    kern = functools.partial(_invert_kernel, n_exp=n_exp, win=win, cap_tot=cap_tot, tokens=t)
    grid_spec = pltpu.PrefetchScalarGridSpec(
        num_scalar_prefetch=2,
        grid=(n // t,),
        in_specs=[pl.BlockSpec((n_exp, t), lambda i, st, en: (0, i))],
        out_specs=pl.BlockSpec((cap_tot, LANES), lambda i, st, en: (0, 0)),
    )
    acc = pl.pallas_call(
        kern,
        grid_spec=grid_spec,
        out_shape=jax.ShapeDtypeStruct((cap_tot, LANES), F32),
        compiler_params=_params("arbitrary"),
        name="ec_invert",
    )(starts, ends, pos.T)
    digits = acc[:, :2 * n_exp].astype(jnp.int32).reshape(cap_tot, n_exp, 2)
    return (digits[:, :, 0] * 256 + digits[:, :, 1]).T.reshape(-1)


SELECT_ROW_TILE = 512


def _select_kernel(aff_ref, pos_ref, cnt_ref, *, cap, n_exp):
    a = aff_ref[...]
    r = a.shape[0]
    bits = pltpu.bitcast(a, jnp.int32)

    def fold(v):
        sh = n_exp
        while sh < LANES:
            v = v + pltpu.roll(v, sh, 1)
            sh *= 2
        return v

    def count(mask):
        return fold(jnp.sum(jnp.where(mask, 1.0, 0.0), axis=0, keepdims=True))

    def search(i, thr):
        cand = thr | jnp.left_shift(jnp.int32(1), 30 - i)
        return jnp.where(count(bits >= cand) >= cap, cand, thr)

    thr = lax.fori_loop(0, 31, search, jnp.zeros((1, LANES), jnp.int32))
    above = bits > thr
    tied = bits == thr
    need = cap - count(above)

    li = lax.broadcasted_iota(jnp.int32, (LANES, 2 * LANES), 0)
    ci = lax.broadcasted_iota(jnp.int32, (LANES, 2 * LANES), 1)
    same_exp = (li & (n_exp - 1)) == (ci & (n_exp - 1))
    earlier = (li // n_exp) < ((ci & (LANES - 1)) // n_exp)
    w2 = jnp.where(same_exp & ((ci >= LANES) | earlier), 1.0, 0.0).astype(BF16)
    tr = min(SELECT_ROW_TILE, r)
    rr = lax.broadcasted_iota(jnp.int32, (tr, tr), 0)
    rc = lax.broadcasted_iota(jnp.int32, (tr, tr), 1)
    rows_before = jnp.where(rr > rc, 1.0, 0.0).astype(BF16)

    def prefix(mask):
        lw = jnp.dot(jnp.where(mask, 1.0, 0.0).astype(BF16), w2, preferred_element_type=F32)
        within, row_tot = lw[:, :LANES], lw[:, LANES:]
        carry = jnp.zeros((1, LANES), F32)
        outs = []
        for t in range(r // tr):
            rt = row_tot[t * tr:(t + 1) * tr]
            outs.append(jnp.dot(rows_before, rt.astype(BF16), preferred_element_type=F32) + carry
                        + within[t * tr:(t + 1) * tr])
            carry = carry + jnp.sum(rt, axis=0, keepdims=True)
        return jnp.concatenate(outs, axis=0)

    sel = above | (tied & (prefix(tied) < need))
    cnt = prefix(sel).astype(jnp.int32)
    pos_ref[...] = jnp.where(sel, cnt, -1)
    cnt_ref[...] = cnt


def _ec_select(aff_group, cap):
    n_g, n_exp = aff_group.shape
    assert LANES % n_exp == 0 and (n_exp & (n_exp - 1)) == 0
    r = n_g * n_exp // LANES
    assert r % min(SELECT_ROW_TILE, r) == 0
    shp = jax.ShapeDtypeStruct((r, LANES), jnp.int32)
    pos, cnt = pl.pallas_call(
        functools.partial(_select_kernel, cap=cap, n_exp=n_exp),
        out_shape=[shp, shp],
        compiler_params=pltpu.CompilerParams(vmem_limit_bytes=V7X_VMEM_LIMIT_BYTES),
        name="ec_select",
    )(aff_group.reshape(r, LANES))
    return pos.reshape(n_g, n_exp), cnt.reshape(n_g, n_exp)


COMBINE_TOKENS = 256
COMBINE_WINDOW = 64
BF16_ROWS = 16


def _combine_kernel(st_ref, en_ref, x_ref, pos_ref, aff_ref, g_ref, b_ref, o_hbm, *rest, alpha, n_exp, win, cap_tot,
                    n_tiles, split_tiles):
    n_out = 1 if split_tiles is None else 2
    out_refs = rest[:n_out]
    buf, sem, xbuf, xsem, acc_ref = rest[n_out:]
    i = pl.program_id(0)
    slot = i % 2

    def window(tile, e):
        s0 = st_ref[tile * n_exp + e]
        return jnp.minimum(lax.shift_left(lax.shift_right_logical(s0, 4), 4), cap_tot - win)

    def fetch(tile, sl, e):
        w = pl.multiple_of(window(tile, e), BF16_ROWS)
        return pltpu.make_async_copy(o_hbm.at[e, pl.ds(w, win), :], buf.at[sl, pl.ds(e * win, win), :], sem.at[sl, e])

    @pl.when(i == 0)
    def _():
        for e in range(n_exp):
            fetch(0, 0, e).start()

    @pl.when(i + 1 < n_tiles)
    def _():
        for e in range(n_exp):
            fetch(i + 1, 1 - slot, e).start()

    pos = pos_ref[...]
    aff = aff_ref[...]
    t = pos.shape[0]
    lane = lax.broadcasted_iota(jnp.int32, (t, 2 * win), 1)
    first = lane < win
    lane_in = jnp.where(first, lane, lane - win)
    parts = []
    for e in range(0, n_exp, 2):
        rel = jnp.where(first, pos[:, e:e + 1] - window(i, e), pos[:, e + 1:e + 2] - window(i, e + 1))
        gate = jnp.where(first, aff[:, e:e + 1], aff[:, e + 1:e + 2])
        parts.append(jnp.where(rel == lane_in, gate, 0.0))
    pmat = jnp.concatenate(parts, axis=1)
    p_hi = pmat.astype(BF16)
    p_lo = (pmat - p_hi.astype(F32)).astype(BF16)
    for e in range(n_exp):
        fetch(i, slot, e).wait()
    rows = buf[slot]
    acc_ref[...] = jnp.dot(p_hi, rows, preferred_element_type=F32) + jnp.dot(p_lo, rows, preferred_element_type=F32)

    lane1 = lax.broadcasted_iota(jnp.int32, (t, win), 1)
    for e in range(n_exp):
        w = window(i, e)
        s1 = en_ref[i * n_exp + e]
        n_extra = jnp.maximum(s1 - w - 1, 0) // win

        def extra(k, carry, e=e, w=w):
            lo = w + win * (k + 1)
            wk = pl.multiple_of(jnp.minimum(lo, cap_tot - win), BF16_ROWS)
            cp = pltpu.make_async_copy(o_hbm.at[e, pl.ds(wk, win), :], xbuf, xsem)
            cp.start()
            cp.wait()
            pe = pos[:, e:e + 1]
            oh = jnp.where((pe - wk == lane1) & (pe >= lo), 1.0, 0.0).astype(BF16)
            acc_ref[...] += jnp.dot(oh, xbuf[...], preferred_element_type=F32) * aff[:, e:e + 1]
            return carry

        lax.fori_loop(0, n_extra, extra, 0)
    y = _res_ln(x_ref[...], acc_ref[...], g_ref[...], b_ref[...], alpha)
    if split_tiles is None:
        out_refs[0][...] = y
    else:
        @pl.when(i < split_tiles)
        def _():
            out_refs[0][...] = y

        @pl.when(i >= split_tiles)
        def _():
            out_refs[1][...] = y


def _moe_combine_ln(x2d, pos, aff, starts, ends, o, g, b, alpha, split=None):
    n, d = x2d.shape
    n_exp, cap_tot, _ = o.shape
    t = COMBINE_TOKENS
    win = COMBINE_WINDOW
    assert n % t == 0 and cap_tot % BF16_ROWS == 0 and cap_tot >= win and n_exp % 2 == 0 and 2 * win == LANES
    n_tiles = n // t
    row = pl.BlockSpec((1, d), lambda i, st, en: (0, 0))
    tok = lambda w: pl.BlockSpec((t, w), lambda i, st, en: (i, 0))
    if split is None:
        split_tiles = None
        out_specs = tok(d)
        out_shape = jax.ShapeDtypeStruct((n, d), F32)
    else:
        assert split % t == 0 and 0 < split < n
        split_tiles = split // t
        out_specs = [pl.BlockSpec((t, d), lambda i, st, en: (jnp.minimum(i, split_tiles - 1), 0)),
                     pl.BlockSpec((t, d), lambda i, st, en: (jnp.maximum(i - split_tiles, 0), 0))]
        out_shape = [jax.ShapeDtypeStruct((split, d), F32), jax.ShapeDtypeStruct((n - split, d), F32)]
    kern = functools.partial(_combine_kernel, alpha=alpha, n_exp=n_exp, win=win, cap_tot=cap_tot, n_tiles=n_tiles,
                             split_tiles=split_tiles)
    grid_spec = pltpu.PrefetchScalarGridSpec(
        num_scalar_prefetch=2,
        grid=(n_tiles,),
        in_specs=[tok(d), tok(n_exp), tok(n_exp), row, row, pl.BlockSpec(memory_space=pl.ANY)],
        out_specs=out_specs,
        scratch_shapes=[pltpu.VMEM((2, n_exp * win, d), BF16), pltpu.SemaphoreType.DMA((2, n_exp)),
                        pltpu.VMEM((win, d), BF16), pltpu.SemaphoreType.DMA(()), pltpu.VMEM((t, d), F32)],
    )
    return pl.pallas_call(
        kern,
        grid_spec=grid_spec,
        out_shape=out_shape,
        compiler_params=_params("arbitrary"),
        name="moe_combine_ln",
    )(starts, ends, x2d, pos, aff, g.reshape(1, d), b.reshape(1, d), o)


def _ec_moe_ln(x2d, aff, groups, wg, wu, wd, g, b, alpha, split=None):
    n, d = x2d.shape
    n_exp = aff.shape[1]
    t = COMBINE_TOKENS
    pos_l, st_l, en_l = [], [], []
    off = 0
    for start, cnt_tok in groups:
        assert start % t == 0 and cnt_tok % t == 0
        cap = EC_CAPACITY_FACTOR * cnt_tok // n_exp
        pos, cnt = _ec_select(aff[start:start + cnt_tok], cap)
        st = cnt[::t] + off
        en = jnp.concatenate([st[1:], jnp.full((1, n_exp), off + cap, jnp.int32)], axis=0)
        pos_l.append(jnp.where(pos >= 0, pos + off, -1))
        st_l.append(st)
        en_l.append(en)
        off += cap
    cap_tot = off
    pos = jnp.concatenate(pos_l, axis=0)
    starts = jnp.concatenate(st_l, axis=0).reshape(-1)
    ends = jnp.concatenate(en_l, axis=0).reshape(-1)
    idx = _ec_invert(pos, starts, ends, cap_tot)
    o = _moe_ffn(x2d, idx, wg, wu, wd)
    return _moe_combine_ln(x2d, pos, aff, starts, ends, o, g, b, alpha, split)


def _split2_bf16(w):
    hi = w.astype(BF16)
    lo = (w - hi.astype(F32)).astype(BF16)
    return jnp.concatenate([hi, lo], axis=1)


def kernel(x_prompt, x_sample, attn_w_qkv, attn_q_norm, attn_k_norm, attn_w_o, pool_w, pool_scale, ssd_w_in,
           ssd_conv_w, ssd_conv_b, ssd_dt_bias, ssd_A_log, ssd_D, ssd_norm, ssd_w_out, moe_w_router, moe_w_gate,
           moe_w_up, moe_w_down, ln_g, ln_b):
    bp, s, d = x_prompt.shape
    bs = x_sample.shape[0]
    assert x_sample.shape[1] == s
    bsz = bp + bs
    n = bsz * s
    groups = [(0, bp * s), (bp * s, bs * s)]
    depth = ln_g.shape[0]
    alpha = (2 * depth) ** 0.25
    n_heads = attn_w_o.shape[1] // HEAD_DIM
    n_kv = (attn_w_qkv.shape[2] // HEAD_DIM - n_heads) // 2
    d_inner = ssd_w_out.shape[1]
    n_ssd_heads = ssd_A_log.shape[-1]
    assert d_inner == n_ssd_heads * SSD_HEAD_DIM
    conv_dim = ssd_conv_w.shape[2]
    n_groups = (conv_dim - d_inner) // (2 * D_STATE)
    rope = _rope_tables(s)

    x = jnp.concatenate([x_prompt, x_sample], axis=0).reshape(n, d)
    ia = ip = isd = 0
    for i in range(depth):
        wr2 = _split2_bf16(moe_w_router[i])
        g1, b1, g2, b2 = ln_g[i, 0], ln_b[i, 0], ln_g[i, 1], ln_b[i, 1]
        kind = i % 3
        if kind == 0:
            qkv = _qkv_proj(x, attn_w_qkv[ia].astype(BF16), attn_q_norm[ia], attn_k_norm[ia], rope, s, n_heads, n_kv)
            qkv3 = qkv.reshape(bsz, s, -1)
            vt = jnp.swapaxes(qkv3[:, :, (n_heads + n_kv) * HEAD_DIM:], 1, 2).reshape(bsz, n_kv, HEAD_DIM, s)
            o = _flash_attention(qkv3, vt, n_heads, n_kv)
            x, aff = _mm_res_ln_router(o.reshape(n, -1), attn_w_o[ia].astype(BF16), x, g1, b1, wr2, alpha)
            ia += 1
        elif kind == 1:
            x3, aff3 = _pool_layer(x.reshape(bsz, s, d), pool_w[ip].astype(BF16), pool_scale[ip], g1, b1, wr2, alpha)
            x, aff = x3.reshape(n, d), aff3.reshape(n, -1)
            ip += 1
        else:
            zx = _matmul_f32(x, ssd_w_in[isd].astype(BF16), 1152)
            zx3 = zx.reshape(bsz, s, -1)
            xbc = _ssd_conv(zx3, ssd_conv_w[isd], ssd_conv_b[isd], d_inner, conv_dim)
            a_neg = -jnp.exp(ssd_A_log[isd].astype(F32)).reshape(-1)
            dt, ecs, tot = _ssd_dt(zx3, ssd_dt_bias[isd].reshape(-1), a_neg, d_inner + conv_dim)
            yf, yb = _ssd_scan(xbc, dt, ecs, tot, ssd_D[isd], d_inner, n_groups)
            yn = _ssd_gate(yf.reshape(n, d_inner), yb.reshape(n, d_inner), zx, ssd_norm[isd])
            x, aff = _mm_res_ln_router(yn, ssd_w_out[isd].astype(BF16), x, g1, b1, wr2, alpha)
            isd += 1
        x = _ec_moe_ln(x, aff, groups, moe_w_gate[i].astype(BF16), moe_w_up[i].astype(BF16),
                       moe_w_down[i].astype(BF16), g2, b2, alpha, split=bp * s if i == depth - 1 else None)
    y_prompt, y_sample = x
    return y_prompt.reshape(bp, s, d), y_sample.reshape(bs, s, d)
```

```python
import functools
import math

import jax
import jax.numpy as jnp
from jax import lax
from jax.experimental import pallas as pl
from jax.experimental.pallas import tpu as pltpu

F32 = jnp.float32
BF16 = jnp.bfloat16

HEAD_DIM = 128
GRID_W = 64
ROPE_THETA = 10000.0
POOL_WINDOWS = (2, 4, 8, 16)
POOL_HALO = 8
D_STATE = 128
SSD_CHUNK = 128
SSD_HEAD_DIM = 64
SSD_HEADS_PER_GROUP = 8
D_CONV = 4
CONV_LEFT = D_CONV // 2
EC_CAPACITY_FACTOR = 2
LN_EPS = 1e-5
RMS_EPS = 1e-6
LOG2E = 1.4426950408889634

V7X_VMEM_LIMIT_BYTES = 52 * 1024 * 1024
LANES = 128


def _params(*sem):
    return pltpu.CompilerParams(dimension_semantics=sem, vmem_limit_bytes=V7X_VMEM_LIMIT_BYTES)


def _pick(n, pref):
    t = min(n, pref)
    while n % t:
        t //= 2
    return t


def _res_ln(x, h, g, b, alpha):
    y = alpha * x + h
    mu = jnp.mean(y, axis=-1, keepdims=True)
    yc = y - mu
    var = jnp.mean(yc * yc, axis=-1, keepdims=True)
    return yc * lax.rsqrt(var + LN_EPS) * g + b


def _router_affinity(xn, wr_ref, n_exp):
    xh = xn.astype(BF16)
    xl = (xn - xh.astype(F32)).astype(BF16)
    wr = wr_ref[...]
    r1 = jnp.dot(xh, wr, preferred_element_type=F32)
    r2 = jnp.dot(xl, wr[:, :n_exp], preferred_element_type=F32)
    logits = r1[:, :n_exp] + (r1[:, n_exp:] + r2)
    m = jnp.max(logits, axis=-1, keepdims=True)
    e = jnp.exp(logits - m)
    return e / jnp.sum(e, axis=-1, keepdims=True)


def _qkv_kernel(x_ref, w_ref, cos_ref, sa_ref, sb_ref, qn_ref, kn_ref, o_ref, xb_ref, *,
                n_q_tiles, n_k_tiles, heads_per_tile, q_scale):
    j = pl.program_id(1)

    @pl.when(j == 0)
    def _():
        xb_ref[...] = x_ref[...].astype(BF16)

    acc = jnp.dot(xb_ref[...], w_ref[...], preferred_element_type=F32)

    def norm_rope(gain_ref, scale):
        cos = cos_ref[...]
        sa = sa_ref[...]
        sb = sb_ref[...]
        g = gain_ref[...]
        for h in range(heads_per_tile):
            a = acc[:, h * HEAD_DIM:(h + 1) * HEAD_DIM]
            a = a * lax.rsqrt(jnp.mean(a * a, axis=-1, keepdims=True) + RMS_EPS) * g
            r = a * cos + pltpu.roll(a, HEAD_DIM - 32, 1) * sa + pltpu.roll(a, 32, 1) * sb
            o_ref[:, h * HEAD_DIM:(h + 1) * HEAD_DIM] = (r * scale).astype(BF16)

    @pl.when(j < n_q_tiles)
    def _():
        norm_rope(qn_ref, q_scale)

    @pl.when((j >= n_q_tiles) & (j < n_q_tiles + n_k_tiles))
    def _():
        norm_rope(kn_ref, 1.0)

    @pl.when(j >= n_q_tiles + n_k_tiles)
    def _():
        o_ref[...] = acc.astype(BF16)


def _rope_tables(seq_len):
    rows = seq_len // GRID_W
    row = jnp.repeat(jnp.arange(rows, dtype=F32), GRID_W)
    col = jnp.tile(jnp.arange(GRID_W, dtype=F32), rows)
    half = HEAD_DIM // 4
    inv_freq = ROPE_THETA ** (-jnp.arange(0, HEAD_DIM // 2, 2, dtype=F32) / (HEAD_DIM // 2))
    ang_r = row[:, None] * inv_freq
    ang_c = col[:, None] * inv_freq
    zeros = jnp.zeros((seq_len, half), F32)
    cos = jnp.concatenate([jnp.cos(ang_r)] * 2 + [jnp.cos(ang_c)] * 2, axis=-1)
    sin_a = jnp.concatenate([-jnp.sin(ang_r), zeros, -jnp.sin(ang_c), zeros], axis=-1)
    sin_b = jnp.concatenate([zeros, jnp.sin(ang_r), zeros, jnp.sin(ang_c)], axis=-1)
    return cos, sin_a, sin_b


def _qkv_proj(x2d, w_bf16, q_norm, k_norm, rope, seq_len, n_heads, n_kv):
    n, d = x2d.shape
    qkv_dim = w_bf16.shape[1]
    tn = n_kv * HEAD_DIM
    tm = _pick(seq_len, 512)
    cos, sa, sb = rope
    nsb = seq_len // tm
    kern = functools.partial(
        _qkv_kernel, n_q_tiles=n_heads // n_kv, n_k_tiles=1, heads_per_tile=n_kv,
        q_scale=HEAD_DIM ** -0.5 * LOG2E)
    tab = pl.BlockSpec((tm, HEAD_DIM), lambda i, j: (i % nsb, 0))
    vec = pl.BlockSpec((1, HEAD_DIM), lambda i, j: (0, 0))
    return pl.pallas_call(
        kern,
        grid=(n // tm, qkv_dim // tn),
        in_specs=[pl.BlockSpec((tm, d), lambda i, j: (i, 0)),
                  pl.BlockSpec((d, tn), lambda i, j: (0, j)),
                  tab, tab, tab, vec, vec],
        out_specs=pl.BlockSpec((tm, tn), lambda i, j: (i, j)),
        out_shape=jax.ShapeDtypeStruct((n, qkv_dim), BF16),
        scratch_shapes=[pltpu.VMEM((tm, d), BF16)],
        compiler_params=_params("parallel", "arbitrary"),
        name="qkv_proj",
    )(x2d, w_bf16, cos, sa, sb, q_norm.reshape(1, HEAD_DIM), k_norm.reshape(1, HEAD_DIM))


FLASH_TQ = 128
FLASH_TK = 512


def _flash_kernel(q_ref, k_ref, vt_ref, o_ref, s_scr, p_scr, acc_scr, *, tk, group):
    tq = q_ref.shape[1]
    seq = k_ref.shape[1]
    q = jnp.concatenate([q_ref[0, :, g * HEAD_DIM:(g + 1) * HEAD_DIM] for g in range(group)], axis=0)
    rows = group * tq
    nc = seq // tk

    def scores(c, slot):
        k = k_ref[0, pl.ds(c * tk, tk), :]
        s_scr[slot] = lax.dot_general(k, q, (((1,), (1,)), ((), ())), preferred_element_type=F32)

    def pv(c, slot, alpha):
        vt = vt_ref[0, 0, :, pl.ds(c * tk, tk)]
        acc_scr[...] = acc_scr[...] * alpha + jnp.dot(vt, p_scr[slot], preferred_element_type=F32)

    def softmax(slot, m, l):
        s = s_scr[slot]
        m_new = jnp.maximum(m, jnp.max(s, axis=0, keepdims=True))
        alpha = jnp.exp2(m - m_new)
        p = jnp.exp2(s - m_new)
        l = alpha * l + jnp.sum(p, axis=0, keepdims=True)
        p_scr[slot] = p.astype(BF16)
        return m_new, l, alpha

    m = jnp.full((1, rows), -jnp.inf, F32)
    l = jnp.zeros((1, rows), F32)
    acc_scr[...] = jnp.zeros_like(acc_scr)
    scores(0, 0)
    if nc > 1:
        scores(1, 1)
    m, l, alpha = softmax(0, m, l)
    for c in range(1, nc):
        if c + 1 < nc:
            scores(c + 1, (c + 1) % 2)
        pv(c - 1, (c - 1) % 2, alpha)
        m, l, alpha = softmax(c % 2, m, l)
    pv(nc - 1, (nc - 1) % 2, alpha)
    o = (acc_scr[...] / l).T
    for g in range(group):
        o_ref[0, :, g * HEAD_DIM:(g + 1) * HEAD_DIM] = o[g * tq:(g + 1) * tq].astype(BF16)


def _flash_attention(qkv, vt, n_heads, n_kv):
    b, s, _ = qkv.shape
    group = n_heads // n_kv
    tq = _pick(s, FLASH_TQ)
    tk = _pick(s, FLASH_TK)
    gw = group * HEAD_DIM
    rows = group * tq
    vrows = vt.shape[2]
    kern = functools.partial(_flash_kernel, tk=tk, group=group)
    return pl.pallas_call(
        kern,
        grid=(b, n_kv, s // tq),
        in_specs=[pl.BlockSpec((1, tq, gw), lambda bi, h, i: (bi, i, h)),
                  pl.BlockSpec((1, s, HEAD_DIM), lambda bi, h, i: (bi, 0, n_heads + h)),
                  pl.BlockSpec((1, 1, vrows, s), lambda bi, h, i: (bi, h, 0, 0))],
        out_specs=pl.BlockSpec((1, tq, gw), lambda bi, h, i: (bi, i, h)),
        out_shape=jax.ShapeDtypeStruct((b, s, n_heads * HEAD_DIM), BF16),
        scratch_shapes=[pltpu.VMEM((2, tk, rows), F32), pltpu.VMEM((2, tk, rows), BF16),
                        pltpu.VMEM((vrows, rows), F32)],
        compiler_params=_params("parallel", "parallel", "arbitrary"),
        name="flash_attention",
    )(qkv, qkv, vt)


MM_LN_TM = 512
MM_LN_TK = 2048


def _mm_ln_kernel(a_ref, w_ref, x_ref, g_ref, b_ref, wr_ref, o_ref, aff_ref, acc_ref, *, alpha, nk, n_exp):
    k = pl.program_id(1)
    part = jnp.dot(a_ref[...], w_ref[...], preferred_element_type=F32)

    def finish(h):
        xn = _res_ln(x_ref[...], h, g_ref[...], b_ref[...], alpha)
        o_ref[...] = xn
        aff_ref[...] = _router_affinity(xn, wr_ref, n_exp)

    if nk == 1:
        finish(part)
        return

    @pl.when(k == 0)
    def _():
        acc_ref[...] = part

    @pl.when((k > 0) & (k < nk - 1))
    def _():
        acc_ref[...] += part

    @pl.when(k == nk - 1)
    def _():
        finish(acc_ref[...] + part)


def _mm_res_ln_router(a_bf16, w_bf16, x2d, g, b, wr2, alpha):
    n, kdim = a_bf16.shape
    d = w_bf16.shape[1]
    n_exp = wr2.shape[1] // 2
    tm = _pick(n, MM_LN_TM)
    tk = _pick(kdim, MM_LN_TK)
    nk = kdim // tk
    kern = functools.partial(_mm_ln_kernel, alpha=alpha, nk=nk, n_exp=n_exp)
    row = pl.BlockSpec((1, d), lambda i, k: (0, 0))
    return pl.pallas_call(
        kern,
        grid=(n // tm, nk),
        in_specs=[pl.BlockSpec((tm, tk), lambda i, k: (i, k)),
                  pl.BlockSpec((tk, d), lambda i, k: (k, 0)),
                  pl.BlockSpec((tm, d), lambda i, k: (i, 0)),
                  row, row,
                  pl.BlockSpec((d, 2 * n_exp), lambda i, k: (0, 0))],
        out_specs=[pl.BlockSpec((tm, d), lambda i, k: (i, 0)),
                   pl.BlockSpec((tm, n_exp), lambda i, k: (i, 0))],
        out_shape=[jax.ShapeDtypeStruct((n, d), F32), jax.ShapeDtypeStruct((n, n_exp), F32)],
        scratch_shapes=[pltpu.VMEM((tm, d), F32)],
        compiler_params=_params("parallel", "arbitrary"),
        name="mm_res_ln_router",
    )(a_bf16, w_bf16, x2d, g.reshape(1, d), b.reshape(1, d), wr2)


def _pool_kernel(prev_ref, cur_ref, next_ref, w_ref, sc_ref, g_ref, b_ref, wr_ref, o_ref, aff_ref, ext_ref, *,
                 alpha, nt, seq_len, n_exp):
    i = pl.program_id(1)
    t = cur_ref.shape[1]
    d = cur_ref.shape[2]
    pg = d // len(POOL_WINDOWS)
    x = cur_ref[0]
    ext_ref[0:POOL_HALO, :] = jnp.where(i == 0, 0.0, prev_ref[0])
    ext_ref[POOL_HALO:POOL_HALO + t, :] = x
    ext_ref[POOL_HALO + t:2 * POOL_HALO + t, :] = jnp.where(i == nt - 1, 0.0, next_ref[0])
    pos = i * t + lax.broadcasted_iota(jnp.int32, (t, 1), 0)
    hs = []
    for gi, w in enumerate(POOL_WINDOWS):
        half = w // 2
        cols = slice(gi * pg, (gi + 1) * pg)
        acc = ext_ref[pl.ds(POOL_HALO - half, t), cols]
        for jj in range(1, w):
            acc = acc + ext_ref[pl.ds(POOL_HALO - half + jj, t), cols]
        cnt = (jnp.minimum(pos + half, seq_len) - jnp.maximum(pos - half, 0)).astype(F32)
        mixed = (acc / cnt - x[:, cols]).astype(BF16)
        hs.append(jnp.dot(mixed, w_ref[gi], preferred_element_type=F32))
    h = jnp.concatenate(hs, axis=-1) * sc_ref[...]
    xn = _res_ln(x, h, g_ref[...], b_ref[...], alpha)
    o_ref[0] = xn
    aff_ref[0] = _router_affinity(xn, wr_ref, n_exp)


def _pool_layer(x3d, w_bf16, scale, g, b, wr2, alpha):
    bsz, s, d = x3d.shape
    n_exp = wr2.shape[1] // 2
    t = _pick(s, 256)
    nt = s // t
    hb = t // POOL_HALO
    last_hb = s // POOL_HALO - 1
    pg = d // len(POOL_WINDOWS)
    kern = functools.partial(_pool_kernel, alpha=alpha, nt=nt, seq_len=s, n_exp=n_exp)
    row = pl.BlockSpec((1, d), lambda bi, i: (0, 0))
    return pl.pallas_call(
        kern,
        grid=(bsz, nt),
        in_specs=[pl.BlockSpec((1, POOL_HALO, d), lambda bi, i: (bi, jnp.maximum(i * hb - 1, 0), 0)),
                  pl.BlockSpec((1, t, d), lambda bi, i: (bi, i, 0)),
                  pl.BlockSpec((1, POOL_HALO, d), lambda bi, i: (bi, jnp.minimum((i + 1) * hb, last_hb), 0)),
                  pl.BlockSpec((len(POOL_WINDOWS), pg, pg), lambda bi, i: (0, 0, 0)),
                  row, row, row,
                  pl.BlockSpec((d, 2 * n_exp), lambda bi, i: (0, 0))],
        out_specs=[pl.BlockSpec((1, t, d), lambda bi, i: (bi, i, 0)),
                   pl.BlockSpec((1, t, n_exp), lambda bi, i: (bi, i, 0))],
        out_shape=[jax.ShapeDtypeStruct((bsz, s, d), F32), jax.ShapeDtypeStruct((bsz, s, n_exp), F32)],
        scratch_shapes=[pltpu.VMEM((t + 2 * POOL_HALO, d), F32)],
        compiler_params=_params("parallel", "parallel"),
        name="pool_mixer",
    )(x3d, x3d, x3d, w_bf16, scale.reshape(1, d), g.reshape(1, d), b.reshape(1, d), wr2)


def _mm_kernel(x_ref, w_ref, o_ref, xb_ref):
    @pl.when(pl.program_id(1) == 0)
    def _():
        xb_ref[...] = x_ref[...].astype(BF16)

    o_ref[...] = jnp.dot(xb_ref[...], w_ref[...], preferred_element_type=F32)


def _matmul_f32(x2d, w_bf16, tn_pref):
    n, d = x2d.shape
    nout = w_bf16.shape[1]
    tm = _pick(n, 1024)
    tn = tn_pref
    assert nout % tn == 0
    return pl.pallas_call(
        _mm_kernel,
        grid=(n // tm, nout // tn),
        in_specs=[pl.BlockSpec((tm, d), lambda i, j: (i, 0)),
                  pl.BlockSpec((d, tn), lambda i, j: (0, j))],
        out_specs=pl.BlockSpec((tm, tn), lambda i, j: (i, j)),
        out_shape=jax.ShapeDtypeStruct((n, nout), F32),
        scratch_shapes=[pltpu.VMEM((tm, d), BF16)],
        compiler_params=_params("parallel", "arbitrary"),
        name="ssd_in_proj",
    )(x2d, w_bf16)


def _conv_kernel(prev_ref, cur_ref, next_ref, w_ref, b_ref, o_ref, ext_ref, *, nt):
    i = pl.program_id(1)
    t = cur_ref.shape[1]
    ext_ref[0:POOL_HALO, :] = jnp.where(i == 0, 0.0, prev_ref[0])
    ext_ref[POOL_HALO:POOL_HALO + t, :] = cur_ref[0]
    ext_ref[POOL_HALO + t:2 * POOL_HALO + t, :] = jnp.where(i == nt - 1, 0.0, next_ref[0])
    acc = ext_ref[pl.ds(POOL_HALO - CONV_LEFT, t), :] * w_ref[0:1, :]
    for kk in range(1, D_CONV):
        acc = acc + ext_ref[pl.ds(POOL_HALO - CONV_LEFT + kk, t), :] * w_ref[kk:kk + 1, :]
    acc = acc + b_ref[...]
    o_ref[0] = acc / (1.0 + jnp.exp(-acc))


def _ssd_conv(zx3d, conv_w, conv_b, d_inner, conv_dim):
    bsz, s, _ = zx3d.shape
    tc = 512
    t = _pick(s, 512)
    nt = s // t
    hb = t // POOL_HALO
    last_hb = s // POOL_HALO - 1
    c0 = d_inner // tc
    return pl.pallas_call(
        functools.partial(_conv_kernel, nt=nt),
        grid=(bsz, nt, conv_dim // tc),
        in_specs=[pl.BlockSpec((1, POOL_HALO, tc), lambda bi, i, j: (bi, jnp.maximum(i * hb - 1, 0), c0 + j)),
                  pl.BlockSpec((1, t, tc), lambda bi, i, j: (bi, i, c0 + j)),
                  pl.BlockSpec((1, POOL_HALO, tc), lambda bi, i, j: (bi, jnp.minimum((i + 1) * hb, last_hb), c0 + j)),
                  pl.BlockSpec((D_CONV, tc), lambda bi, i, j: (0, j)),
                  pl.BlockSpec((1, tc), lambda bi, i, j: (0, j))],
        out_specs=pl.BlockSpec((1, t, tc), lambda bi, i, j: (bi, i, j)),
        out_shape=jax.ShapeDtypeStruct((bsz, s, conv_dim), F32),
        scratch_shapes=[pltpu.VMEM((t + 2 * POOL_HALO, tc), F32)],
        compiler_params=_params("parallel", "parallel", "parallel"),
        name="ssd_conv",
    )(zx3d, zx3d, zx3d, conv_w, conv_b.reshape(1, conv_dim))


def _split3(x):
    hi = x.astype(BF16)
    r1 = x - hi.astype(F32)
    mid = r1.astype(BF16)
    lo = (r1 - mid.astype(F32)).astype(BF16)
    return hi, mid, lo


def _dt_kernel(raw_ref, bias_ref, a_ref, dt_ref, e_ref, dtt_ref, et_ref, tot_ref):
    v = raw_ref[0] + bias_ref[...]
    dt = jnp.maximum(v, 0.0) + jnp.log1p(jnp.exp(-jnp.abs(v)))
    dt_ref[0] = dt
    dtt_ref[0] = dt.T
    a = dt * a_ref[...]
    q, w = a.shape
    li = lax.broadcasted_iota(jnp.int32, (q, q), 0)
    si = lax.broadcasted_iota(jnp.int32, (q, q), 1)
    tri = jnp.where(li >= si, 1.0, 0.0).astype(BF16)
    hi, mid, lo = _split3(a)
    cs = (jnp.dot(tri, lo, preferred_element_type=F32) + jnp.dot(tri, mid, preferred_element_type=F32)
          + jnp.dot(tri, hi, preferred_element_type=F32))
    lane = lax.broadcasted_iota(jnp.int32, (q, w), 1)
    e = jnp.where(lane < w // 2, cs, cs - a)
    e_ref[0] = e
    et_ref[0] = e.T
    tot_ref[0, 0] = cs[q - 1:q, :]


def _ssd_dt(zx3d, dt_bias, a_neg, col0):
    bsz, s, _ = zx3d.shape
    w = dt_bias.shape[-1]
    assert w == LANES and col0 % LANES == 0 and SSD_CHUNK == LANES
    nc = s // SSD_CHUNK
    blk = pl.BlockSpec((1, SSD_CHUNK, w), lambda bi, c: (bi, c, 0))
    blk_t = pl.BlockSpec((1, w, SSD_CHUNK), lambda bi, c: (bi, 0, c))
    row = pl.BlockSpec((1, w), lambda bi, c: (0, 0))
    return pl.pallas_call(
        _dt_kernel,
        grid=(bsz, nc),
        in_specs=[pl.BlockSpec((1, SSD_CHUNK, w), lambda bi, c: (bi, c, col0 // LANES)), row, row],
        out_specs=[blk, blk, blk_t, blk_t, pl.BlockSpec((1, 1, 1, w), lambda bi, c: (bi, c, 0, 0))],
        out_shape=[jax.ShapeDtypeStruct((bsz, s, w), F32), jax.ShapeDtypeStruct((bsz, s, w), F32),
                   jax.ShapeDtypeStruct((bsz, w, s), F32), jax.ShapeDtypeStruct((bsz, w, s), F32),
                   jax.ShapeDtypeStruct((bsz, nc, 1, w), F32)],
        compiler_params=_params("parallel", "parallel"),
        name="ssd_dt",
    )(zx3d, dt_bias.reshape(1, w), a_neg.reshape(1, w))


def _expand_heads(v, j0, width):
    m = v.shape[0]
    lane = lax.broadcasted_iota(jnp.int32, (m, LANES), 1)
    parts = []
    for pr in range(width // LANES):
        j = j0 + 2 * pr
        parts.append(jnp.where(lane < SSD_HEAD_DIM, v[:, j:j + 1], v[:, j + 1:j + 2]))
    return jnp.concatenate(parts, axis=1)


def _ssd_direction(x, bmat, cmat, dt_r, e_c, e_r, out_dec, st_w, chunk_dec, j0, st_ref, forward):
    q, width = x.shape
    li = lax.broadcasted_iota(jnp.int32, (q, q), 0)
    si = lax.broadcasted_iota(jnp.int32, (q, q), 1)
    lane = lax.broadcasted_iota(jnp.int32, (q, LANES), 1)
    mask = (li >= si) if forward else (si >= li)
    cb = lax.dot_general(cmat.astype(BF16), bmat.astype(BF16), (((1,), (1,)), ((), ())),
                         preferred_element_type=F32)
    xb = x.astype(BF16)
    y_parts = []
    for pr in range(width // LANES):
        ms = []
        for j in (j0 + 2 * pr, j0 + 2 * pr + 1):
            if forward:
                diff = e_c[:, j:j + 1] - e_r[j:j + 1, :]
            else:
                diff = e_r[j:j + 1, :] - e_c[:, j:j + 1]
            decay = jnp.exp(jnp.where(mask, diff, -jnp.inf))
            ms.append((decay * cb * dt_r[j:j + 1, :]).astype(BF16))
        xp = xb[:, pr * LANES:(pr + 1) * LANES]
        zero = jnp.zeros_like(xp)
        rhs = jnp.concatenate([jnp.where(lane < SSD_HEAD_DIM, xp, zero),
                               jnp.where(lane >= SSD_HEAD_DIM, xp, zero)], axis=0)
        y_parts.append(jnp.dot(jnp.concatenate(ms, axis=1), rhs, preferred_element_type=F32))
    y = jnp.concatenate(y_parts, axis=1)
    st = st_ref[...]
    y = y + jnp.dot(cmat.astype(BF16), st.astype(BF16), preferred_element_type=F32) * _expand_heads(out_dec, j0, width)
    xd = (x * _expand_heads(st_w, j0, width)).astype(BF16)
    st_new = lax.dot_general(bmat.astype(BF16), xd, (((0,), (0,)), ((), ())), preferred_element_type=F32)
    st_ref[...] = st * _expand_heads(chunk_dec, j0, width)[0:1, :] + st_new
    return y


def _ssd_scan_kernel(xf_ref, bf_ref, cf_ref, dtf_ref, ef_ref, dttf_ref, etf_ref, tf_ref,
                     xr_ref, br_ref, cr_ref, dtr_ref, er_ref, dttr_ref, etr_ref, tr_ref,
                     dskip_ref, yf_ref, yb_ref, stf_ref, stb_ref, *, n_groups, gw):
    @pl.when(pl.program_id(1) == 0)
    def _():
        stf_ref[...] = jnp.zeros_like(stf_ref)
        stb_ref[...] = jnp.zeros_like(stb_ref)

    hg = SSD_HEADS_PER_GROUP
    ef, tf = ef_ref[0], tf_ref[0, 0]
    out_dec_f = jnp.exp(ef)
    st_w_f = dtf_ref[0] * jnp.exp(tf - ef)
    chunk_dec_f = jnp.broadcast_to(jnp.exp(tf), (8, LANES))
    er, tr = er_ref[0], tr_ref[0, 0]
    out_dec_r = jnp.exp(tr - er)
    st_w_r = dtr_ref[0] * jnp.exp(er)
    chunk_dec_r = jnp.broadcast_to(jnp.exp(tr), (8, LANES))
    dttf, etf, dttr, etr = dttf_ref[0], etf_ref[0], dttr_ref[0], etr_ref[0]
    for gi in range(n_groups):
        xs = slice(gi * gw, (gi + 1) * gw)
        ns = slice(gi * D_STATE, (gi + 1) * D_STATE)
        xf = xf_ref[0, :, xs]
        yf = _ssd_direction(xf, bf_ref[0, :, ns], cf_ref[0, :, ns], dttf, ef, etf, out_dec_f, st_w_f, chunk_dec_f,
                            gi * hg, stf_ref.at[gi], True)
        yf_ref[0, :, xs] = yf + dskip_ref[:, xs] * xf
        yb_ref[0, :, xs] = _ssd_direction(xr_ref[0, :, xs], br_ref[0, :, ns], cr_ref[0, :, ns], dttr, er, etr,
                                          out_dec_r, st_w_r, chunk_dec_r, (n_groups + gi) * hg, stb_ref.at[gi],
                                          False)


def _ssd_scan(xbc, dt, ecs, dt_t, ecs_t, tot, d_skip, d_inner, n_groups):
    bsz, s, _ = xbc.shape
    nc = s // SSD_CHUNK
    gw = SSD_HEADS_PER_GROUP * SSD_HEAD_DIM
    w = dt.shape[-1]
    assert gw % LANES == 0 and d_inner == n_groups * gw and w == 2 * n_groups * SSD_HEADS_PER_GROUP == LANES
    gn = n_groups * D_STATE
    assert d_inner % gn == 0
    b0 = d_inner // gn
    dskip = jnp.repeat(d_skip.astype(F32), SSD_HEAD_DIM).reshape(1, d_inner)

    def specs(cidx):
        return [
            pl.BlockSpec((1, SSD_CHUNK, d_inner), lambda b, c: (b, cidx(c), 0)),
            pl.BlockSpec((1, SSD_CHUNK, gn), lambda b, c: (b, cidx(c), b0)),
            pl.BlockSpec((1, SSD_CHUNK, gn), lambda b, c: (b, cidx(c), b0 + 1)),
            pl.BlockSpec((1, SSD_CHUNK, w), lambda b, c: (b, cidx(c), 0)),
            pl.BlockSpec((1, SSD_CHUNK, w), lambda b, c: (b, cidx(c), 0)),
            pl.BlockSpec((1, w, SSD_CHUNK), lambda b, c: (b, 0, cidx(c))),
            pl.BlockSpec((1, w, SSD_CHUNK), lambda b, c: (b, 0, cidx(c))),
            pl.BlockSpec((1, 1, 1, w), lambda b, c: (b, cidx(c), 0, 0)),
        ]

    fwd = lambda c: c
    bwd = lambda c: nc - 1 - c
    y_shape = jax.ShapeDtypeStruct((bsz, s, d_inner), F32)
    return pl.pallas_call(
        functools.partial(_ssd_scan_kernel, n_groups=n_groups, gw=gw),
        grid=(bsz, nc),
        in_specs=specs(fwd) + specs(bwd) + [pl.BlockSpec((1, d_inner), lambda b, c: (0, 0))],
        out_specs=[pl.BlockSpec((1, SSD_CHUNK, d_inner), lambda b, c: (b, c, 0)),
                   pl.BlockSpec((1, SSD_CHUNK, d_inner), lambda b, c: (b, nc - 1 - c, 0))],
        out_shape=[y_shape, y_shape],
        scratch_shapes=[pltpu.VMEM((n_groups, D_STATE, gw), F32), pltpu.VMEM((n_groups, D_STATE, gw), F32)],
        compiler_params=_params("parallel", "arbitrary"),
        name="ssd_scan",
    )(xbc, xbc, xbc, dt, ecs, dt_t, ecs_t, tot,
      xbc, xbc, xbc, dt, ecs, dt_t, ecs_t, tot, dskip)


def _gate_kernel(yf_ref, yb_ref, z_ref, nw_ref, o_ref):
    z = z_ref[...]
    y = (yf_ref[...] + yb_ref[...]) * (z / (1.0 + jnp.exp(-z)))
    y = y * lax.rsqrt(jnp.mean(y * y, axis=-1, keepdims=True) + RMS_EPS) * nw_ref[...]
    o_ref[...] = y.astype(BF16)


def _ssd_gate(yf2d, yb2d, zx2d, norm_w):
    n, d_inner = yf2d.shape
    tm = _pick(n, 256)
    blk = pl.BlockSpec((tm, d_inner), lambda i: (i, 0))
    return pl.pallas_call(
        _gate_kernel,
        grid=(n // tm,),
        in_specs=[blk, blk, blk, pl.BlockSpec((1, d_inner), lambda i: (0, 0))],
        out_specs=blk,
        out_shape=jax.ShapeDtypeStruct((n, d_inner), BF16),
        compiler_params=_params("parallel"),
        name="ssd_gate_norm",
    )(yf2d, yb2d, zx2d, norm_w.reshape(1, d_inner))


FFN_TILE = 256


def _ffn_kernel(idx_ref, x_hbm, wg_ref, wu_ref, wd_ref, o_ref, xa, xb, sem, *, tile, n_steps):
    step = pl.program_id(0) * pl.num_programs(1) + pl.program_id(1)

    def issue(tile_idx, buf, s):
        base = tile_idx * tile
        for r in range(tile):
            tok = idx_ref[base + r]
            pltpu.make_async_copy(x_hbm.at[pl.ds(tok, 1), :], buf.at[pl.ds(r, 1), :], sem.at[s]).start(priority=r % 2)

    def wait(buf, s):
        pltpu.make_async_copy(x_hbm.at[pl.ds(0, tile), :], buf, sem.at[s]).wait()

    def ffn(buf, half):
        xs = buf[...].astype(BF16)
        hg = jnp.dot(xs, wg_ref[0, 0], preferred_element_type=F32)
        hu = jnp.dot(xs, wu_ref[0, 0], preferred_element_type=F32)
        h = (hg / (1.0 + jnp.exp(-hg)) * hu).astype(BF16)
        o_ref[0, half * tile:(half + 1) * tile, :] = jnp.dot(h, wd_ref[0, 0], preferred_element_type=F32).astype(BF16)

    @pl.when(step == 0)
    def _():
        issue(0, xa, 0)

    wait(xa, 0)
    issue(2 * step + 1, xb, 1)
    ffn(xa, 0)
    nxt = jnp.where(step + 1 < n_steps, 2 * step + 2, 0)
    issue(nxt, xa, 0)
    wait(xb, 1)
    ffn(xb, 1)

    @pl.when(step == n_steps - 1)
    def _():
        wait(xa, 0)


def _moe_ffn(x2d, idx, wg, wu, wd, layer):
    n, d = x2d.shape
    _, n_exp, _, f = wg.shape
    cap = idx.shape[0] // n_exp
    tile = _pick(cap // 2, FFN_TILE)
    steps_per_exp = cap // (2 * tile)
    kern = functools.partial(_ffn_kernel, tile=tile, n_steps=n_exp * steps_per_exp)
    grid_spec = pltpu.PrefetchScalarGridSpec(
        num_scalar_prefetch=1,
        grid=(n_exp, steps_per_exp),
        in_specs=[pl.BlockSpec(memory_space=pl.ANY),
                  pl.BlockSpec((1, 1, d, f), lambda e, t, ix: (layer, e, 0, 0)),
                  pl.BlockSpec((1, 1, d, f), lambda e, t, ix: (layer, e, 0, 0)),
                  pl.BlockSpec((1, 1, f, d), lambda e, t, ix: (layer, e, 0, 0))],
        out_specs=pl.BlockSpec((1, 2 * tile, d), lambda e, t, ix: (e, t, 0)),
        scratch_shapes=[pltpu.VMEM((tile, d), F32), pltpu.VMEM((tile, d), F32), pltpu.SemaphoreType.DMA((2,))],
    )
    return pl.pallas_call(
        kern,
        grid_spec=grid_spec,
        out_shape=jax.ShapeDtypeStruct((n_exp, cap, d), BF16),
        compiler_params=_params("arbitrary", "arbitrary"),
        name="moe_ffn",
    )(idx, x2d, wg, wu, wd)


def _invert_kernel(st_ref, en_ref, post_ref, acc_ref, *, n_exp, win, cap_tot, tokens):
    i = pl.program_id(0)

    @pl.when(i == 0)
    def _():
        acc_ref[...] = jnp.zeros_like(acc_ref)

    t = tokens
    tok = i * t + lax.broadcasted_iota(jnp.int32, (t, LANES), 0)
    lane = lax.broadcasted_iota(jnp.int32, (t, LANES), 1)
    digits = jnp.where(lane == 0, lax.shift_right_logical(tok, 8), jnp.where(lane == 1, tok & 255, 0))
    digits = digits.astype(F32).astype(BF16)
    post = post_ref[...]
    row = lax.broadcasted_iota(jnp.int32, (win, t), 0)

    def window(e):
        s0 = st_ref[i * n_exp + e]
        return jnp.minimum(lax.shift_left(lax.shift_right_logical(s0, 4), 4), cap_tot - win)

    def place(e, w, r):
        w = pl.multiple_of(w, BF16_ROWS)
        acc_ref[pl.ds(w, win), :] += r if e == 0 else pltpu.roll(r, 2 * e, 1)

    ws = [window(e) for e in range(n_exp)]
    onehots = jnp.concatenate([jnp.where(post[e:e + 1, :] - ws[e] == row, 1.0, 0.0).astype(BF16)
                               for e in range(n_exp)], axis=0)
    res = jnp.dot(onehots, digits, preferred_element_type=F32)
    for e in range(n_exp):
        place(e, ws[e], res[e * win:(e + 1) * win])

    for e in range(n_exp):
        w = ws[e]
        pe = post[e:e + 1, :]
        s1 = en_ref[i * n_exp + e]
        n_extra = jnp.maximum(s1 - w - 1, 0) // win

        def extra(k, carry, e=e, w=w, pe=pe):
            lo = w + win * (k + 1)
            wk = jnp.minimum(lo, cap_tot - win)
            oh = jnp.where((pe - wk == row) & (pe >= lo), 1.0, 0.0).astype(BF16)
            place(e, wk, jnp.dot(oh, digits, preferred_element_type=F32))
            return carry

        lax.fori_loop(0, n_extra, extra, 0)


def _ec_invert(pos, starts, ends, cap_tot):
    n, n_exp = pos.shape
    t = COMBINE_TOKENS
    win = COMBINE_WINDOW
    assert 2 * n_exp <= LANES and n < 256 * 256
    kern = functools.partial(_invert_kernel, n_exp=n_exp, win=win, cap_tot=cap_tot, tokens=t)
    grid_spec = pltpu.PrefetchScalarGridSpec(
        num_scalar_prefetch=2,
        grid=(n // t,),
        in_specs=[pl.BlockSpec((n_exp, t), lambda i, st, en: (0, i))],
        out_specs=pl.BlockSpec((cap_tot, LANES), lambda i, st, en: (0, 0)),
    )
    acc = pl.pallas_call(
        kern,
        grid_spec=grid_spec,
        out_shape=jax.ShapeDtypeStruct((cap_tot, LANES), F32),
        compiler_params=_params("arbitrary"),
        name="ec_invert",
    )(starts, ends, pos.T)
    digits = acc[:, :2 * n_exp].astype(jnp.int32).reshape(cap_tot, n_exp, 2)
    return (digits[:, :, 0] * 256 + digits[:, :, 1]).T.reshape(-1)


SELECT_ROW_TILE = 512


def _select_kernel(aff_ref, pos_ref, cnt_ref, *, cap, n_exp):
    a = aff_ref[...]
    r = a.shape[0]
    bits = pltpu.bitcast(a, jnp.int32)

    def fold(v):
        sh = n_exp
        while sh < LANES:
            v = v + pltpu.roll(v, sh, 1)
            sh *= 2
        return v

    def count(mask):
        return fold(jnp.sum(jnp.where(mask, 1.0, 0.0), axis=0, keepdims=True))

    def search(i, thr):
        cand = thr | jnp.left_shift(jnp.int32(1), 30 - i)
        return jnp.where(count(bits >= cand) >= cap, cand, thr)

    thr = lax.fori_loop(0, 31, search, jnp.zeros((1, LANES), jnp.int32))
    above = bits > thr
    tied = bits == thr
    need = cap - count(above)

    li = lax.broadcasted_iota(jnp.int32, (LANES, 2 * LANES), 0)
    ci = lax.broadcasted_iota(jnp.int32, (LANES, 2 * LANES), 1)
    same_exp = (li & (n_exp - 1)) == (ci & (n_exp - 1))
    earlier = (li // n_exp) < ((ci & (LANES - 1)) // n_exp)
    w2 = jnp.where(same_exp & ((ci >= LANES) | earlier), 1.0, 0.0).astype(BF16)
    tr = min(SELECT_ROW_TILE, r)
    rr = lax.broadcasted_iota(jnp.int32, (tr, tr), 0)
    rc = lax.broadcasted_iota(jnp.int32, (tr, tr), 1)
    rows_before = jnp.where(rr > rc, 1.0, 0.0).astype(BF16)

    def prefix(mask):
        lw = jnp.dot(jnp.where(mask, 1.0, 0.0).astype(BF16), w2, preferred_element_type=F32)
        within, row_tot = lw[:, :LANES], lw[:, LANES:]
        carry = jnp.zeros((1, LANES), F32)
        outs = []
        for t in range(r // tr):
            rt = row_tot[t * tr:(t + 1) * tr]
            outs.append(jnp.dot(rows_before, rt.astype(BF16), preferred_element_type=F32) + carry
                        + within[t * tr:(t + 1) * tr])
            carry = carry + jnp.sum(rt, axis=0, keepdims=True)
        return jnp.concatenate(outs, axis=0)

    sel = above | (tied & (prefix(tied) < need))
    cnt = prefix(sel).astype(jnp.int32)
    pos_ref[...] = jnp.where(sel, cnt, -1)
    cnt_ref[...] = cnt


def _ec_select(aff_group, cap):
    n_g, n_exp = aff_group.shape
    assert LANES % n_exp == 0 and (n_exp & (n_exp - 1)) == 0
    r = n_g * n_exp // LANES
    assert r % min(SELECT_ROW_TILE, r) == 0
    shp = jax.ShapeDtypeStruct((r, LANES), jnp.int32)
    pos, cnt = pl.pallas_call(
        functools.partial(_select_kernel, cap=cap, n_exp=n_exp),
        out_shape=[shp, shp],
        compiler_params=pltpu.CompilerParams(vmem_limit_bytes=V7X_VMEM_LIMIT_BYTES),
        name="ec_select",
    )(aff_group.reshape(r, LANES))
    return pos.reshape(n_g, n_exp), cnt.reshape(n_g, n_exp)


COMBINE_TOKENS = 256
COMBINE_WINDOW = 64
BF16_ROWS = 16


def _combine_kernel(st_ref, en_ref, x_ref, pos_ref, aff_ref, g_ref, b_ref, o_hbm, *rest, alpha, n_exp, win, cap_tot,
                    n_tiles, split_tiles):
    n_out = 1 if split_tiles is None else 2
    out_refs = rest[:n_out]
    buf, sem, xbuf, xsem, acc_ref = rest[n_out:]
    i = pl.program_id(0)
    slot = i % 2

    def window(tile, e):
        s0 = st_ref[tile * n_exp + e]
        return jnp.minimum(lax.shift_left(lax.shift_right_logical(s0, 4), 4), cap_tot - win)

    def fetch(tile, sl, e):
        w = pl.multiple_of(window(tile, e), BF16_ROWS)
        return pltpu.make_async_copy(o_hbm.at[e, pl.ds(w, win), :], buf.at[sl, pl.ds(e * win, win), :], sem.at[sl, e])

    @pl.when(i == 0)
    def _():
        for e in range(n_exp):
            fetch(0, 0, e).start()

    @pl.when(i + 1 < n_tiles)
    def _():
        for e in range(n_exp):
            fetch(i + 1, 1 - slot, e).start()

    pos = pos_ref[...]
    aff = aff_ref[...]
    t = pos.shape[0]
    lane = lax.broadcasted_iota(jnp.int32, (t, 2 * win), 1)
    first = lane < win
    lane_in = jnp.where(first, lane, lane - win)
    parts = []
    for e in range(0, n_exp, 2):
        rel = jnp.where(first, pos[:, e:e + 1] - window(i, e), pos[:, e + 1:e + 2] - window(i, e + 1))
        gate = jnp.where(first, aff[:, e:e + 1], aff[:, e + 1:e + 2])
        parts.append(jnp.where(rel == lane_in, gate, 0.0))
    pmat = jnp.concatenate(parts, axis=1).astype(BF16)
    for e in range(n_exp):
        fetch(i, slot, e).wait()
    acc_ref[...] = jnp.dot(pmat, buf[slot], preferred_element_type=F32)

    lane1 = lax.broadcasted_iota(jnp.int32, (t, win), 1)
    for e in range(n_exp):
        w = window(i, e)
        s1 = en_ref[i * n_exp + e]
        n_extra = jnp.maximum(s1 - w - 1, 0) // win

        def extra(k, carry, e=e, w=w):
            lo = w + win * (k + 1)
            wk = pl.multiple_of(jnp.minimum(lo, cap_tot - win), BF16_ROWS)
            cp = pltpu.make_async_copy(o_hbm.at[e, pl.ds(wk, win), :], xbuf, xsem)
            cp.start()
            cp.wait()
            pe = pos[:, e:e + 1]
            oh = jnp.where((pe - wk == lane1) & (pe >= lo), 1.0, 0.0).astype(BF16)
            acc_ref[...] += jnp.dot(oh, xbuf[...], preferred_element_type=F32) * aff[:, e:e + 1]
            return carry

        lax.fori_loop(0, n_extra, extra, 0)
    y = _res_ln(x_ref[...], acc_ref[...], g_ref[...], b_ref[...], alpha)
    if split_tiles is None:
        out_refs[0][...] = y
    else:
        @pl.when(i < split_tiles)
        def _():
            out_refs[0][...] = y

        @pl.when(i >= split_tiles)
        def _():
            out_refs[1][...] = y


def _moe_combine_ln(x2d, pos, aff, starts, ends, o, g, b, alpha, split=None):
    n, d = x2d.shape
    n_exp, cap_tot, _ = o.shape
    t = COMBINE_TOKENS
    win = COMBINE_WINDOW
    assert n % t == 0 and cap_tot % BF16_ROWS == 0 and cap_tot >= win and n_exp % 2 == 0 and 2 * win == LANES
    n_tiles = n // t
    row = pl.BlockSpec((1, d), lambda i, st, en: (0, 0))
    tok = lambda w: pl.BlockSpec((t, w), lambda i, st, en: (i, 0))
    if split is None:
        split_tiles = None
        out_specs = tok(d)
        out_shape = jax.ShapeDtypeStruct((n, d), F32)
    else:
        assert split % t == 0 and 0 < split < n
        split_tiles = split // t
        out_specs = [pl.BlockSpec((t, d), lambda i, st, en: (jnp.minimum(i, split_tiles - 1), 0)),
                     pl.BlockSpec((t, d), lambda i, st, en: (jnp.maximum(i - split_tiles, 0), 0))]
        out_shape = [jax.ShapeDtypeStruct((split, d), F32), jax.ShapeDtypeStruct((n - split, d), F32)]
    kern = functools.partial(_combine_kernel, alpha=alpha, n_exp=n_exp, win=win, cap_tot=cap_tot, n_tiles=n_tiles,
                             split_tiles=split_tiles)
    grid_spec = pltpu.PrefetchScalarGridSpec(
        num_scalar_prefetch=2,
        grid=(n_tiles,),
        in_specs=[tok(d), tok(n_exp), tok(n_exp), row, row, pl.BlockSpec(memory_space=pl.ANY)],
        out_specs=out_specs,
        scratch_shapes=[pltpu.VMEM((2, n_exp * win, d), BF16), pltpu.SemaphoreType.DMA((2, n_exp)),
                        pltpu.VMEM((win, d), BF16), pltpu.SemaphoreType.DMA(()), pltpu.VMEM((t, d), F32)],
    )
    return pl.pallas_call(
        kern,
        grid_spec=grid_spec,
        out_shape=out_shape,
        compiler_params=_params("arbitrary"),
        name="moe_combine_ln",
    )(starts, ends, x2d, pos, aff, g.reshape(1, d), b.reshape(1, d), o)


def _ec_moe_ln(x2d, aff, groups, wg, wu, wd, layer, g, b, alpha, split=None):
    n, d = x2d.shape
    n_exp = aff.shape[1]
    t = COMBINE_TOKENS
    pos_l, st_l, en_l = [], [], []
    off = 0
    for start, cnt_tok in groups:
        assert start % t == 0 and cnt_tok % t == 0
        cap = EC_CAPACITY_FACTOR * cnt_tok // n_exp
        pos, cnt = _ec_select(aff[start:start + cnt_tok], cap)
        st = cnt[::t] + off
        en = jnp.concatenate([st[1:], jnp.full((1, n_exp), off + cap, jnp.int32)], axis=0)
        pos_l.append(jnp.where(pos >= 0, pos + off, -1))
        st_l.append(st)
        en_l.append(en)
        off += cap
    cap_tot = off
    pos = jnp.concatenate(pos_l, axis=0)
    starts = jnp.concatenate(st_l, axis=0).reshape(-1)
    ends = jnp.concatenate(en_l, axis=0).reshape(-1)
    idx = _ec_invert(pos, starts, ends, cap_tot)
    o = _moe_ffn(x2d, idx, wg, wu, wd, layer)
    return _moe_combine_ln(x2d, pos, aff, starts, ends, o, g, b, alpha, split)


def _split2_bf16(w):
    hi = w.astype(BF16)
    lo = (w - hi.astype(F32)).astype(BF16)
    return jnp.concatenate([hi, lo], axis=1)


def kernel(x_prompt, x_sample, attn_w_qkv, attn_q_norm, attn_k_norm, attn_w_o, pool_w, pool_scale, ssd_w_in,
           ssd_conv_w, ssd_conv_b, ssd_dt_bias, ssd_A_log, ssd_D, ssd_norm, ssd_w_out, moe_w_router, moe_w_gate,
           moe_w_up, moe_w_down, ln_g, ln_b):
    bp, s, d = x_prompt.shape
    bs = x_sample.shape[0]
    assert x_sample.shape[1] == s
    bsz = bp + bs
    n = bsz * s
    groups = [(0, bp * s), (bp * s, bs * s)]
    depth = ln_g.shape[0]
    alpha = (2 * depth) ** 0.25
    n_heads = attn_w_o.shape[1] // HEAD_DIM
    n_kv = (attn_w_qkv.shape[2] // HEAD_DIM - n_heads) // 2
    d_inner = ssd_w_out.shape[1]
    n_ssd_heads = ssd_A_log.shape[-1]
    assert d_inner == n_ssd_heads * SSD_HEAD_DIM
    conv_dim = ssd_conv_w.shape[2]
    n_groups = (conv_dim - d_inner) // (2 * D_STATE)
    rope = _rope_tables(s)

    x = jnp.concatenate([x_prompt, x_sample], axis=0).reshape(n, d)
    wg_all, wu_all, wd_all = moe_w_gate.astype(BF16), moe_w_up.astype(BF16), moe_w_down.astype(BF16)
    ia = ip = isd = 0
    for i in range(depth):
        wr2 = _split2_bf16(moe_w_router[i])
        g1, b1, g2, b2 = ln_g[i, 0], ln_b[i, 0], ln_g[i, 1], ln_b[i, 1]
        kind = i % 3
        if kind == 0:
            qkv = _qkv_proj(x, attn_w_qkv[ia].astype(BF16), attn_q_norm[ia], attn_k_norm[ia], rope, s, n_heads, n_kv)
            qkv3 = qkv.reshape(bsz, s, -1)
            vt = jnp.swapaxes(qkv3[:, :, (n_heads + n_kv) * HEAD_DIM:], 1, 2).reshape(bsz, n_kv, HEAD_DIM, s)
            o = _flash_attention(qkv3, vt, n_heads, n_kv)
            x, aff = _mm_res_ln_router(o.reshape(n, -1), attn_w_o[ia].astype(BF16), x, g1, b1, wr2, alpha)
            ia += 1
        elif kind == 1:
            x3, aff3 = _pool_layer(x.reshape(bsz, s, d), pool_w[ip].astype(BF16), pool_scale[ip], g1, b1, wr2, alpha)
            x, aff = x3.reshape(n, d), aff3.reshape(n, -1)
            ip += 1
        else:
            zx = _matmul_f32(x, ssd_w_in[isd].astype(BF16), 1152)
            zx3 = zx.reshape(bsz, s, -1)
            xbc = _ssd_conv(zx3, ssd_conv_w[isd], ssd_conv_b[isd], d_inner, conv_dim)
            a_neg = -jnp.exp(ssd_A_log[isd].astype(F32)).reshape(-1)
            dt, ecs, dt_t, ecs_t, tot = _ssd_dt(zx3, ssd_dt_bias[isd].reshape(-1), a_neg, d_inner + conv_dim)
            yf, yb = _ssd_scan(xbc, dt, ecs, dt_t, ecs_t, tot, ssd_D[isd], d_inner, n_groups)
            yn = _ssd_gate(yf.reshape(n, d_inner), yb.reshape(n, d_inner), zx, ssd_norm[isd])
            x, aff = _mm_res_ln_router(yn, ssd_w_out[isd].astype(BF16), x, g1, b1, wr2, alpha)
            isd += 1
        x = _ec_moe_ln(x, aff, groups, wg_all, wu_all, wd_all, i, g2, b2, alpha,
                       split=bp * s if i == depth - 1 else None)
    y_prompt, y_sample = x
    return y_prompt.reshape(bp, s, d), y_sample.reshape(bs, s, d)
```

```python
import functools
import math

import jax
import jax.numpy as jnp
from jax import lax
from jax.experimental import pallas as pl
from jax.experimental.pallas import tpu as pltpu

F32 = jnp.float32
BF16 = jnp.bfloat16

HEAD_DIM = 128
GRID_W = 64
ROPE_THETA = 10000.0
POOL_WINDOWS = (2, 4, 8, 16)
POOL_HALO = 8
D_STATE = 128
SSD_CHUNK = 128
SSD_HEAD_DIM = 64
SSD_HEADS_PER_GROUP = 8
D_CONV = 4
CONV_LEFT = D_CONV // 2
EC_CAPACITY_FACTOR = 2
LN_EPS = 1e-5
RMS_EPS = 1e-6
LOG2E = 1.4426950408889634

V7X_VMEM_LIMIT_BYTES = 52 * 1024 * 1024
LANES = 128


def _params(*sem):
    return pltpu.CompilerParams(dimension_semantics=sem, vmem_limit_bytes=V7X_VMEM_LIMIT_BYTES)


def _pick(n, pref):
    t = min(n, pref)
    while n % t:
        t //= 2
    return t


def _res_ln(x, h, g, b, alpha):
    y = alpha * x + h
    mu = jnp.mean(y, axis=-1, keepdims=True)
    yc = y - mu
    var = jnp.mean(yc * yc, axis=-1, keepdims=True)
    return yc * lax.rsqrt(var + LN_EPS) * g + b


def _router_affinity(xn, wr_ref, n_exp):
    xh = xn.astype(BF16)
    xl = (xn - xh.astype(F32)).astype(BF16)
    wr = wr_ref[...]
    r1 = jnp.dot(xh, wr, preferred_element_type=F32)
    r2 = jnp.dot(xl, wr[:, :n_exp], preferred_element_type=F32)
    logits = r1[:, :n_exp] + (r1[:, n_exp:] + r2)
    m = jnp.max(logits, axis=-1, keepdims=True)
    e = jnp.exp(logits - m)
    return e / jnp.sum(e, axis=-1, keepdims=True)


def _qkv_kernel(x_ref, w_ref, cos_ref, sin_ref, qn_ref, kn_ref, o_ref, xb_ref, *,
                n_q_tiles, n_k_tiles, heads_per_tile, q_scale):
    j = pl.program_id(1)

    @pl.when(j == 0)
    def _():
        xb_ref[...] = x_ref[...].astype(BF16)

    acc = jnp.dot(xb_ref[...], w_ref[...], preferred_element_type=F32)

    def norm_rope(gain_ref, scale):
        cos = cos_ref[...]
        sin = sin_ref[...]
        g = gain_ref[...]
        for h in range(heads_per_tile):
            a = acc[:, h * HEAD_DIM:(h + 1) * HEAD_DIM]
            a = a * lax.rsqrt(jnp.mean(a * a, axis=-1, keepdims=True) + RMS_EPS) * g
            r = a * cos + pltpu.roll(a, HEAD_DIM // 2, 1) * sin
            o_ref[:, h * HEAD_DIM:(h + 1) * HEAD_DIM] = (r * scale).astype(BF16)

    @pl.when(j < n_q_tiles)
    def _():
        norm_rope(qn_ref, q_scale)

    @pl.when((j >= n_q_tiles) & (j < n_q_tiles + n_k_tiles))
    def _():
        norm_rope(kn_ref, 1.0)

    @pl.when(j >= n_q_tiles + n_k_tiles)
    def _():
        o_ref[...] = acc.astype(BF16)


def _rope_perm():
    quarter = HEAD_DIM // 4
    blocks = (0, 2, 1, 3)
    return jnp.concatenate([jnp.arange(quarter) + b * quarter for b in blocks])


def _rope_tables(seq_len):
    rows = seq_len // GRID_W
    row = jnp.repeat(jnp.arange(rows, dtype=F32), GRID_W)
    col = jnp.tile(jnp.arange(GRID_W, dtype=F32), rows)
    inv_freq = ROPE_THETA ** (-jnp.arange(0, HEAD_DIM // 2, 2, dtype=F32) / (HEAD_DIM // 2))
    ang = jnp.concatenate([row[:, None] * inv_freq, col[:, None] * inv_freq], axis=-1)
    cos = jnp.concatenate([jnp.cos(ang), jnp.cos(ang)], axis=-1)
    sin = jnp.concatenate([-jnp.sin(ang), jnp.sin(ang)], axis=-1)
    return cos, sin


def _permute_qk_columns(w_qkv, n_heads, n_kv):
    d = w_qkv.shape[0]
    n_qk = n_heads + n_kv
    qk = w_qkv[:, :n_qk * HEAD_DIM].reshape(d, n_qk, HEAD_DIM)[:, :, _rope_perm()].reshape(d, n_qk * HEAD_DIM)
    return jnp.concatenate([qk, w_qkv[:, n_qk * HEAD_DIM:]], axis=1)


def _qkv_proj(x2d, w_bf16, q_norm, k_norm, rope, seq_len, n_heads, n_kv):
    n, d = x2d.shape
    qkv_dim = w_bf16.shape[1]
    tn = n_kv * HEAD_DIM
    tm = _pick(seq_len, 512)
    cos, sin = rope
    perm = _rope_perm()
    q_norm, k_norm = q_norm[perm], k_norm[perm]
    nsb = seq_len // tm
    kern = functools.partial(
        _qkv_kernel, n_q_tiles=n_heads // n_kv, n_k_tiles=1, heads_per_tile=n_kv,
        q_scale=HEAD_DIM ** -0.5 * LOG2E)
    tab = pl.BlockSpec((tm, HEAD_DIM), lambda i, j: (i % nsb, 0))
    vec = pl.BlockSpec((1, HEAD_DIM), lambda i, j: (0, 0))
    return pl.pallas_call(
        kern,
        grid=(n // tm, qkv_dim // tn),
        in_specs=[pl.BlockSpec((tm, d), lambda i, j: (i, 0)),
                  pl.BlockSpec((d, tn), lambda i, j: (0, j)),
                  tab, tab, vec, vec],
        out_specs=pl.BlockSpec((tm, tn), lambda i, j: (i, j)),
        out_shape=jax.ShapeDtypeStruct((n, qkv_dim), BF16),
        scratch_shapes=[pltpu.VMEM((tm, d), BF16)],
        compiler_params=_params("parallel", "arbitrary"),
        name="qkv_proj",
    )(x2d, w_bf16, cos, sin, q_norm.reshape(1, HEAD_DIM), k_norm.reshape(1, HEAD_DIM))


FLASH_TQ = 128
FLASH_TK = 1024


def _flash_kernel(q_ref, k_ref, vt_ref, o_ref, s_scr, p_scr, acc_scr, *, tk, group):
    tq = q_ref.shape[1]
    seq = k_ref.shape[1]
    q = jnp.concatenate([q_ref[0, :, g * HEAD_DIM:(g + 1) * HEAD_DIM] for g in range(group)], axis=0)
    rows = group * tq
    nc = seq // tk

    def scores(c, slot):
        k = k_ref[0, pl.ds(c * tk, tk), :]
        s_scr[slot] = lax.dot_general(k, q, (((1,), (1,)), ((), ())), preferred_element_type=F32)

    def pv(c, slot, alpha):
        vt = vt_ref[0, 0, :, pl.ds(c * tk, tk)]
        acc_scr[...] = acc_scr[...] * alpha + jnp.dot(vt, p_scr[slot], preferred_element_type=F32)

    def softmax(slot, m, l):
        s = s_scr[slot]
        m_new = jnp.maximum(m, jnp.max(s, axis=0, keepdims=True))
        alpha = jnp.exp2(m - m_new)
        p = jnp.exp2(s - m_new)
        l = alpha * l + jnp.sum(p, axis=0, keepdims=True)
        p_scr[slot] = p.astype(BF16)
        return m_new, l, alpha

    m = jnp.full((1, rows), -jnp.inf, F32)
    l = jnp.zeros((1, rows), F32)
    acc_scr[...] = jnp.zeros_like(acc_scr)
    scores(0, 0)
    if nc > 1:
        scores(1, 1)
    m, l, alpha = softmax(0, m, l)
    for c in range(1, nc):
        if c + 1 < nc:
            scores(c + 1, (c + 1) % 2)
        pv(c - 1, (c - 1) % 2, alpha)
        m, l, alpha = softmax(c % 2, m, l)
    pv(nc - 1, (nc - 1) % 2, alpha)
    o = (acc_scr[...] / l).T
    for g in range(group):
        o_ref[0, :, g * HEAD_DIM:(g + 1) * HEAD_DIM] = o[g * tq:(g + 1) * tq].astype(BF16)


def _flash_attention(qkv, vt, n_heads, n_kv):
    b, s, _ = qkv.shape
    group = n_heads // n_kv
    tq = _pick(s, FLASH_TQ)
    tk = _pick(s, FLASH_TK)
    gw = group * HEAD_DIM
    rows = group * tq
    vrows = vt.shape[2]
    kern = functools.partial(_flash_kernel, tk=tk, group=group)
    return pl.pallas_call(
        kern,
        grid=(b, n_kv, s // tq),
        in_specs=[pl.BlockSpec((1, tq, gw), lambda bi, h, i: (bi, i, h)),
                  pl.BlockSpec((1, s, HEAD_DIM), lambda bi, h, i: (bi, 0, n_heads + h)),
                  pl.BlockSpec((1, 1, vrows, s), lambda bi, h, i: (bi, h, 0, 0))],
        out_specs=pl.BlockSpec((1, tq, gw), lambda bi, h, i: (bi, i, h)),
        out_shape=jax.ShapeDtypeStruct((b, s, n_heads * HEAD_DIM), BF16),
        scratch_shapes=[pltpu.VMEM((2, tk, rows), F32), pltpu.VMEM((2, tk, rows), BF16),
                        pltpu.VMEM((vrows, rows), F32)],
        compiler_params=_params("parallel", "parallel", "arbitrary"),
        name="flash_attention",
    )(qkv, qkv, vt)


MM_LN_TM = 512
MM_LN_TK = 2048


def _mm_ln_kernel(a_ref, w_ref, x_ref, g_ref, b_ref, wr_ref, o_ref, aff_ref, acc_ref, *, alpha, nk, n_exp):
    k = pl.program_id(1)
    part = jnp.dot(a_ref[...], w_ref[...], preferred_element_type=F32)

    def finish(h):
        xn = _res_ln(x_ref[...], h, g_ref[...], b_ref[...], alpha)
        o_ref[...] = xn
        aff_ref[...] = _router_affinity(xn, wr_ref, n_exp)

    if nk == 1:
        finish(part)
        return

    @pl.when(k == 0)
    def _():
        acc_ref[...] = part

    @pl.when((k > 0) & (k < nk - 1))
    def _():
        acc_ref[...] += part

    @pl.when(k == nk - 1)
    def _():
        finish(acc_ref[...] + part)


def _mm_res_ln_router(a_bf16, w_bf16, x2d, g, b, wr2, alpha):
    n, kdim = a_bf16.shape
    d = w_bf16.shape[1]
    n_exp = wr2.shape[1] // 2
    tm = _pick(n, MM_LN_TM)
    tk = _pick(kdim, MM_LN_TK)
    nk = kdim // tk
    kern = functools.partial(_mm_ln_kernel, alpha=alpha, nk=nk, n_exp=n_exp)
    row = pl.BlockSpec((1, d), lambda i, k: (0, 0))
    return pl.pallas_call(
        kern,
        grid=(n // tm, nk),
        in_specs=[pl.BlockSpec((tm, tk), lambda i, k: (i, k)),
                  pl.BlockSpec((tk, d), lambda i, k: (k, 0)),
                  pl.BlockSpec((tm, d), lambda i, k: (i, 0)),
                  row, row,
                  pl.BlockSpec((d, 2 * n_exp), lambda i, k: (0, 0))],
        out_specs=[pl.BlockSpec((tm, d), lambda i, k: (i, 0)),
                   pl.BlockSpec((tm, n_exp), lambda i, k: (i, 0))],
        out_shape=[jax.ShapeDtypeStruct((n, d), F32), jax.ShapeDtypeStruct((n, n_exp), F32)],
        scratch_shapes=[pltpu.VMEM((tm, d), F32)],
        compiler_params=_params("parallel", "arbitrary"),
        name="mm_res_ln_router",
    )(a_bf16, w_bf16, x2d, g.reshape(1, d), b.reshape(1, d), wr2)


def _pool_kernel(prev_ref, cur_ref, next_ref, w_ref, sc_ref, g_ref, b_ref, wr_ref, o_ref, aff_ref, ext_ref, *,
                 alpha, nt, seq_len, n_exp):
    i = pl.program_id(1)
    t = cur_ref.shape[1]
    d = cur_ref.shape[2]
    pg = d // len(POOL_WINDOWS)
    x = cur_ref[0]
    ext_ref[0:POOL_HALO, :] = jnp.where(i == 0, 0.0, prev_ref[0])
    ext_ref[POOL_HALO:POOL_HALO + t, :] = x
    ext_ref[POOL_HALO + t:2 * POOL_HALO + t, :] = jnp.where(i == nt - 1, 0.0, next_ref[0])
    pos = i * t + lax.broadcasted_iota(jnp.int32, (t, 1), 0)
    hs = []
    for gi, w in enumerate(POOL_WINDOWS):
        half = w // 2
        cols = slice(gi * pg, (gi + 1) * pg)
        acc = ext_ref[pl.ds(POOL_HALO - half, t), cols]
        for jj in range(1, w):
            acc = acc + ext_ref[pl.ds(POOL_HALO - half + jj, t), cols]
        cnt = (jnp.minimum(pos + half, seq_len) - jnp.maximum(pos - half, 0)).astype(F32)
        mixed = (acc / cnt - x[:, cols]).astype(BF16)
        hs.append(jnp.dot(mixed, w_ref[gi], preferred_element_type=F32))
    h = jnp.concatenate(hs, axis=-1) * sc_ref[...]
    xn = _res_ln(x, h, g_ref[...], b_ref[...], alpha)
    o_ref[0] = xn
    aff_ref[0] = _router_affinity(xn, wr_ref, n_exp)


def _pool_layer(x3d, w_bf16, scale, g, b, wr2, alpha):
    bsz, s, d = x3d.shape
    n_exp = wr2.shape[1] // 2
    t = _pick(s, 256)
    nt = s // t
    hb = t // POOL_HALO
    last_hb = s // POOL_HALO - 1
    pg = d // len(POOL_WINDOWS)
    kern = functools.partial(_pool_kernel, alpha=alpha, nt=nt, seq_len=s, n_exp=n_exp)
    row = pl.BlockSpec((1, d), lambda bi, i: (0, 0))
    return pl.pallas_call(
        kern,
        grid=(bsz, nt),
        in_specs=[pl.BlockSpec((1, POOL_HALO, d), lambda bi, i: (bi, jnp.maximum(i * hb - 1, 0), 0)),
                  pl.BlockSpec((1, t, d), lambda bi, i: (bi, i, 0)),
                  pl.BlockSpec((1, POOL_HALO, d), lambda bi, i: (bi, jnp.minimum((i + 1) * hb, last_hb), 0)),
                  pl.BlockSpec((len(POOL_WINDOWS), pg, pg), lambda bi, i: (0, 0, 0)),
                  row, row, row,
                  pl.BlockSpec((d, 2 * n_exp), lambda bi, i: (0, 0))],
        out_specs=[pl.BlockSpec((1, t, d), lambda bi, i: (bi, i, 0)),
                   pl.BlockSpec((1, t, n_exp), lambda bi, i: (bi, i, 0))],
        out_shape=[jax.ShapeDtypeStruct((bsz, s, d), F32), jax.ShapeDtypeStruct((bsz, s, n_exp), F32)],
        scratch_shapes=[pltpu.VMEM((t + 2 * POOL_HALO, d), F32)],
        compiler_params=_params("parallel", "parallel"),
        name="pool_mixer",
    )(x3d, x3d, x3d, w_bf16, scale.reshape(1, d), g.reshape(1, d), b.reshape(1, d), wr2)


def _mm_kernel(x_ref, w_ref, o_ref, xb_ref):
    @pl.when(pl.program_id(1) == 0)
    def _():
        xb_ref[...] = x_ref[...].astype(BF16)

    o_ref[...] = jnp.dot(xb_ref[...], w_ref[...], preferred_element_type=F32)


def _matmul_f32(x2d, w_bf16, tn_pref):
    n, d = x2d.shape
    nout = w_bf16.shape[1]
    tm = _pick(n, 1024)
    tn = tn_pref
    assert nout % tn == 0
    return pl.pallas_call(
        _mm_kernel,
        grid=(n // tm, nout // tn),
        in_specs=[pl.BlockSpec((tm, d), lambda i, j: (i, 0)),
                  pl.BlockSpec((d, tn), lambda i, j: (0, j))],
        out_specs=pl.BlockSpec((tm, tn), lambda i, j: (i, j)),
        out_shape=jax.ShapeDtypeStruct((n, nout), F32),
        scratch_shapes=[pltpu.VMEM((tm, d), BF16)],
        compiler_params=_params("parallel", "arbitrary"),
        name="ssd_in_proj",
    )(x2d, w_bf16)


def _conv_kernel(prev_ref, cur_ref, next_ref, w_ref, b_ref, o_ref, ext_ref, *, nt):
    i = pl.program_id(1)
    t = cur_ref.shape[1]
    ext_ref[0:POOL_HALO, :] = jnp.where(i == 0, 0.0, prev_ref[0])
    ext_ref[POOL_HALO:POOL_HALO + t, :] = cur_ref[0]
    ext_ref[POOL_HALO + t:2 * POOL_HALO + t, :] = jnp.where(i == nt - 1, 0.0, next_ref[0])
    acc = ext_ref[pl.ds(POOL_HALO - CONV_LEFT, t), :] * w_ref[0:1, :]
    for kk in range(1, D_CONV):
        acc = acc + ext_ref[pl.ds(POOL_HALO - CONV_LEFT + kk, t), :] * w_ref[kk:kk + 1, :]
    acc = acc + b_ref[...]
    o_ref[0] = acc / (1.0 + jnp.exp(-acc))


def _ssd_conv(zx3d, conv_w, conv_b, d_inner, conv_dim):
    bsz, s, _ = zx3d.shape
    tc = 512
    t = _pick(s, 512)
    nt = s // t
    hb = t // POOL_HALO
    last_hb = s // POOL_HALO - 1
    c0 = d_inner // tc
    return pl.pallas_call(
        functools.partial(_conv_kernel, nt=nt),
        grid=(bsz, nt, conv_dim // tc),
        in_specs=[pl.BlockSpec((1, POOL_HALO, tc), lambda bi, i, j: (bi, jnp.maximum(i * hb - 1, 0), c0 + j)),
                  pl.BlockSpec((1, t, tc), lambda bi, i, j: (bi, i, c0 + j)),
                  pl.BlockSpec((1, POOL_HALO, tc), lambda bi, i, j: (bi, jnp.minimum((i + 1) * hb, last_hb), c0 + j)),
                  pl.BlockSpec((D_CONV, tc), lambda bi, i, j: (0, j)),
                  pl.BlockSpec((1, tc), lambda bi, i, j: (0, j))],
        out_specs=pl.BlockSpec((1, t, tc), lambda bi, i, j: (bi, i, j)),
        out_shape=jax.ShapeDtypeStruct((bsz, s, conv_dim), F32),
        scratch_shapes=[pltpu.VMEM((t + 2 * POOL_HALO, tc), F32)],
        compiler_params=_params("parallel", "parallel", "parallel"),
        name="ssd_conv",
    )(zx3d, zx3d, zx3d, conv_w, conv_b.reshape(1, conv_dim))


def _split3(x):
    hi = x.astype(BF16)
    r1 = x - hi.astype(F32)
    mid = r1.astype(BF16)
    lo = (r1 - mid.astype(F32)).astype(BF16)
    return hi, mid, lo


def _dt_kernel(raw_ref, bias_ref, a_ref, dt_ref, e_ref, dtt_ref, et_ref, tot_ref):
    v = raw_ref[0] + bias_ref[...]
    dt = jnp.maximum(v, 0.0) + jnp.log1p(jnp.exp(-jnp.abs(v)))
    dt_ref[0] = dt
    dtt_ref[0] = dt.T
    a = dt * a_ref[...]
    q, w = a.shape
    li = lax.broadcasted_iota(jnp.int32, (q, q), 0)
    si = lax.broadcasted_iota(jnp.int32, (q, q), 1)
    tri = jnp.where(li >= si, 1.0, 0.0).astype(BF16)
    hi, mid, lo = _split3(a)
    cs = (jnp.dot(tri, lo, preferred_element_type=F32) + jnp.dot(tri, mid, preferred_element_type=F32)
          + jnp.dot(tri, hi, preferred_element_type=F32))
    lane = lax.broadcasted_iota(jnp.int32, (q, w), 1)
    e = jnp.where(lane < w // 2, cs, cs - a)
    e_ref[0] = e
    et_ref[0] = e.T
    tot_ref[0, 0] = cs[q - 1:q, :]


def _ssd_dt(zx3d, dt_bias, a_neg, col0):
    bsz, s, _ = zx3d.shape
    w = dt_bias.shape[-1]
    assert w == LANES and col0 % LANES == 0 and SSD_CHUNK == LANES
    nc = s // SSD_CHUNK
    blk = pl.BlockSpec((1, SSD_CHUNK, w), lambda bi, c: (bi, c, 0))
    blk_t = pl.BlockSpec((1, w, SSD_CHUNK), lambda bi, c: (bi, 0, c))
    row = pl.BlockSpec((1, w), lambda bi, c: (0, 0))
    return pl.pallas_call(
        _dt_kernel,
        grid=(bsz, nc),
        in_specs=[pl.BlockSpec((1, SSD_CHUNK, w), lambda bi, c: (bi, c, col0 // LANES)), row, row],
        out_specs=[blk, blk, blk_t, blk_t, pl.BlockSpec((1, 1, 1, w), lambda bi, c: (bi, c, 0, 0))],
        out_shape=[jax.ShapeDtypeStruct((bsz, s, w), F32), jax.ShapeDtypeStruct((bsz, s, w), F32),
                   jax.ShapeDtypeStruct((bsz, w, s), F32), jax.ShapeDtypeStruct((bsz, w, s), F32),
                   jax.ShapeDtypeStruct((bsz, nc, 1, w), F32)],
        compiler_params=_params("parallel", "parallel"),
        name="ssd_dt",
    )(zx3d, dt_bias.reshape(1, w), a_neg.reshape(1, w))


def _expand_heads(v, j0, width):
    m = v.shape[0]
    lane = lax.broadcasted_iota(jnp.int32, (m, LANES), 1)
    parts = []
    for pr in range(width // LANES):
        j = j0 + 2 * pr
        parts.append(jnp.where(lane < SSD_HEAD_DIM, v[:, j:j + 1], v[:, j + 1:j + 2]))
    return jnp.concatenate(parts, axis=1)


def _ssd_direction(x, bmat, cmat, dt_r, e_c, e_r, out_dec, st_w, chunk_dec, j0, st_ref, forward):
    q, width = x.shape
    li = lax.broadcasted_iota(jnp.int32, (q, q), 0)
    si = lax.broadcasted_iota(jnp.int32, (q, q), 1)
    lane = lax.broadcasted_iota(jnp.int32, (q, LANES), 1)
    mask = (li >= si) if forward else (si >= li)
    cb = lax.dot_general(cmat.astype(BF16), bmat.astype(BF16), (((1,), (1,)), ((), ())),
                         preferred_element_type=F32)
    xb = x.astype(BF16)
    y_parts = []
    for pr in range(width // LANES):
        ms = []
        for j in (j0 + 2 * pr, j0 + 2 * pr + 1):
            if forward:
                diff = e_c[:, j:j + 1] - e_r[j:j + 1, :]
            else:
                diff = e_r[j:j + 1, :] - e_c[:, j:j + 1]
            decay = jnp.exp(jnp.where(mask, diff, -jnp.inf))
            ms.append((decay * cb * dt_r[j:j + 1, :]).astype(BF16))
        xp = xb[:, pr * LANES:(pr + 1) * LANES]
        zero = jnp.zeros_like(xp)
        rhs = jnp.concatenate([jnp.where(lane < SSD_HEAD_DIM, xp, zero),
                               jnp.where(lane >= SSD_HEAD_DIM, xp, zero)], axis=0)
        y_parts.append(jnp.dot(jnp.concatenate(ms, axis=1), rhs, preferred_element_type=F32))
    y = jnp.concatenate(y_parts, axis=1)
    st = st_ref[...]
    y = y + jnp.dot(cmat.astype(BF16), st.astype(BF16), preferred_element_type=F32) * _expand_heads(out_dec, j0, width)
    xd = (x * _expand_heads(st_w, j0, width)).astype(BF16)
    st_new = lax.dot_general(bmat.astype(BF16), xd, (((0,), (0,)), ((), ())), preferred_element_type=F32)
    st_ref[...] = st * _expand_heads(chunk_dec, j0, width)[0:1, :] + st_new
    return y


def _ssd_scan_kernel(xf_ref, bf_ref, cf_ref, dtf_ref, ef_ref, dttf_ref, etf_ref, tf_ref,
                     xr_ref, br_ref, cr_ref, dtr_ref, er_ref, dttr_ref, etr_ref, tr_ref,
                     dskip_ref, yf_ref, yb_ref, stf_ref, stb_ref, *, n_groups, gw):
    @pl.when(pl.program_id(1) == 0)
    def _():
        stf_ref[...] = jnp.zeros_like(stf_ref)
        stb_ref[...] = jnp.zeros_like(stb_ref)

    hg = SSD_HEADS_PER_GROUP
    ef, tf = ef_ref[0], tf_ref[0, 0]
    out_dec_f = jnp.exp(ef)
    st_w_f = dtf_ref[0] * jnp.exp(tf - ef)
    chunk_dec_f = jnp.broadcast_to(jnp.exp(tf), (8, LANES))
    er, tr = er_ref[0], tr_ref[0, 0]
    out_dec_r = jnp.exp(tr - er)
    st_w_r = dtr_ref[0] * jnp.exp(er)
    chunk_dec_r = jnp.broadcast_to(jnp.exp(tr), (8, LANES))
    dttf, etf, dttr, etr = dttf_ref[0], etf_ref[0], dttr_ref[0], etr_ref[0]
    for gi in range(n_groups):
        xs = slice(gi * gw, (gi + 1) * gw)
        ns = slice(gi * D_STATE, (gi + 1) * D_STATE)
        xf = xf_ref[0, :, xs]
        yf = _ssd_direction(xf, bf_ref[0, :, ns], cf_ref[0, :, ns], dttf, ef, etf, out_dec_f, st_w_f, chunk_dec_f,
                            gi * hg, stf_ref.at[gi], True)
        yf_ref[0, :, xs] = yf + dskip_ref[:, xs] * xf
        yb_ref[0, :, xs] = _ssd_direction(xr_ref[0, :, xs], br_ref[0, :, ns], cr_ref[0, :, ns], dttr, er, etr,
                                          out_dec_r, st_w_r, chunk_dec_r, (n_groups + gi) * hg, stb_ref.at[gi],
                                          False)


def _ssd_scan(xbc, dt, ecs, dt_t, ecs_t, tot, d_skip, d_inner, n_groups):
    bsz, s, _ = xbc.shape
    nc = s // SSD_CHUNK
    gw = SSD_HEADS_PER_GROUP * SSD_HEAD_DIM
    w = dt.shape[-1]
    assert gw % LANES == 0 and d_inner == n_groups * gw and w == 2 * n_groups * SSD_HEADS_PER_GROUP == LANES
    gn = n_groups * D_STATE
    assert d_inner % gn == 0
    b0 = d_inner // gn
    dskip = jnp.repeat(d_skip.astype(F32), SSD_HEAD_DIM).reshape(1, d_inner)

    def specs(cidx):
        return [
            pl.BlockSpec((1, SSD_CHUNK, d_inner), lambda b, c: (b, cidx(c), 0)),
            pl.BlockSpec((1, SSD_CHUNK, gn), lambda b, c: (b, cidx(c), b0)),
            pl.BlockSpec((1, SSD_CHUNK, gn), lambda b, c: (b, cidx(c), b0 + 1)),
            pl.BlockSpec((1, SSD_CHUNK, w), lambda b, c: (b, cidx(c), 0)),
            pl.BlockSpec((1, SSD_CHUNK, w), lambda b, c: (b, cidx(c), 0)),
            pl.BlockSpec((1, w, SSD_CHUNK), lambda b, c: (b, 0, cidx(c))),
            pl.BlockSpec((1, w, SSD_CHUNK), lambda b, c: (b, 0, cidx(c))),
            pl.BlockSpec((1, 1, 1, w), lambda b, c: (b, cidx(c), 0, 0)),
        ]

    fwd = lambda c: c
    bwd = lambda c: nc - 1 - c
    y_shape = jax.ShapeDtypeStruct((bsz, s, d_inner), F32)
    return pl.pallas_call(
        functools.partial(_ssd_scan_kernel, n_groups=n_groups, gw=gw),
        grid=(bsz, nc),
        in_specs=specs(fwd) + specs(bwd) + [pl.BlockSpec((1, d_inner), lambda b, c: (0, 0))],
        out_specs=[pl.BlockSpec((1, SSD_CHUNK, d_inner), lambda b, c: (b, c, 0)),
                   pl.BlockSpec((1, SSD_CHUNK, d_inner), lambda b, c: (b, nc - 1 - c, 0))],
        out_shape=[y_shape, y_shape],
        scratch_shapes=[pltpu.VMEM((n_groups, D_STATE, gw), F32), pltpu.VMEM((n_groups, D_STATE, gw), F32)],
        compiler_params=_params("parallel", "arbitrary"),
        name="ssd_scan",
    )(xbc, xbc, xbc, dt, ecs, dt_t, ecs_t, tot,
      xbc, xbc, xbc, dt, ecs, dt_t, ecs_t, tot, dskip)


def _gate_kernel(yf_ref, yb_ref, z_ref, nw_ref, o_ref):
    z = z_ref[...]
    y = (yf_ref[...] + yb_ref[...]) * (z / (1.0 + jnp.exp(-z)))
    y = y * lax.rsqrt(jnp.mean(y * y, axis=-1, keepdims=True) + RMS_EPS) * nw_ref[...]
    o_ref[...] = y.astype(BF16)


def _ssd_gate(yf2d, yb2d, zx2d, norm_w):
    n, d_inner = yf2d.shape
    tm = _pick(n, 256)
    blk = pl.BlockSpec((tm, d_inner), lambda i: (i, 0))
    return pl.pallas_call(
        _gate_kernel,
        grid=(n // tm,),
        in_specs=[blk, blk, blk, pl.BlockSpec((1, d_inner), lambda i: (0, 0))],
        out_specs=blk,
        out_shape=jax.ShapeDtypeStruct((n, d_inner), BF16),
        compiler_params=_params("parallel"),
        name="ssd_gate_norm",
    )(yf2d, yb2d, zx2d, norm_w.reshape(1, d_inner))


FFN_TILE = 256


def _ffn_kernel(idx_ref, x_hbm, wg_ref, wu_ref, wd_ref, o_ref, xa, xb, sem, *, tile, n_steps):
    step = pl.program_id(0) * pl.num_programs(1) + pl.program_id(1)

    def issue(tile_idx, buf, s):
        base = tile_idx * tile
        for r in range(tile):
            tok = idx_ref[base + r]
            pltpu.make_async_copy(x_hbm.at[pl.ds(tok, 1), :], buf.at[pl.ds(r, 1), :], sem.at[s]).start(priority=r % 2)

    def wait(buf, s):
        pltpu.make_async_copy(x_hbm.at[pl.ds(0, tile), :], buf, sem.at[s]).wait()

    def ffn(buf, half):
        xs = buf[...].astype(BF16)
        hg = jnp.dot(xs, wg_ref[0, 0], preferred_element_type=F32)
        hu = jnp.dot(xs, wu_ref[0, 0], preferred_element_type=F32)
        h = (hg / (1.0 + jnp.exp(-hg)) * hu).astype(BF16)
        o_ref[0, half * tile:(half + 1) * tile, :] = jnp.dot(h, wd_ref[0, 0], preferred_element_type=F32).astype(BF16)

    @pl.when(step == 0)
    def _():
        issue(0, xa, 0)

    wait(xa, 0)
    issue(2 * step + 1, xb, 1)
    ffn(xa, 0)
    nxt = jnp.where(step + 1 < n_steps, 2 * step + 2, 0)
    issue(nxt, xa, 0)
    wait(xb, 1)
    ffn(xb, 1)

    @pl.when(step == n_steps - 1)
    def _():
        wait(xa, 0)


def _moe_ffn(x2d, idx, wg, wu, wd, layer):
    n, d = x2d.shape
    _, n_exp, _, f = wg.shape
    cap = idx.shape[0] // n_exp
    tile = _pick(cap // 2, FFN_TILE)
    steps_per_exp = cap // (2 * tile)
    kern = functools.partial(_ffn_kernel, tile=tile, n_steps=n_exp * steps_per_exp)
    grid_spec = pltpu.PrefetchScalarGridSpec(
        num_scalar_prefetch=1,
        grid=(n_exp, steps_per_exp),
        in_specs=[pl.BlockSpec(memory_space=pl.ANY),
                  pl.BlockSpec((1, 1, d, f), lambda e, t, ix: (layer, e, 0, 0)),
                  pl.BlockSpec((1, 1, d, f), lambda e, t, ix: (layer, e, 0, 0)),
                  pl.BlockSpec((1, 1, f, d), lambda e, t, ix: (layer, e, 0, 0))],
        out_specs=pl.BlockSpec((1, 2 * tile, d), lambda e, t, ix: (e, t, 0)),
        scratch_shapes=[pltpu.VMEM((tile, d), F32), pltpu.VMEM((tile, d), F32), pltpu.SemaphoreType.DMA((2,))],
    )
    return pl.pallas_call(
        kern,
        grid_spec=grid_spec,
        out_shape=jax.ShapeDtypeStruct((n_exp, cap, d), BF16),
        compiler_params=_params("arbitrary", "arbitrary"),
        name="moe_ffn",
    )(idx, x2d, wg, wu, wd)


def _invert_kernel(st_ref, en_ref, post_ref, acc_ref, *, n_exp, win, cap_tot, tokens):
    i = pl.program_id(0)

    @pl.when(i == 0)
    def _():
        acc_ref[...] = jnp.zeros_like(acc_ref)

    t = tokens
    tok = i * t + lax.broadcasted_iota(jnp.int32, (t, LANES), 0)
    lane = lax.broadcasted_iota(jnp.int32, (t, LANES), 1)
    digits = jnp.where(lane == 0, lax.shift_right_logical(tok, 8), jnp.where(lane == 1, tok & 255, 0))
    digits = digits.astype(F32).astype(BF16)
    post = post_ref[...]
    row = lax.broadcasted_iota(jnp.int32, (win, t), 0)

    def window(e):
        s0 = st_ref[i * n_exp + e]
        return jnp.minimum(lax.shift_left(lax.shift_right_logical(s0, 4), 4), cap_tot - win)

    def place(e, w, r):
        w = pl.multiple_of(w, BF16_ROWS)
        acc_ref[pl.ds(w, win), :] += r if e == 0 else pltpu.roll(r, 2 * e, 1)

    ws = [window(e) for e in range(n_exp)]
    onehots = jnp.concatenate([jnp.where(post[e:e + 1, :] - ws[e] == row, 1.0, 0.0).astype(BF16)
                               for e in range(n_exp)], axis=0)
    res = jnp.dot(onehots, digits, preferred_element_type=F32)
    for e in range(n_exp):
        place(e, ws[e], res[e * win:(e + 1) * win])

    for e in range(n_exp):
        w = ws[e]
        pe = post[e:e + 1, :]
        s1 = en_ref[i * n_exp + e]
        n_extra = jnp.maximum(s1 - w - 1, 0) // win

        def extra(k, carry, e=e, w=w, pe=pe):
            lo = w + win * (k + 1)
            wk = jnp.minimum(lo, cap_tot - win)
            oh = jnp.where((pe - wk == row) & (pe >= lo), 1.0, 0.0).astype(BF16)
            place(e, wk, jnp.dot(oh, digits, preferred_element_type=F32))
            return carry

        lax.fori_loop(0, n_extra, extra, 0)


def _ec_invert(pos, starts, ends, cap_tot):
    n, n_exp = pos.shape
    t = COMBINE_TOKENS
    win = COMBINE_WINDOW
    assert 2 * n_exp <= LANES and n < 256 * 256
    kern = functools.partial(_invert_kernel, n_exp=n_exp, win=win, cap_tot=cap_tot, tokens=t)
    grid_spec = pltpu.PrefetchScalarGridSpec(
        num_scalar_prefetch=2,
        grid=(n // t,),
        in_specs=[pl.BlockSpec((n_exp, t), lambda i, st, en: (0, i))],
        out_specs=pl.BlockSpec((cap_tot, LANES), lambda i, st, en: (0, 0)),
    )
    acc = pl.pallas_call(
        kern,
        grid_spec=grid_spec,
        out_shape=jax.ShapeDtypeStruct((cap_tot, LANES), F32),
        compiler_params=_params("arbitrary"),
        name="ec_invert",
    )(starts, ends, pos.T)
    digits = acc[:, :2 * n_exp].astype(jnp.int32).reshape(cap_tot, n_exp, 2)
    return (digits[:, :, 0] * 256 + digits[:, :, 1]).T.reshape(-1)


SELECT_ROW_TILE = 512


def _select_kernel(aff_ref, pos_ref, cnt_ref, *, cap, n_exp):
    a = aff_ref[...]
    r = a.shape[0]
    bits = pltpu.bitcast(a, jnp.int32)

    def fold(v):
        sh = n_exp
        while sh < LANES:
            v = v + pltpu.roll(v, sh, 1)
            sh *= 2
        return v

    def count(mask):
        return fold(jnp.sum(jnp.where(mask, 1.0, 0.0), axis=0, keepdims=True))

    def search(i, thr):
        cand = thr | jnp.left_shift(jnp.int32(1), 30 - i)
        return jnp.where(count(bits >= cand) >= cap, cand, thr)

    thr = lax.fori_loop(0, 31, search, jnp.zeros((1, LANES), jnp.int32))
    above = bits > thr
    tied = bits == thr
    need = cap - count(above)

    li = lax.broadcasted_iota(jnp.int32, (LANES, 2 * LANES), 0)
    ci = lax.broadcasted_iota(jnp.int32, (LANES, 2 * LANES), 1)
    same_exp = (li & (n_exp - 1)) == (ci & (n_exp - 1))
    earlier = (li // n_exp) < ((ci & (LANES - 1)) // n_exp)
    w2 = jnp.where(same_exp & ((ci >= LANES) | earlier), 1.0, 0.0).astype(BF16)
    tr = min(SELECT_ROW_TILE, r)
    rr = lax.broadcasted_iota(jnp.int32, (tr, tr), 0)
    rc = lax.broadcasted_iota(jnp.int32, (tr, tr), 1)
    rows_before = jnp.where(rr > rc, 1.0, 0.0).astype(BF16)

    def prefix(mask):
        lw = jnp.dot(jnp.where(mask, 1.0, 0.0).astype(BF16), w2, preferred_element_type=F32)
        within, row_tot = lw[:, :LANES], lw[:, LANES:]
        carry = jnp.zeros((1, LANES), F32)
        outs = []
        for t in range(r // tr):
            rt = row_tot[t * tr:(t + 1) * tr]
            outs.append(jnp.dot(rows_before, rt.astype(BF16), preferred_element_type=F32) + carry
                        + within[t * tr:(t + 1) * tr])
            carry = carry + jnp.sum(rt, axis=0, keepdims=True)
        return jnp.concatenate(outs, axis=0)

    sel = above | (tied & (prefix(tied) < need))
    cnt = prefix(sel).astype(jnp.int32)
    pos_ref[...] = jnp.where(sel, cnt, -1)
    cnt_ref[...] = cnt


def _ec_select(aff_group, cap):
    n_g, n_exp = aff_group.shape
    assert LANES % n_exp == 0 and (n_exp & (n_exp - 1)) == 0
    r = n_g * n_exp // LANES
    assert r % min(SELECT_ROW_TILE, r) == 0
    shp = jax.ShapeDtypeStruct((r, LANES), jnp.int32)
    pos, cnt = pl.pallas_call(
        functools.partial(_select_kernel, cap=cap, n_exp=n_exp),
        out_shape=[shp, shp],
        compiler_params=pltpu.CompilerParams(vmem_limit_bytes=V7X_VMEM_LIMIT_BYTES),
        name="ec_select",
    )(aff_group.reshape(r, LANES))
    return pos.reshape(n_g, n_exp), cnt.reshape(n_g, n_exp)


COMBINE_TOKENS = 256
COMBINE_WINDOW = 64
BF16_ROWS = 16


def _combine_kernel(st_ref, en_ref, x_ref, pos_ref, aff_ref, g_ref, b_ref, o_hbm, *rest, alpha, n_exp, win, cap_tot,
                    n_tiles, split_tiles):
    n_out = 1 if split_tiles is None else 2
    out_refs = rest[:n_out]
    buf, sem, xbuf, xsem, acc_ref = rest[n_out:]
    i = pl.program_id(0)
    slot = i % 2

    def window(tile, e):
        s0 = st_ref[tile * n_exp + e]
        return jnp.minimum(lax.shift_left(lax.shift_right_logical(s0, 4), 4), cap_tot - win)

    def fetch(tile, sl, e):
        w = pl.multiple_of(window(tile, e), BF16_ROWS)
        return pltpu.make_async_copy(o_hbm.at[e, pl.ds(w, win), :], buf.at[sl, pl.ds(e * win, win), :], sem.at[sl, e])

    @pl.when(i == 0)
    def _():
        for e in range(n_exp):
            fetch(0, 0, e).start()

    @pl.when(i + 1 < n_tiles)
    def _():
        for e in range(n_exp):
            fetch(i + 1, 1 - slot, e).start()

    pos = pos_ref[...]
    aff = aff_ref[...]
    t = pos.shape[0]
    lane = lax.broadcasted_iota(jnp.int32, (t, 2 * win), 1)
    first = lane < win
    lane_in = jnp.where(first, lane, lane - win)
    parts = []
    for e in range(0, n_exp, 2):
        rel = jnp.where(first, pos[:, e:e + 1] - window(i, e), pos[:, e + 1:e + 2] - window(i, e + 1))
        gate = jnp.where(first, aff[:, e:e + 1], aff[:, e + 1:e + 2])
        parts.append(jnp.where(rel == lane_in, gate, 0.0))
    pmat = jnp.concatenate(parts, axis=1).astype(BF16)
    for e in range(n_exp):
        fetch(i, slot, e).wait()
    acc_ref[...] = jnp.dot(pmat, buf[slot], preferred_element_type=F32)

    lane1 = lax.broadcasted_iota(jnp.int32, (t, win), 1)
    for e in range(n_exp):
        w = window(i, e)
        s1 = en_ref[i * n_exp + e]
        n_extra = jnp.maximum(s1 - w - 1, 0) // win

        def extra(k, carry, e=e, w=w):
            lo = w + win * (k + 1)
            wk = pl.multiple_of(jnp.minimum(lo, cap_tot - win), BF16_ROWS)
            cp = pltpu.make_async_copy(o_hbm.at[e, pl.ds(wk, win), :], xbuf, xsem)
            cp.start()
            cp.wait()
            pe = pos[:, e:e + 1]
            oh = jnp.where((pe - wk == lane1) & (pe >= lo), 1.0, 0.0).astype(BF16)
            acc_ref[...] += jnp.dot(oh, xbuf[...], preferred_element_type=F32) * aff[:, e:e + 1]
            return carry

        lax.fori_loop(0, n_extra, extra, 0)
    y = _res_ln(x_ref[...], acc_ref[...], g_ref[...], b_ref[...], alpha)
    if split_tiles is None:
        out_refs[0][...] = y
    else:
        @pl.when(i < split_tiles)
        def _():
            out_refs[0][...] = y

        @pl.when(i >= split_tiles)
        def _():
            out_refs[1][...] = y


def _moe_combine_ln(x2d, pos, aff, starts, ends, o, g, b, alpha, split=None):
    n, d = x2d.shape
    n_exp, cap_tot, _ = o.shape
    t = COMBINE_TOKENS
    win = COMBINE_WINDOW
    assert n % t == 0 and cap_tot % BF16_ROWS == 0 and cap_tot >= win and n_exp % 2 == 0 and 2 * win == LANES
    n_tiles = n // t
    row = pl.BlockSpec((1, d), lambda i, st, en: (0, 0))
    tok = lambda w: pl.BlockSpec((t, w), lambda i, st, en: (i, 0))
    if split is None:
        split_tiles = None
        out_specs = tok(d)
        out_shape = jax.ShapeDtypeStruct((n, d), F32)
    else:
        assert split % t == 0 and 0 < split < n
        split_tiles = split // t
        out_specs = [pl.BlockSpec((t, d), lambda i, st, en: (jnp.minimum(i, split_tiles - 1), 0)),
                     pl.BlockSpec((t, d), lambda i, st, en: (jnp.maximum(i - split_tiles, 0), 0))]
        out_shape = [jax.ShapeDtypeStruct((split, d), F32), jax.ShapeDtypeStruct((n - split, d), F32)]
    kern = functools.partial(_combine_kernel, alpha=alpha, n_exp=n_exp, win=win, cap_tot=cap_tot, n_tiles=n_tiles,
                             split_tiles=split_tiles)
    grid_spec = pltpu.PrefetchScalarGridSpec(
        num_scalar_prefetch=2,
        grid=(n_tiles,),
        in_specs=[tok(d), tok(n_exp), tok(n_exp), row, row, pl.BlockSpec(memory_space=pl.ANY)],
        out_specs=out_specs,
        scratch_shapes=[pltpu.VMEM((2, n_exp * win, d), BF16), pltpu.SemaphoreType.DMA((2, n_exp)),
                        pltpu.VMEM((win, d), BF16), pltpu.SemaphoreType.DMA(()), pltpu.VMEM((t, d), F32)],
    )
    return pl.pallas_call(
        kern,
        grid_spec=grid_spec,
        out_shape=out_shape,
        compiler_params=_params("arbitrary"),
        name="moe_combine_ln",
    )(starts, ends, x2d, pos, aff, g.reshape(1, d), b.reshape(1, d), o)


def _ec_moe_ln(x2d, aff, groups, wg, wu, wd, layer, g, b, alpha, split=None):
    n, d = x2d.shape
    n_exp = aff.shape[1]
    t = COMBINE_TOKENS
    pos_l, st_l, en_l = [], [], []
    off = 0
    for start, cnt_tok in groups:
        assert start % t == 0 and cnt_tok % t == 0
        cap = EC_CAPACITY_FACTOR * cnt_tok // n_exp
        pos, cnt = _ec_select(aff[start:start + cnt_tok], cap)
        st = cnt[::t] + off
        en = jnp.concatenate([st[1:], jnp.full((1, n_exp), off + cap, jnp.int32)], axis=0)
        pos_l.append(jnp.where(pos >= 0, pos + off, -1))
        st_l.append(st)
        en_l.append(en)
        off += cap
    cap_tot = off
    pos = jnp.concatenate(pos_l, axis=0)
    starts = jnp.concatenate(st_l, axis=0).reshape(-1)
    ends = jnp.concatenate(en_l, axis=0).reshape(-1)
    idx = _ec_invert(pos, starts, ends, cap_tot)
    o = _moe_ffn(x2d, idx, wg, wu, wd, layer)
    return _moe_combine_ln(x2d, pos, aff, starts, ends, o, g, b, alpha, split)


def _split2_bf16(w):
    hi = w.astype(BF16)
    lo = (w - hi.astype(F32)).astype(BF16)
    return jnp.concatenate([hi, lo], axis=1)


def kernel(x_prompt, x_sample, attn_w_qkv, attn_q_norm, attn_k_norm, attn_w_o, pool_w, pool_scale, ssd_w_in,
           ssd_conv_w, ssd_conv_b, ssd_dt_bias, ssd_A_log, ssd_D, ssd_norm, ssd_w_out, moe_w_router, moe_w_gate,
           moe_w_up, moe_w_down, ln_g, ln_b):
    bp, s, d = x_prompt.shape
    bs = x_sample.shape[0]
    assert x_sample.shape[1] == s
    bsz = bp + bs
    n = bsz * s
    groups = [(0, bp * s), (bp * s, bs * s)]
    depth = ln_g.shape[0]
    alpha = (2 * depth) ** 0.25
    n_heads = attn_w_o.shape[1] // HEAD_DIM
    n_kv = (attn_w_qkv.shape[2] // HEAD_DIM - n_heads) // 2
    d_inner = ssd_w_out.shape[1]
    n_ssd_heads = ssd_A_log.shape[-1]
    assert d_inner == n_ssd_heads * SSD_HEAD_DIM
    conv_dim = ssd_conv_w.shape[2]
    n_groups = (conv_dim - d_inner) // (2 * D_STATE)
    rope = _rope_tables(s)

    x = jnp.concatenate([x_prompt, x_sample], axis=0).reshape(n, d)
    wg_all, wu_all, wd_all = moe_w_gate.astype(BF16), moe_w_up.astype(BF16), moe_w_down.astype(BF16)
    ia = ip = isd = 0
    for i in range(depth):
        wr2 = _split2_bf16(moe_w_router[i])
        g1, b1, g2, b2 = ln_g[i, 0], ln_b[i, 0], ln_g[i, 1], ln_b[i, 1]
        kind = i % 3
        if kind == 0:
            w_qkv = _permute_qk_columns(attn_w_qkv[ia], n_heads, n_kv).astype(BF16)
            qkv = _qkv_proj(x, w_qkv, attn_q_norm[ia], attn_k_norm[ia], rope, s, n_heads, n_kv)
            qkv3 = qkv.reshape(bsz, s, -1)
            vt = jnp.swapaxes(qkv3[:, :, (n_heads + n_kv) * HEAD_DIM:], 1, 2).reshape(bsz, n_kv, HEAD_DIM, s)
            o = _flash_attention(qkv3, vt, n_heads, n_kv)
            x, aff = _mm_res_ln_router(o.reshape(n, -1), attn_w_o[ia].astype(BF16), x, g1, b1, wr2, alpha)
            ia += 1
        elif kind == 1:
            x3, aff3 = _pool_layer(x.reshape(bsz, s, d), pool_w[ip].astype(BF16), pool_scale[ip], g1, b1, wr2, alpha)
            x, aff = x3.reshape(n, d), aff3.reshape(n, -1)
            ip += 1
        else:
            zx = _matmul_f32(x, ssd_w_in[isd].astype(BF16), 1152)
            zx3 = zx.reshape(bsz, s, -1)
            xbc = _ssd_conv(zx3, ssd_conv_w[isd], ssd_conv_b[isd], d_inner, conv_dim)
            a_neg = -jnp.exp(ssd_A_log[isd].astype(F32)).reshape(-1)
            dt, ecs, dt_t, ecs_t, tot = _ssd_dt(zx3, ssd_dt_bias[isd].reshape(-1), a_neg, d_inner + conv_dim)
            yf, yb = _ssd_scan(xbc, dt, ecs, dt_t, ecs_t, tot, ssd_D[isd], d_inner, n_groups)
            yn = _ssd_gate(yf.reshape(n, d_inner), yb.reshape(n, d_inner), zx, ssd_norm[isd])
            x, aff = _mm_res_ln_router(yn, ssd_w_out[isd].astype(BF16), x, g1, b1, wr2, alpha)
            isd += 1
        x = _ec_moe_ln(x, aff, groups, wg_all, wu_all, wd_all, i, g2, b2, alpha,
                       split=bp * s if i == depth - 1 else None)
    y_prompt, y_sample = x
    return y_prompt.reshape(bp, s, d), y_sample.reshape(bs, s, d)
```

```python
import functools
import math

import jax
import jax.numpy as jnp
from jax import lax
from jax.experimental import pallas as pl
from jax.experimental.pallas import tpu as pltpu

F32 = jnp.float32
BF16 = jnp.bfloat16

HEAD_DIM = 128
GRID_W = 64
ROPE_THETA = 10000.0
POOL_WINDOWS = (2, 4, 8, 16)
POOL_HALO = 8
D_STATE = 128
SSD_CHUNK = 128
SSD_HEAD_DIM = 64
SSD_HEADS_PER_GROUP = 8
D_CONV = 4
CONV_LEFT = D_CONV // 2
EC_CAPACITY_FACTOR = 2
LN_EPS = 1e-5
RMS_EPS = 1e-6
LOG2E = 1.4426950408889634

V7X_VMEM_LIMIT_BYTES = 52 * 1024 * 1024
LANES = 128


def _params(*sem):
    return pltpu.CompilerParams(dimension_semantics=sem, vmem_limit_bytes=V7X_VMEM_LIMIT_BYTES)


def _pick(n, pref):
    t = min(n, pref)
    while n % t:
        t //= 2
    return t


def _res_ln(x, h, g, b, alpha):
    y = alpha * x + h
    mu = jnp.mean(y, axis=-1, keepdims=True)
    yc = y - mu
    var = jnp.mean(yc * yc, axis=-1, keepdims=True)
    return yc * lax.rsqrt(var + LN_EPS) * g + b


def _router_affinity(xn, wr_ref, n_exp):
    xh = xn.astype(BF16)
    xl = (xn - xh.astype(F32)).astype(BF16)
    wr = wr_ref[...]
    r1 = jnp.dot(xh, wr, preferred_element_type=F32)
    r2 = jnp.dot(xl, wr[:, :n_exp], preferred_element_type=F32)
    logits = r1[:, :n_exp] + (r1[:, n_exp:] + r2)
    m = jnp.max(logits, axis=-1, keepdims=True)
    e = jnp.exp(logits - m)
    return e / jnp.sum(e, axis=-1, keepdims=True)


def _qkv_kernel(x_ref, w_ref, cos_ref, sin_ref, qn_ref, kn_ref, o_ref, xb_ref, *,
                n_q_tiles, n_k_tiles, heads_per_tile, q_scale):
    j = pl.program_id(1)

    @pl.when(j == 0)
    def _():
        xb_ref[...] = x_ref[...].astype(BF16)

    acc = jnp.dot(xb_ref[...], w_ref[...], preferred_element_type=F32)

    def norm_rope(gain_ref, scale):
        cos = cos_ref[...]
        sin = sin_ref[...]
        g = gain_ref[...]
        for h in range(heads_per_tile):
            a = acc[:, h * HEAD_DIM:(h + 1) * HEAD_DIM]
            a = a * lax.rsqrt(jnp.mean(a * a, axis=-1, keepdims=True) + RMS_EPS) * g
            r = a * cos + pltpu.roll(a, HEAD_DIM // 2, 1) * sin
            o_ref[:, h * HEAD_DIM:(h + 1) * HEAD_DIM] = (r * scale).astype(BF16)

    @pl.when(j < n_q_tiles)
    def _():
        norm_rope(qn_ref, q_scale)

    @pl.when((j >= n_q_tiles) & (j < n_q_tiles + n_k_tiles))
    def _():
        norm_rope(kn_ref, 1.0)

    @pl.when(j >= n_q_tiles + n_k_tiles)
    def _():
        o_ref[...] = acc.astype(BF16)


def _rope_perm():
    quarter = HEAD_DIM // 4
    blocks = (0, 2, 1, 3)
    return jnp.concatenate([jnp.arange(quarter) + b * quarter for b in blocks])


def _rope_tables(seq_len):
    rows = seq_len // GRID_W
    row = jnp.repeat(jnp.arange(rows, dtype=F32), GRID_W)
    col = jnp.tile(jnp.arange(GRID_W, dtype=F32), rows)
    inv_freq = ROPE_THETA ** (-jnp.arange(0, HEAD_DIM // 2, 2, dtype=F32) / (HEAD_DIM // 2))
    ang = jnp.concatenate([row[:, None] * inv_freq, col[:, None] * inv_freq], axis=-1)
    cos = jnp.concatenate([jnp.cos(ang), jnp.cos(ang)], axis=-1)
    sin = jnp.concatenate([-jnp.sin(ang), jnp.sin(ang)], axis=-1)
    return cos, sin


def _permute_qk_columns(w_qkv, n_heads, n_kv):
    d = w_qkv.shape[0]
    n_qk = n_heads + n_kv
    qk = w_qkv[:, :n_qk * HEAD_DIM].reshape(d, n_qk, HEAD_DIM)[:, :, _rope_perm()].reshape(d, n_qk * HEAD_DIM)
    return jnp.concatenate([qk, w_qkv[:, n_qk * HEAD_DIM:]], axis=1)


def _qkv_proj(x2d, w_bf16, q_norm, k_norm, rope, seq_len, n_heads, n_kv):
    n, d = x2d.shape
    qkv_dim = w_bf16.shape[1]
    tn = n_kv * HEAD_DIM
    tm = _pick(seq_len, 1024)
    cos, sin = rope
    perm = _rope_perm()
    q_norm, k_norm = q_norm[perm], k_norm[perm]
    nsb = seq_len // tm
    kern = functools.partial(
        _qkv_kernel, n_q_tiles=n_heads // n_kv, n_k_tiles=1, heads_per_tile=n_kv,
        q_scale=HEAD_DIM ** -0.5 * LOG2E)
    tab = pl.BlockSpec((tm, HEAD_DIM), lambda i, j: (i % nsb, 0))
    vec = pl.BlockSpec((1, HEAD_DIM), lambda i, j: (0, 0))
    return pl.pallas_call(
        kern,
        grid=(n // tm, qkv_dim // tn),
        in_specs=[pl.BlockSpec((tm, d), lambda i, j: (i, 0)),
                  pl.BlockSpec((d, tn), lambda i, j: (0, j)),
                  tab, tab, vec, vec],
        out_specs=pl.BlockSpec((tm, tn), lambda i, j: (i, j)),
        out_shape=jax.ShapeDtypeStruct((n, qkv_dim), BF16),
        scratch_shapes=[pltpu.VMEM((tm, d), BF16)],
        compiler_params=_params("parallel", "arbitrary"),
        name="qkv_proj",
    )(x2d, w_bf16, cos, sin, q_norm.reshape(1, HEAD_DIM), k_norm.reshape(1, HEAD_DIM))


FLASH_TQ = 128
FLASH_TK = 2048


def _flash_kernel(q_ref, k_ref, vt_ref, o_ref, s_scr, p_scr, acc_scr, *, tk, group):
    tq = q_ref.shape[1]
    seq = k_ref.shape[1]
    q = jnp.concatenate([q_ref[0, :, g * HEAD_DIM:(g + 1) * HEAD_DIM] for g in range(group)], axis=0)
    rows = group * tq
    nc = seq // tk

    def scores(c, slot):
        k = k_ref[0, pl.ds(c * tk, tk), :]
        s_scr[slot] = lax.dot_general(k, q, (((1,), (1,)), ((), ())), preferred_element_type=F32)

    def pv(c, slot, alpha):
        vt = vt_ref[0, 0, :, pl.ds(c * tk, tk)]
        acc_scr[...] = acc_scr[...] * alpha + jnp.dot(vt, p_scr[slot], preferred_element_type=F32)

    def softmax(slot, m, l):
        s = s_scr[slot]
        m_new = jnp.maximum(m, jnp.max(s, axis=0, keepdims=True))
        alpha = jnp.exp2(m - m_new)
        p = jnp.exp2(s - m_new)
        l = alpha * l + jnp.sum(p, axis=0, keepdims=True)
        p_scr[slot] = p.astype(BF16)
        return m_new, l, alpha

    m = jnp.full((1, rows), -jnp.inf, F32)
    l = jnp.zeros((1, rows), F32)
    acc_scr[...] = jnp.zeros_like(acc_scr)
    scores(0, 0)
    if nc > 1:
        scores(1, 1)
    m, l, alpha = softmax(0, m, l)
    for c in range(1, nc):
        if c + 1 < nc:
            scores(c + 1, (c + 1) % 2)
        pv(c - 1, (c - 1) % 2, alpha)
        m, l, alpha = softmax(c % 2, m, l)
    pv(nc - 1, (nc - 1) % 2, alpha)
    o = (acc_scr[...] / l).T
    for g in range(group):
        o_ref[0, :, g * HEAD_DIM:(g + 1) * HEAD_DIM] = o[g * tq:(g + 1) * tq].astype(BF16)


def _flash_attention(qkv, vt, n_heads, n_kv):
    b, s, _ = qkv.shape
    group = n_heads // n_kv
    tq = _pick(s, FLASH_TQ)
    tk = _pick(s, FLASH_TK)
    gw = group * HEAD_DIM
    rows = group * tq
    vrows = vt.shape[2]
    kern = functools.partial(_flash_kernel, tk=tk, group=group)
    return pl.pallas_call(
        kern,
        grid=(b, n_kv, s // tq),
        in_specs=[pl.BlockSpec((1, tq, gw), lambda bi, h, i: (bi, i, h)),
                  pl.BlockSpec((1, s, HEAD_DIM), lambda bi, h, i: (bi, 0, n_heads + h)),
                  pl.BlockSpec((1, 1, vrows, s), lambda bi, h, i: (bi, h, 0, 0))],
        out_specs=pl.BlockSpec((1, tq, gw), lambda bi, h, i: (bi, i, h)),
        out_shape=jax.ShapeDtypeStruct((b, s, n_heads * HEAD_DIM), BF16),
        scratch_shapes=[pltpu.VMEM((2, tk, rows), F32), pltpu.VMEM((2, tk, rows), BF16),
                        pltpu.VMEM((vrows, rows), F32)],
        compiler_params=_params("parallel", "parallel", "arbitrary"),
        name="flash_attention",
    )(qkv, qkv, vt)


MM_LN_TM = 512
MM_LN_TK = 2048


def _mm_ln_kernel(a_ref, w_ref, x_ref, g_ref, b_ref, wr_ref, o_ref, aff_ref, acc_ref, *, alpha, nk, n_exp):
    k = pl.program_id(1)
    part = jnp.dot(a_ref[...], w_ref[...], preferred_element_type=F32)

    def finish(h):
        xn = _res_ln(x_ref[...], h, g_ref[...], b_ref[...], alpha)
        o_ref[...] = xn
        aff_ref[...] = _router_affinity(xn, wr_ref, n_exp)

    if nk == 1:
        finish(part)
        return

    @pl.when(k == 0)
    def _():
        acc_ref[...] = part

    @pl.when((k > 0) & (k < nk - 1))
    def _():
        acc_ref[...] += part

    @pl.when(k == nk - 1)
    def _():
        finish(acc_ref[...] + part)


def _mm_res_ln_router(a_bf16, w_bf16, x2d, g, b, wr2, alpha):
    n, kdim = a_bf16.shape
    d = w_bf16.shape[1]
    n_exp = wr2.shape[1] // 2
    tm = _pick(n, MM_LN_TM)
    tk = _pick(kdim, MM_LN_TK)
    nk = kdim // tk
    kern = functools.partial(_mm_ln_kernel, alpha=alpha, nk=nk, n_exp=n_exp)
    row = pl.BlockSpec((1, d), lambda i, k: (0, 0))
    return pl.pallas_call(
        kern,
        grid=(n // tm, nk),
        in_specs=[pl.BlockSpec((tm, tk), lambda i, k: (i, k)),
                  pl.BlockSpec((tk, d), lambda i, k: (k, 0)),
                  pl.BlockSpec((tm, d), lambda i, k: (i, 0)),
                  row, row,
                  pl.BlockSpec((d, 2 * n_exp), lambda i, k: (0, 0))],
        out_specs=[pl.BlockSpec((tm, d), lambda i, k: (i, 0)),
                   pl.BlockSpec((tm, n_exp), lambda i, k: (i, 0))],
        out_shape=[jax.ShapeDtypeStruct((n, d), F32), jax.ShapeDtypeStruct((n, n_exp), F32)],
        scratch_shapes=[pltpu.VMEM((tm, d), F32)],
        compiler_params=_params("parallel", "arbitrary"),
        name="mm_res_ln_router",
    )(a_bf16, w_bf16, x2d, g.reshape(1, d), b.reshape(1, d), wr2)


def _pool_kernel(prev_ref, cur_ref, next_ref, w_ref, sc_ref, g_ref, b_ref, wr_ref, o_ref, aff_ref, ext_ref, *,
                 alpha, nt, seq_len, n_exp):
    i = pl.program_id(1)
    t = cur_ref.shape[1]
    d = cur_ref.shape[2]
    pg = d // len(POOL_WINDOWS)
    x = cur_ref[0]
    ext_ref[0:POOL_HALO, :] = jnp.where(i == 0, 0.0, prev_ref[0])
    ext_ref[POOL_HALO:POOL_HALO + t, :] = x
    ext_ref[POOL_HALO + t:2 * POOL_HALO + t, :] = jnp.where(i == nt - 1, 0.0, next_ref[0])
    pos = i * t + lax.broadcasted_iota(jnp.int32, (t, 1), 0)
    hs = []
    for gi, w in enumerate(POOL_WINDOWS):
        half = w // 2
        cols = slice(gi * pg, (gi + 1) * pg)
        acc = ext_ref[pl.ds(POOL_HALO - half, t), cols]
        for jj in range(1, w):
            acc = acc + ext_ref[pl.ds(POOL_HALO - half + jj, t), cols]
        cnt = (jnp.minimum(pos + half, seq_len) - jnp.maximum(pos - half, 0)).astype(F32)
        mixed = (acc / cnt - x[:, cols]).astype(BF16)
        hs.append(jnp.dot(mixed, w_ref[gi], preferred_element_type=F32))
    h = jnp.concatenate(hs, axis=-1) * sc_ref[...]
    xn = _res_ln(x, h, g_ref[...], b_ref[...], alpha)
    o_ref[0] = xn
    aff_ref[0] = _router_affinity(xn, wr_ref, n_exp)


def _pool_layer(x3d, w_bf16, scale, g, b, wr2, alpha):
    bsz, s, d = x3d.shape
    n_exp = wr2.shape[1] // 2
    t = _pick(s, 512)
    nt = s // t
    hb = t // POOL_HALO
    last_hb = s // POOL_HALO - 1
    pg = d // len(POOL_WINDOWS)
    kern = functools.partial(_pool_kernel, alpha=alpha, nt=nt, seq_len=s, n_exp=n_exp)
    row = pl.BlockSpec((1, d), lambda bi, i: (0, 0))
    return pl.pallas_call(
        kern,
        grid=(bsz, nt),
        in_specs=[pl.BlockSpec((1, POOL_HALO, d), lambda bi, i: (bi, jnp.maximum(i * hb - 1, 0), 0)),
                  pl.BlockSpec((1, t, d), lambda bi, i: (bi, i, 0)),
                  pl.BlockSpec((1, POOL_HALO, d), lambda bi, i: (bi, jnp.minimum((i + 1) * hb, last_hb), 0)),
                  pl.BlockSpec((len(POOL_WINDOWS), pg, pg), lambda bi, i: (0, 0, 0)),
                  row, row, row,
                  pl.BlockSpec((d, 2 * n_exp), lambda bi, i: (0, 0))],
        out_specs=[pl.BlockSpec((1, t, d), lambda bi, i: (bi, i, 0)),
                   pl.BlockSpec((1, t, n_exp), lambda bi, i: (bi, i, 0))],
        out_shape=[jax.ShapeDtypeStruct((bsz, s, d), F32), jax.ShapeDtypeStruct((bsz, s, n_exp), F32)],
        scratch_shapes=[pltpu.VMEM((t + 2 * POOL_HALO, d), F32)],
        compiler_params=_params("parallel", "parallel"),
        name="pool_mixer",
    )(x3d, x3d, x3d, w_bf16, scale.reshape(1, d), g.reshape(1, d), b.reshape(1, d), wr2)


def _mm_kernel(x_ref, w_ref, o_ref, xb_ref):
    @pl.when(pl.program_id(1) == 0)
    def _():
        xb_ref[...] = x_ref[...].astype(BF16)

    o_ref[...] = jnp.dot(xb_ref[...], w_ref[...], preferred_element_type=F32)


def _matmul_f32(x2d, w_bf16, tn_pref):
    n, d = x2d.shape
    nout = w_bf16.shape[1]
    tm = _pick(n, 1024)
    tn = tn_pref
    assert nout % tn == 0
    return pl.pallas_call(
        _mm_kernel,
        grid=(n // tm, nout // tn),
        in_specs=[pl.BlockSpec((tm, d), lambda i, j: (i, 0)),
                  pl.BlockSpec((d, tn), lambda i, j: (0, j))],
        out_specs=pl.BlockSpec((tm, tn), lambda i, j: (i, j)),
        out_shape=jax.ShapeDtypeStruct((n, nout), F32),
        scratch_shapes=[pltpu.VMEM((tm, d), BF16)],
        compiler_params=_params("parallel", "arbitrary"),
        name="ssd_in_proj",
    )(x2d, w_bf16)


def _conv_kernel(prev_ref, cur_ref, next_ref, w_ref, b_ref, o_ref, ext_ref, *, nt):
    i = pl.program_id(1)
    t = cur_ref.shape[1]
    ext_ref[0:POOL_HALO, :] = jnp.where(i == 0, 0.0, prev_ref[0])
    ext_ref[POOL_HALO:POOL_HALO + t, :] = cur_ref[0]
    ext_ref[POOL_HALO + t:2 * POOL_HALO + t, :] = jnp.where(i == nt - 1, 0.0, next_ref[0])
    acc = ext_ref[pl.ds(POOL_HALO - CONV_LEFT, t), :] * w_ref[0:1, :]
    for kk in range(1, D_CONV):
        acc = acc + ext_ref[pl.ds(POOL_HALO - CONV_LEFT + kk, t), :] * w_ref[kk:kk + 1, :]
    acc = acc + b_ref[...]
    o_ref[0] = acc / (1.0 + jnp.exp(-acc))


def _ssd_conv(zx3d, conv_w, conv_b, d_inner, conv_dim):
    bsz, s, _ = zx3d.shape
    tc = 1024
    t = _pick(s, 512)
    nt = s // t
    hb = t // POOL_HALO
    last_hb = s // POOL_HALO - 1
    c0 = d_inner // tc
    return pl.pallas_call(
        functools.partial(_conv_kernel, nt=nt),
        grid=(bsz, nt, conv_dim // tc),
        in_specs=[pl.BlockSpec((1, POOL_HALO, tc), lambda bi, i, j: (bi, jnp.maximum(i * hb - 1, 0), c0 + j)),
                  pl.BlockSpec((1, t, tc), lambda bi, i, j: (bi, i, c0 + j)),
                  pl.BlockSpec((1, POOL_HALO, tc), lambda bi, i, j: (bi, jnp.minimum((i + 1) * hb, last_hb), c0 + j)),
                  pl.BlockSpec((D_CONV, tc), lambda bi, i, j: (0, j)),
                  pl.BlockSpec((1, tc), lambda bi, i, j: (0, j))],
        out_specs=pl.BlockSpec((1, t, tc), lambda bi, i, j: (bi, i, j)),
        out_shape=jax.ShapeDtypeStruct((bsz, s, conv_dim), F32),
        scratch_shapes=[pltpu.VMEM((t + 2 * POOL_HALO, tc), F32)],
        compiler_params=_params("parallel", "parallel", "parallel"),
        name="ssd_conv",
    )(zx3d, zx3d, zx3d, conv_w, conv_b.reshape(1, conv_dim))


def _split3(x):
    hi = x.astype(BF16)
    r1 = x - hi.astype(F32)
    mid = r1.astype(BF16)
    lo = (r1 - mid.astype(F32)).astype(BF16)
    return hi, mid, lo


def _dt_kernel(raw_ref, bias_ref, a_ref, dt_ref, e_ref, dtt_ref, et_ref, tot_ref):
    v = raw_ref[0] + bias_ref[...]
    dt = jnp.maximum(v, 0.0) + jnp.log1p(jnp.exp(-jnp.abs(v)))
    dt_ref[0] = dt
    dtt_ref[0] = dt.T
    a = dt * a_ref[...]
    q, w = a.shape
    li = lax.broadcasted_iota(jnp.int32, (q, q), 0)
    si = lax.broadcasted_iota(jnp.int32, (q, q), 1)
    tri = jnp.where(li >= si, 1.0, 0.0).astype(BF16)
    hi, mid, lo = _split3(a)
    cs = (jnp.dot(tri, lo, preferred_element_type=F32) + jnp.dot(tri, mid, preferred_element_type=F32)
          + jnp.dot(tri, hi, preferred_element_type=F32))
    lane = lax.broadcasted_iota(jnp.int32, (q, w), 1)
    e = jnp.where(lane < w // 2, cs, cs - a)
    e_ref[0] = e
    et_ref[0] = e.T
    tot_ref[0, 0] = cs[q - 1:q, :]


def _ssd_dt(zx3d, dt_bias, a_neg, col0):
    bsz, s, _ = zx3d.shape
    w = dt_bias.shape[-1]
    assert w == LANES and col0 % LANES == 0 and SSD_CHUNK == LANES
    nc = s // SSD_CHUNK
    blk = pl.BlockSpec((1, SSD_CHUNK, w), lambda bi, c: (bi, c, 0))
    blk_t = pl.BlockSpec((1, w, SSD_CHUNK), lambda bi, c: (bi, 0, c))
    row = pl.BlockSpec((1, w), lambda bi, c: (0, 0))
    return pl.pallas_call(
        _dt_kernel,
        grid=(bsz, nc),
        in_specs=[pl.BlockSpec((1, SSD_CHUNK, w), lambda bi, c: (bi, c, col0 // LANES)), row, row],
        out_specs=[blk, blk, blk_t, blk_t, pl.BlockSpec((1, 1, 1, w), lambda bi, c: (bi, c, 0, 0))],
        out_shape=[jax.ShapeDtypeStruct((bsz, s, w), F32), jax.ShapeDtypeStruct((bsz, s, w), F32),
                   jax.ShapeDtypeStruct((bsz, w, s), F32), jax.ShapeDtypeStruct((bsz, w, s), F32),
                   jax.ShapeDtypeStruct((bsz, nc, 1, w), F32)],
        compiler_params=_params("parallel", "parallel"),
        name="ssd_dt",
    )(zx3d, dt_bias.reshape(1, w), a_neg.reshape(1, w))


def _expand_heads(v, j0, width):
    m = v.shape[0]
    lane = lax.broadcasted_iota(jnp.int32, (m, LANES), 1)
    parts = []
    for pr in range(width // LANES):
        j = j0 + 2 * pr
        parts.append(jnp.where(lane < SSD_HEAD_DIM, v[:, j:j + 1], v[:, j + 1:j + 2]))
    return jnp.concatenate(parts, axis=1)


def _ssd_direction(x, bmat, cmat, dt_r, e_c, e_r, out_dec, st_w, chunk_dec, j0, st_ref, forward):
    q, width = x.shape
    li = lax.broadcasted_iota(jnp.int32, (q, q), 0)
    si = lax.broadcasted_iota(jnp.int32, (q, q), 1)
    lane = lax.broadcasted_iota(jnp.int32, (q, LANES), 1)
    mask = (li >= si) if forward else (si >= li)
    cb = lax.dot_general(cmat.astype(BF16), bmat.astype(BF16), (((1,), (1,)), ((), ())),
                         preferred_element_type=F32)
    xb = x.astype(BF16)
    y_parts = []
    for pr in range(width // LANES):
        ms = []
        for j in (j0 + 2 * pr, j0 + 2 * pr + 1):
            if forward:
                diff = e_c[:, j:j + 1] - e_r[j:j + 1, :]
            else:
                diff = e_r[j:j + 1, :] - e_c[:, j:j + 1]
            decay = jnp.exp(jnp.where(mask, diff, -jnp.inf))
            ms.append((decay * cb * dt_r[j:j + 1, :]).astype(BF16))
        xp = xb[:, pr * LANES:(pr + 1) * LANES]
        zero = jnp.zeros_like(xp)
        rhs = jnp.concatenate([jnp.where(lane < SSD_HEAD_DIM, xp, zero),
                               jnp.where(lane >= SSD_HEAD_DIM, xp, zero)], axis=0)
        y_parts.append(jnp.dot(jnp.concatenate(ms, axis=1), rhs, preferred_element_type=F32))
    y = jnp.concatenate(y_parts, axis=1)
    st = st_ref[...]
    y = y + jnp.dot(cmat.astype(BF16), st.astype(BF16), preferred_element_type=F32) * _expand_heads(out_dec, j0, width)
    xd = (x * _expand_heads(st_w, j0, width)).astype(BF16)
    st_new = lax.dot_general(bmat.astype(BF16), xd, (((0,), (0,)), ((), ())), preferred_element_type=F32)
    st_ref[...] = st * _expand_heads(chunk_dec, j0, width)[0:1, :] + st_new
    return y


def _ssd_scan_kernel(xf_ref, bf_ref, cf_ref, dtf_ref, ef_ref, dttf_ref, etf_ref, tf_ref,
                     xr_ref, br_ref, cr_ref, dtr_ref, er_ref, dttr_ref, etr_ref, tr_ref,
                     dskip_ref, yf_ref, yb_ref, stf_ref, stb_ref, *, n_groups, gw):
    @pl.when(pl.program_id(1) == 0)
    def _():
        stf_ref[...] = jnp.zeros_like(stf_ref)
        stb_ref[...] = jnp.zeros_like(stb_ref)

    hg = SSD_HEADS_PER_GROUP
    ef, tf = ef_ref[0], tf_ref[0, 0]
    out_dec_f = jnp.exp(ef)
    st_w_f = dtf_ref[0] * jnp.exp(tf - ef)
    chunk_dec_f = jnp.broadcast_to(jnp.exp(tf), (8, LANES))
    er, tr = er_ref[0], tr_ref[0, 0]
    out_dec_r = jnp.exp(tr - er)
    st_w_r = dtr_ref[0] * jnp.exp(er)
    chunk_dec_r = jnp.broadcast_to(jnp.exp(tr), (8, LANES))
    dttf, etf, dttr, etr = dttf_ref[0], etf_ref[0], dttr_ref[0], etr_ref[0]
    for gi in range(n_groups):
        xs = slice(gi * gw, (gi + 1) * gw)
        ns = slice(gi * D_STATE, (gi + 1) * D_STATE)
        xf = xf_ref[0, :, xs]
        yf = _ssd_direction(xf, bf_ref[0, :, ns], cf_ref[0, :, ns], dttf, ef, etf, out_dec_f, st_w_f, chunk_dec_f,
                            gi * hg, stf_ref.at[gi], True)
        yf_ref[0, :, xs] = yf + dskip_ref[:, xs] * xf
        yb_ref[0, :, xs] = _ssd_direction(xr_ref[0, :, xs], br_ref[0, :, ns], cr_ref[0, :, ns], dttr, er, etr,
                                          out_dec_r, st_w_r, chunk_dec_r, (n_groups + gi) * hg, stb_ref.at[gi],
                                          False)


def _ssd_scan(xbc, dt, ecs, dt_t, ecs_t, tot, d_skip, d_inner, n_groups):
    bsz, s, _ = xbc.shape
    nc = s // SSD_CHUNK
    gw = SSD_HEADS_PER_GROUP * SSD_HEAD_DIM
    w = dt.shape[-1]
    assert gw % LANES == 0 and d_inner == n_groups * gw and w == 2 * n_groups * SSD_HEADS_PER_GROUP == LANES
    gn = n_groups * D_STATE
    assert d_inner % gn == 0
    b0 = d_inner // gn
    dskip = jnp.repeat(d_skip.astype(F32), SSD_HEAD_DIM).reshape(1, d_inner)

    def specs(cidx):
        return [
            pl.BlockSpec((1, SSD_CHUNK, d_inner), lambda b, c: (b, cidx(c), 0)),
            pl.BlockSpec((1, SSD_CHUNK, gn), lambda b, c: (b, cidx(c), b0)),
            pl.BlockSpec((1, SSD_CHUNK, gn), lambda b, c: (b, cidx(c), b0 + 1)),
            pl.BlockSpec((1, SSD_CHUNK, w), lambda b, c: (b, cidx(c), 0)),
            pl.BlockSpec((1, SSD_CHUNK, w), lambda b, c: (b, cidx(c), 0)),
            pl.BlockSpec((1, w, SSD_CHUNK), lambda b, c: (b, 0, cidx(c))),
            pl.BlockSpec((1, w, SSD_CHUNK), lambda b, c: (b, 0, cidx(c))),
            pl.BlockSpec((1, 1, 1, w), lambda b, c: (b, cidx(c), 0, 0)),
        ]

    fwd = lambda c: c
    bwd = lambda c: nc - 1 - c
    y_shape = jax.ShapeDtypeStruct((bsz, s, d_inner), F32)
    return pl.pallas_call(
        functools.partial(_ssd_scan_kernel, n_groups=n_groups, gw=gw),
        grid=(bsz, nc),
        in_specs=specs(fwd) + specs(bwd) + [pl.BlockSpec((1, d_inner), lambda b, c: (0, 0))],
        out_specs=[pl.BlockSpec((1, SSD_CHUNK, d_inner), lambda b, c: (b, c, 0)),
                   pl.BlockSpec((1, SSD_CHUNK, d_inner), lambda b, c: (b, nc - 1 - c, 0))],
        out_shape=[y_shape, y_shape],
        scratch_shapes=[pltpu.VMEM((n_groups, D_STATE, gw), F32), pltpu.VMEM((n_groups, D_STATE, gw), F32)],
        compiler_params=_params("parallel", "arbitrary"),
        name="ssd_scan",
    )(xbc, xbc, xbc, dt, ecs, dt_t, ecs_t, tot,
      xbc, xbc, xbc, dt, ecs, dt_t, ecs_t, tot, dskip)


def _gate_kernel(yf_ref, yb_ref, z_ref, nw_ref, o_ref):
    z = z_ref[...]
    y = (yf_ref[...] + yb_ref[...]) * (z / (1.0 + jnp.exp(-z)))
    y = y * lax.rsqrt(jnp.mean(y * y, axis=-1, keepdims=True) + RMS_EPS) * nw_ref[...]
    o_ref[...] = y.astype(BF16)


def _ssd_gate(yf2d, yb2d, zx2d, norm_w):
    n, d_inner = yf2d.shape
    tm = _pick(n, 256)
    blk = pl.BlockSpec((tm, d_inner), lambda i: (i, 0))
    return pl.pallas_call(
        _gate_kernel,
        grid=(n // tm,),
        in_specs=[blk, blk, blk, pl.BlockSpec((1, d_inner), lambda i: (0, 0))],
        out_specs=blk,
        out_shape=jax.ShapeDtypeStruct((n, d_inner), BF16),
        compiler_params=_params("parallel"),
        name="ssd_gate_norm",
    )(yf2d, yb2d, zx2d, norm_w.reshape(1, d_inner))


FFN_TILE = 256


def _ffn_kernel(idx_ref, x_hbm, wg_ref, wu_ref, wd_ref, o_ref, xa, xb, sem, *, tile, n_steps):
    step = pl.program_id(0) * pl.num_programs(1) + pl.program_id(1)

    def issue(tile_idx, buf, s):
        base = tile_idx * tile
        for r in range(tile):
            tok = idx_ref[base + r]
            pltpu.make_async_copy(x_hbm.at[pl.ds(tok, 1), :], buf.at[pl.ds(r, 1), :], sem.at[s]).start(priority=r % 2)

    def wait(buf, s):
        pltpu.make_async_copy(x_hbm.at[pl.ds(0, tile), :], buf, sem.at[s]).wait()

    def ffn(buf, half):
        xs = buf[...].astype(BF16)
        hg = jnp.dot(xs, wg_ref[0, 0], preferred_element_type=F32)
        hu = jnp.dot(xs, wu_ref[0, 0], preferred_element_type=F32)
        h = (hg / (1.0 + jnp.exp(-hg)) * hu).astype(BF16)
        o_ref[0, half * tile:(half + 1) * tile, :] = jnp.dot(h, wd_ref[0, 0], preferred_element_type=F32).astype(BF16)

    @pl.when(step == 0)
    def _():
        issue(0, xa, 0)

    wait(xa, 0)
    issue(2 * step + 1, xb, 1)
    ffn(xa, 0)
    nxt = jnp.where(step + 1 < n_steps, 2 * step + 2, 0)
    issue(nxt, xa, 0)
    wait(xb, 1)
    ffn(xb, 1)

    @pl.when(step == n_steps - 1)
    def _():
        wait(xa, 0)


def _moe_ffn(x2d, idx, wg, wu, wd, layer):
    n, d = x2d.shape
    _, n_exp, _, f = wg.shape
    cap = idx.shape[0] // n_exp
    tile = _pick(cap // 2, FFN_TILE)
    steps_per_exp = cap // (2 * tile)
    kern = functools.partial(_ffn_kernel, tile=tile, n_steps=n_exp * steps_per_exp)
    grid_spec = pltpu.PrefetchScalarGridSpec(
        num_scalar_prefetch=1,
        grid=(n_exp, steps_per_exp),
        in_specs=[pl.BlockSpec(memory_space=pl.ANY),
                  pl.BlockSpec((1, 1, d, f), lambda e, t, ix: (layer, e, 0, 0)),
                  pl.BlockSpec((1, 1, d, f), lambda e, t, ix: (layer, e, 0, 0)),
                  pl.BlockSpec((1, 1, f, d), lambda e, t, ix: (layer, e, 0, 0))],
        out_specs=pl.BlockSpec((1, 2 * tile, d), lambda e, t, ix: (e, t, 0)),
        scratch_shapes=[pltpu.VMEM((tile, d), F32), pltpu.VMEM((tile, d), F32), pltpu.SemaphoreType.DMA((2,))],
    )
    return pl.pallas_call(
        kern,
        grid_spec=grid_spec,
        out_shape=jax.ShapeDtypeStruct((n_exp, cap, d), BF16),
        compiler_params=_params("arbitrary", "arbitrary"),
        name="moe_ffn",
    )(idx, x2d, wg, wu, wd)


def _invert_kernel(st_ref, en_ref, post_ref, acc_ref, *, n_exp, win, cap_tot, tokens):
    i = pl.program_id(0)

    @pl.when(i == 0)
    def _():
        acc_ref[...] = jnp.zeros_like(acc_ref)

    t = tokens
    tok = i * t + lax.broadcasted_iota(jnp.int32, (t, LANES), 0)
    lane = lax.broadcasted_iota(jnp.int32, (t, LANES), 1)
    digits = jnp.where(lane == 0, lax.shift_right_logical(tok, 8), jnp.where(lane == 1, tok & 255, 0))
    digits = digits.astype(F32).astype(BF16)
    post = post_ref[...]
    row = lax.broadcasted_iota(jnp.int32, (win, t), 0)

    def window(e):
        s0 = st_ref[i * n_exp + e]
        return jnp.minimum(lax.shift_left(lax.shift_right_logical(s0, 4), 4), cap_tot - win)

    def place(e, w, r):
        w = pl.multiple_of(w, BF16_ROWS)
        acc_ref[pl.ds(w, win), :] += r if e == 0 else pltpu.roll(r, 2 * e, 1)

    ws = [window(e) for e in range(n_exp)]
    onehots = jnp.concatenate([jnp.where(post[e:e + 1, :] - ws[e] == row, 1.0, 0.0).astype(BF16)
                               for e in range(n_exp)], axis=0)
    res = jnp.dot(onehots, digits, preferred_element_type=F32)
    for e in range(n_exp):
        place(e, ws[e], res[e * win:(e + 1) * win])

    for e in range(n_exp):
        w = ws[e]
        pe = post[e:e + 1, :]
        s1 = en_ref[i * n_exp + e]
        n_extra = jnp.maximum(s1 - w - 1, 0) // win

        def extra(k, carry, e=e, w=w, pe=pe):
            lo = w + win * (k + 1)
            wk = jnp.minimum(lo, cap_tot - win)
            oh = jnp.where((pe - wk == row) & (pe >= lo), 1.0, 0.0).astype(BF16)
            place(e, wk, jnp.dot(oh, digits, preferred_element_type=F32))
            return carry

        lax.fori_loop(0, n_extra, extra, 0)


def _ec_invert(pos, starts, ends, cap_tot):
    n, n_exp = pos.shape
    t = COMBINE_TOKENS
    win = COMBINE_WINDOW
    assert 2 * n_exp <= LANES and n < 256 * 256
    kern = functools.partial(_invert_kernel, n_exp=n_exp, win=win, cap_tot=cap_tot, tokens=t)
    grid_spec = pltpu.PrefetchScalarGridSpec(
        num_scalar_prefetch=2,
        grid=(n // t,),
        in_specs=[pl.BlockSpec((n_exp, t), lambda i, st, en: (0, i))],
        out_specs=pl.BlockSpec((cap_tot, LANES), lambda i, st, en: (0, 0)),
    )
    acc = pl.pallas_call(
        kern,
        grid_spec=grid_spec,
        out_shape=jax.ShapeDtypeStruct((cap_tot, LANES), F32),
        compiler_params=_params("arbitrary"),
        name="ec_invert",
    )(starts, ends, pos.T)
    digits = acc[:, :2 * n_exp].astype(jnp.int32).reshape(cap_tot, n_exp, 2)
    return (digits[:, :, 0] * 256 + digits[:, :, 1]).T.reshape(-1)


SELECT_ROW_TILE = 512


def _select_kernel(aff_ref, pos_ref, cnt_ref, *, cap, n_exp):
    a = aff_ref[...]
    r = a.shape[0]
    bits = pltpu.bitcast(a, jnp.int32)

    def fold(v):
        sh = n_exp
        while sh < LANES:
            v = v + pltpu.roll(v, sh, 1)
            sh *= 2
        return v

    def count(mask):
        return fold(jnp.sum(jnp.where(mask, 1.0, 0.0), axis=0, keepdims=True))

    def search(i, thr):
        cand = thr | jnp.left_shift(jnp.int32(1), 30 - i)
        return jnp.where(count(bits >= cand) >= cap, cand, thr)

    thr = lax.fori_loop(0, 31, search, jnp.zeros((1, LANES), jnp.int32))
    above = bits > thr
    tied = bits == thr
    need = cap - count(above)

    li = lax.broadcasted_iota(jnp.int32, (LANES, 2 * LANES), 0)
    ci = lax.broadcasted_iota(jnp.int32, (LANES, 2 * LANES), 1)
    same_exp = (li & (n_exp - 1)) == (ci & (n_exp - 1))
    earlier = (li // n_exp) < ((ci & (LANES - 1)) // n_exp)
    w2 = jnp.where(same_exp & ((ci >= LANES) | earlier), 1.0, 0.0).astype(BF16)
    tr = min(SELECT_ROW_TILE, r)
    rr = lax.broadcasted_iota(jnp.int32, (tr, tr), 0)
    rc = lax.broadcasted_iota(jnp.int32, (tr, tr), 1)
    rows_before = jnp.where(rr > rc, 1.0, 0.0).astype(BF16)

    def prefix(mask):
        lw = jnp.dot(jnp.where(mask, 1.0, 0.0).astype(BF16), w2, preferred_element_type=F32)
        within, row_tot = lw[:, :LANES], lw[:, LANES:]
        carry = jnp.zeros((1, LANES), F32)
        outs = []
        for t in range(r // tr):
            rt = row_tot[t * tr:(t + 1) * tr]
            outs.append(jnp.dot(rows_before, rt.astype(BF16), preferred_element_type=F32) + carry
                        + within[t * tr:(t + 1) * tr])
            carry = carry + jnp.sum(rt, axis=0, keepdims=True)
        return jnp.concatenate(outs, axis=0)

    sel = above | (tied & (prefix(tied) < need))
    cnt = prefix(sel).astype(jnp.int32)
    pos_ref[...] = jnp.where(sel, cnt, -1)
    cnt_ref[...] = cnt


def _ec_select(aff_group, cap):
    n_g, n_exp = aff_group.shape
    assert LANES % n_exp == 0 and (n_exp & (n_exp - 1)) == 0
    r = n_g * n_exp // LANES
    assert r % min(SELECT_ROW_TILE, r) == 0
    shp = jax.ShapeDtypeStruct((r, LANES), jnp.int32)
    pos, cnt = pl.pallas_call(
        functools.partial(_select_kernel, cap=cap, n_exp=n_exp),
        out_shape=[shp, shp],
        compiler_params=pltpu.CompilerParams(vmem_limit_bytes=V7X_VMEM_LIMIT_BYTES),
        name="ec_select",
    )(aff_group.reshape(r, LANES))
    return pos.reshape(n_g, n_exp), cnt.reshape(n_g, n_exp)


COMBINE_TOKENS = 256
COMBINE_WINDOW = 64
BF16_ROWS = 16


def _combine_kernel(st_ref, en_ref, x_ref, pos_ref, aff_ref, g_ref, b_ref, o_hbm, *rest, alpha, n_exp, win, cap_tot,
                    n_tiles, split_tiles):
    n_out = 1 if split_tiles is None else 2
    out_refs = rest[:n_out]
    buf, sem, xbuf, xsem, acc_ref = rest[n_out:]
    i = pl.program_id(0)
    slot = i % 2

    def window(tile, e):
        s0 = st_ref[tile * n_exp + e]
        return jnp.minimum(lax.shift_left(lax.shift_right_logical(s0, 4), 4), cap_tot - win)

    def fetch(tile, sl, e):
        w = pl.multiple_of(window(tile, e), BF16_ROWS)
        return pltpu.make_async_copy(o_hbm.at[e, pl.ds(w, win), :], buf.at[sl, pl.ds(e * win, win), :], sem.at[sl, e])

    @pl.when(i == 0)
    def _():
        for e in range(n_exp):
            fetch(0, 0, e).start()

    @pl.when(i + 1 < n_tiles)
    def _():
        for e in range(n_exp):
            fetch(i + 1, 1 - slot, e).start()

    pos = pos_ref[...]
    aff = aff_ref[...]
    t = pos.shape[0]
    lane = lax.broadcasted_iota(jnp.int32, (t, 2 * win), 1)
    first = lane < win
    lane_in = jnp.where(first, lane, lane - win)
    parts = []
    for e in range(0, n_exp, 2):
        rel = jnp.where(first, pos[:, e:e + 1] - window(i, e), pos[:, e + 1:e + 2] - window(i, e + 1))
        gate = jnp.where(first, aff[:, e:e + 1], aff[:, e + 1:e + 2])
        parts.append(jnp.where(rel == lane_in, gate, 0.0))
    pmat = jnp.concatenate(parts, axis=1).astype(BF16)
    for e in range(n_exp):
        fetch(i, slot, e).wait()
    acc_ref[...] = jnp.dot(pmat, buf[slot], preferred_element_type=F32)

    lane1 = lax.broadcasted_iota(jnp.int32, (t, win), 1)
    for e in range(n_exp):
        w = window(i, e)
        s1 = en_ref[i * n_exp + e]
        n_extra = jnp.maximum(s1 - w - 1, 0) // win

        def extra(k, carry, e=e, w=w):
            lo = w + win * (k + 1)
            wk = pl.multiple_of(jnp.minimum(lo, cap_tot - win), BF16_ROWS)
            cp = pltpu.make_async_copy(o_hbm.at[e, pl.ds(wk, win), :], xbuf, xsem)
            cp.start()
            cp.wait()
            pe = pos[:, e:e + 1]
            oh = jnp.where((pe - wk == lane1) & (pe >= lo), 1.0, 0.0).astype(BF16)
            acc_ref[...] += jnp.dot(oh, xbuf[...], preferred_element_type=F32) * aff[:, e:e + 1]
            return carry

        lax.fori_loop(0, n_extra, extra, 0)
    y = _res_ln(x_ref[...], acc_ref[...], g_ref[...], b_ref[...], alpha)
    if split_tiles is None:
        out_refs[0][...] = y
    else:
        @pl.when(i < split_tiles)
        def _():
            out_refs[0][...] = y

        @pl.when(i >= split_tiles)
        def _():
            out_refs[1][...] = y


def _moe_combine_ln(x2d, pos, aff, starts, ends, o, g, b, alpha, split=None):
    n, d = x2d.shape
    n_exp, cap_tot, _ = o.shape
    t = COMBINE_TOKENS
    win = COMBINE_WINDOW
    assert n % t == 0 and cap_tot % BF16_ROWS == 0 and cap_tot >= win and n_exp % 2 == 0 and 2 * win == LANES
    n_tiles = n // t
    row = pl.BlockSpec((1, d), lambda i, st, en: (0, 0))
    tok = lambda w: pl.BlockSpec((t, w), lambda i, st, en: (i, 0))
    if split is None:
        split_tiles = None
        out_specs = tok(d)
        out_shape = jax.ShapeDtypeStruct((n, d), F32)
    else:
        assert split % t == 0 and 0 < split < n
        split_tiles = split // t
        out_specs = [pl.BlockSpec((t, d), lambda i, st, en: (jnp.minimum(i, split_tiles - 1), 0)),
                     pl.BlockSpec((t, d), lambda i, st, en: (jnp.maximum(i - split_tiles, 0), 0))]
        out_shape = [jax.ShapeDtypeStruct((split, d), F32), jax.ShapeDtypeStruct((n - split, d), F32)]
    kern = functools.partial(_combine_kernel, alpha=alpha, n_exp=n_exp, win=win, cap_tot=cap_tot, n_tiles=n_tiles,
                             split_tiles=split_tiles)
    grid_spec = pltpu.PrefetchScalarGridSpec(
        num_scalar_prefetch=2,
        grid=(n_tiles,),
        in_specs=[tok(d), tok(n_exp), tok(n_exp), row, row, pl.BlockSpec(memory_space=pl.ANY)],
        out_specs=out_specs,
        scratch_shapes=[pltpu.VMEM((2, n_exp * win, d), BF16), pltpu.SemaphoreType.DMA((2, n_exp)),
                        pltpu.VMEM((win, d), BF16), pltpu.SemaphoreType.DMA(()), pltpu.VMEM((t, d), F32)],
    )
    return pl.pallas_call(
        kern,
        grid_spec=grid_spec,
        out_shape=out_shape,
        compiler_params=_params("arbitrary"),
        name="moe_combine_ln",
    )(starts, ends, x2d, pos, aff, g.reshape(1, d), b.reshape(1, d), o)


def _ec_moe_ln(x2d, aff, groups, wg, wu, wd, layer, g, b, alpha, split=None):
    n, d = x2d.shape
    n_exp = aff.shape[1]
    t = COMBINE_TOKENS
    pos_l, st_l, en_l = [], [], []
    off = 0
    for start, cnt_tok in groups:
        assert start % t == 0 and cnt_tok % t == 0
        cap = EC_CAPACITY_FACTOR * cnt_tok // n_exp
        pos, cnt = _ec_select(aff[start:start + cnt_tok], cap)
        st = cnt[::t] + off
        en = jnp.concatenate([st[1:], jnp.full((1, n_exp), off + cap, jnp.int32)], axis=0)
        pos_l.append(jnp.where(pos >= 0, pos + off, -1))
        st_l.append(st)
        en_l.append(en)
        off += cap
    cap_tot = off
    pos = jnp.concatenate(pos_l, axis=0)
    starts = jnp.concatenate(st_l, axis=0).reshape(-1)
    ends = jnp.concatenate(en_l, axis=0).reshape(-1)
    idx = _ec_invert(pos, starts, ends, cap_tot)
    o = _moe_ffn(x2d, idx, wg, wu, wd, layer)
    return _moe_combine_ln(x2d, pos, aff, starts, ends, o, g, b, alpha, split)


def _split2_bf16(w):
    hi = w.astype(BF16)
    lo = (w - hi.astype(F32)).astype(BF16)
    return jnp.concatenate([hi, lo], axis=1)


def kernel(x_prompt, x_sample, attn_w_qkv, attn_q_norm, attn_k_norm, attn_w_o, pool_w, pool_scale, ssd_w_in,
           ssd_conv_w, ssd_conv_b, ssd_dt_bias, ssd_A_log, ssd_D, ssd_norm, ssd_w_out, moe_w_router, moe_w_gate,
           moe_w_up, moe_w_down, ln_g, ln_b):
    bp, s, d = x_prompt.shape
    bs = x_sample.shape[0]
    assert x_sample.shape[1] == s
    bsz = bp + bs
    n = bsz * s
    groups = [(0, bp * s), (bp * s, bs * s)]
    depth = ln_g.shape[0]
    alpha = (2 * depth) ** 0.25
    n_heads = attn_w_o.shape[1] // HEAD_DIM
    n_kv = (attn_w_qkv.shape[2] // HEAD_DIM - n_heads) // 2
    d_inner = ssd_w_out.shape[1]
    n_ssd_heads = ssd_A_log.shape[-1]
    assert d_inner == n_ssd_heads * SSD_HEAD_DIM
    conv_dim = ssd_conv_w.shape[2]
    n_groups = (conv_dim - d_inner) // (2 * D_STATE)
    rope = _rope_tables(s)

    x = jnp.concatenate([x_prompt, x_sample], axis=0).reshape(n, d)
    wg_all, wu_all, wd_all = moe_w_gate.astype(BF16), moe_w_up.astype(BF16), moe_w_down.astype(BF16)
    ia = ip = isd = 0
    for i in range(depth):
        wr2 = _split2_bf16(moe_w_router[i])
        g1, b1, g2, b2 = ln_g[i, 0], ln_b[i, 0], ln_g[i, 1], ln_b[i, 1]
        kind = i % 3
        if kind == 0:
            w_qkv = _permute_qk_columns(attn_w_qkv[ia], n_heads, n_kv).astype(BF16)
            qkv = _qkv_proj(x, w_qkv, attn_q_norm[ia], attn_k_norm[ia], rope, s, n_heads, n_kv)
            qkv3 = qkv.reshape(bsz, s, -1)
            vt = jnp.swapaxes(qkv3[:, :, (n_heads + n_kv) * HEAD_DIM:], 1, 2).reshape(bsz, n_kv, HEAD_DIM, s)
            o = _flash_attention(qkv3, vt, n_heads, n_kv)
            x, aff = _mm_res_ln_router(o.reshape(n, -1), attn_w_o[ia].astype(BF16), x, g1, b1, wr2, alpha)
            ia += 1
        elif kind == 1:
            x3, aff3 = _pool_layer(x.reshape(bsz, s, d), pool_w[ip].astype(BF16), pool_scale[ip], g1, b1, wr2, alpha)
            x, aff = x3.reshape(n, d), aff3.reshape(n, -1)
            ip += 1
        else:
            zx = _matmul_f32(x, ssd_w_in[isd].astype(BF16), 1152)
            zx3 = zx.reshape(bsz, s, -1)
            xbc = _ssd_conv(zx3, ssd_conv_w[isd], ssd_conv_b[isd], d_inner, conv_dim)
            a_neg = -jnp.exp(ssd_A_log[isd].astype(F32)).reshape(-1)
            dt, ecs, dt_t, ecs_t, tot = _ssd_dt(zx3, ssd_dt_bias[isd].reshape(-1), a_neg, d_inner + conv_dim)
            yf, yb = _ssd_scan(xbc, dt, ecs, dt_t, ecs_t, tot, ssd_D[isd], d_inner, n_groups)
            yn = _ssd_gate(yf.reshape(n, d_inner), yb.reshape(n, d_inner), zx, ssd_norm[isd])
            x, aff = _mm_res_ln_router(yn, ssd_w_out[isd].astype(BF16), x, g1, b1, wr2, alpha)
            isd += 1
        x = _ec_moe_ln(x, aff, groups, wg_all, wu_all, wd_all, i, g2, b2, alpha,
                       split=bp * s if i == depth - 1 else None)
    y_prompt, y_sample = x
    return y_prompt.reshape(bp, s, d), y_sample.reshape(bs, s, d)
```

```python
import functools
import math

import jax
import jax.numpy as jnp
from jax import lax
from jax.experimental import pallas as pl
from jax.experimental.pallas import tpu as pltpu

F32 = jnp.float32
BF16 = jnp.bfloat16

HEAD_DIM = 128
GRID_W = 64
ROPE_THETA = 10000.0
POOL_WINDOWS = (2, 4, 8, 16)
POOL_HALO = 8
D_STATE = 128
SSD_CHUNK = 128
SSD_HEAD_DIM = 64
SSD_HEADS_PER_GROUP = 8
D_CONV = 4
CONV_LEFT = D_CONV // 2
EC_CAPACITY_FACTOR = 2
LN_EPS = 1e-5
RMS_EPS = 1e-6
LOG2E = 1.4426950408889634

V7X_VMEM_LIMIT_BYTES = 52 * 1024 * 1024
LANES = 128


def _params(*sem):
    return pltpu.CompilerParams(dimension_semantics=sem, vmem_limit_bytes=V7X_VMEM_LIMIT_BYTES)


def _pick(n, pref):
    t = min(n, pref)
    while n % t:
        t //= 2
    return t


def _res_ln(x, h, g, b, alpha):
    y = alpha * x + h
    mu = jnp.mean(y, axis=-1, keepdims=True)
    yc = y - mu
    var = jnp.mean(yc * yc, axis=-1, keepdims=True)
    return yc * lax.rsqrt(var + LN_EPS) * g + b


def _router_affinity(xn, wr_ref, n_exp):
    xh = xn.astype(BF16)
    xl = (xn - xh.astype(F32)).astype(BF16)
    wr = wr_ref[...]
    r1 = jnp.dot(xh, wr, preferred_element_type=F32)
    r2 = jnp.dot(xl, wr[:, :n_exp], preferred_element_type=F32)
    logits = r1[:, :n_exp] + (r1[:, n_exp:] + r2)
    m = jnp.max(logits, axis=-1, keepdims=True)
    e = jnp.exp(logits - m)
    return e / jnp.sum(e, axis=-1, keepdims=True)


def _qkv_kernel(x_ref, w_ref, cos_ref, sin_ref, qn_ref, kn_ref, o_ref, xb_ref, *,
                n_q_tiles, n_k_tiles, heads_per_tile, q_scale):
    j = pl.program_id(1)

    @pl.when(j == 0)
    def _():
        xb_ref[...] = x_ref[...].astype(BF16)

    acc = jnp.dot(xb_ref[...], w_ref[...], preferred_element_type=F32)

    def norm_rope(gain_ref, scale):
        cos = cos_ref[...]
        sin = sin_ref[...]
        g = gain_ref[...]
        for h in range(heads_per_tile):
            a = acc[:, h * HEAD_DIM:(h + 1) * HEAD_DIM]
            a = a * lax.rsqrt(jnp.mean(a * a, axis=-1, keepdims=True) + RMS_EPS) * g
            r = a * cos + pltpu.roll(a, HEAD_DIM // 2, 1) * sin
            o_ref[:, h * HEAD_DIM:(h + 1) * HEAD_DIM] = (r * scale).astype(BF16)

    @pl.when(j < n_q_tiles)
    def _():
        norm_rope(qn_ref, q_scale)

    @pl.when((j >= n_q_tiles) & (j < n_q_tiles + n_k_tiles))
    def _():
        norm_rope(kn_ref, 1.0)

    @pl.when(j >= n_q_tiles + n_k_tiles)
    def _():
        o_ref[...] = acc.astype(BF16)


def _rope_perm():
    quarter = HEAD_DIM // 4
    blocks = (0, 2, 1, 3)
    return jnp.concatenate([jnp.arange(quarter) + b * quarter for b in blocks])


def _rope_tables(seq_len):
    rows = seq_len // GRID_W
    row = jnp.repeat(jnp.arange(rows, dtype=F32), GRID_W)
    col = jnp.tile(jnp.arange(GRID_W, dtype=F32), rows)
    inv_freq = ROPE_THETA ** (-jnp.arange(0, HEAD_DIM // 2, 2, dtype=F32) / (HEAD_DIM // 2))
    ang = jnp.concatenate([row[:, None] * inv_freq, col[:, None] * inv_freq], axis=-1)
    cos = jnp.concatenate([jnp.cos(ang), jnp.cos(ang)], axis=-1)
    sin = jnp.concatenate([-jnp.sin(ang), jnp.sin(ang)], axis=-1)
    return cos, sin


def _permute_qk_columns(w_qkv, n_heads, n_kv):
    d = w_qkv.shape[0]
    n_qk = n_heads + n_kv
    qk = w_qkv[:, :n_qk * HEAD_DIM].reshape(d, n_qk, HEAD_DIM)[:, :, _rope_perm()].reshape(d, n_qk * HEAD_DIM)
    return jnp.concatenate([qk, w_qkv[:, n_qk * HEAD_DIM:]], axis=1)


def _qkv_proj(x2d, w_bf16, q_norm, k_norm, rope, seq_len, n_heads, n_kv):
    n, d = x2d.shape
    qkv_dim = w_bf16.shape[1]
    tn = n_kv * HEAD_DIM
    tm = _pick(seq_len, 1024)
    cos, sin = rope
    perm = _rope_perm()
    q_norm, k_norm = q_norm[perm], k_norm[perm]
    nsb = seq_len // tm
    kern = functools.partial(
        _qkv_kernel, n_q_tiles=n_heads // n_kv, n_k_tiles=1, heads_per_tile=n_kv,
        q_scale=HEAD_DIM ** -0.5 * LOG2E)
    tab = pl.BlockSpec((tm, HEAD_DIM), lambda i, j: (i % nsb, 0))
    vec = pl.BlockSpec((1, HEAD_DIM), lambda i, j: (0, 0))
    return pl.pallas_call(
        kern,
        grid=(n // tm, qkv_dim // tn),
        in_specs=[pl.BlockSpec((tm, d), lambda i, j: (i, 0)),
                  pl.BlockSpec((d, tn), lambda i, j: (0, j)),
                  tab, tab, vec, vec],
        out_specs=pl.BlockSpec((tm, tn), lambda i, j: (i, j)),
        out_shape=jax.ShapeDtypeStruct((n, qkv_dim), BF16),
        scratch_shapes=[pltpu.VMEM((tm, d), BF16)],
        compiler_params=_params("parallel", "arbitrary"),
        name="qkv_proj",
    )(x2d, w_bf16, cos, sin, q_norm.reshape(1, HEAD_DIM), k_norm.reshape(1, HEAD_DIM))


FLASH_TQ = 128
FLASH_TK = 2048


def _flash_kernel(q_ref, k_ref, vt_ref, o_ref, s_scr, p_scr, acc_scr, *, tk, group):
    tq = q_ref.shape[1]
    seq = k_ref.shape[1]
    q = jnp.concatenate([q_ref[0, :, g * HEAD_DIM:(g + 1) * HEAD_DIM] for g in range(group)], axis=0)
    rows = group * tq
    nc = seq // tk

    def scores(c, slot):
        k = k_ref[0, pl.ds(c * tk, tk), :]
        s_scr[slot] = lax.dot_general(k, q, (((1,), (1,)), ((), ())), preferred_element_type=F32)

    def pv(c, slot, alpha):
        vt = vt_ref[0, 0, :, pl.ds(c * tk, tk)]
        acc_scr[...] = acc_scr[...] * alpha + jnp.dot(vt, p_scr[slot], preferred_element_type=F32)

    def softmax(slot, m, l):
        s = s_scr[slot]
        m_new = jnp.maximum(m, jnp.max(s, axis=0, keepdims=True))
        alpha = jnp.exp2(m - m_new)
        p = jnp.exp2(s - m_new)
        l = alpha * l + jnp.sum(p, axis=0, keepdims=True)
        p_scr[slot] = p.astype(BF16)
        return m_new, l, alpha

    m = jnp.full((1, rows), -jnp.inf, F32)
    l = jnp.zeros((1, rows), F32)
    acc_scr[...] = jnp.zeros_like(acc_scr)
    scores(0, 0)
    if nc > 1:
        scores(1, 1)
    m, l, alpha = softmax(0, m, l)
    for c in range(1, nc):
        if c + 1 < nc:
            scores(c + 1, (c + 1) % 2)
        pv(c - 1, (c - 1) % 2, alpha)
        m, l, alpha = softmax(c % 2, m, l)
    pv(nc - 1, (nc - 1) % 2, alpha)
    o = (acc_scr[...] / l).T
    for g in range(group):
        o_ref[0, :, g * HEAD_DIM:(g + 1) * HEAD_DIM] = o[g * tq:(g + 1) * tq].astype(BF16)


def _flash_attention(qkv, vt, n_heads, n_kv):
    b, s, _ = qkv.shape
    group = n_heads // n_kv
    tq = _pick(s, FLASH_TQ)
    tk = _pick(s, FLASH_TK)
    gw = group * HEAD_DIM
    rows = group * tq
    vrows = vt.shape[2]
    kern = functools.partial(_flash_kernel, tk=tk, group=group)
    return pl.pallas_call(
        kern,
        grid=(b, n_kv, s // tq),
        in_specs=[pl.BlockSpec((1, tq, gw), lambda bi, h, i: (bi, i, h)),
                  pl.BlockSpec((1, s, HEAD_DIM), lambda bi, h, i: (bi, 0, n_heads + h)),
                  pl.BlockSpec((1, 1, vrows, s), lambda bi, h, i: (bi, h, 0, 0))],
        out_specs=pl.BlockSpec((1, tq, gw), lambda bi, h, i: (bi, i, h)),
        out_shape=jax.ShapeDtypeStruct((b, s, n_heads * HEAD_DIM), BF16),
        scratch_shapes=[pltpu.VMEM((2, tk, rows), F32), pltpu.VMEM((2, tk, rows), BF16),
                        pltpu.VMEM((vrows, rows), F32)],
        compiler_params=_params("parallel", "parallel", "arbitrary"),
        name="flash_attention",
    )(qkv, qkv, vt)


MM_LN_TM = 512
MM_LN_TK = 2048


def _mm_ln_kernel(a_ref, w_ref, x_ref, g_ref, b_ref, wr_ref, o_ref, aff_ref, acc_ref, *, alpha, nk, n_exp):
    k = pl.program_id(1)
    part = jnp.dot(a_ref[...], w_ref[...], preferred_element_type=F32)

    def finish(h):
        xn = _res_ln(x_ref[...], h, g_ref[...], b_ref[...], alpha)
        o_ref[...] = xn
        aff_ref[...] = _router_affinity(xn, wr_ref, n_exp)

    if nk == 1:
        finish(part)
        return

    @pl.when(k == 0)
    def _():
        acc_ref[...] = part

    @pl.when((k > 0) & (k < nk - 1))
    def _():
        acc_ref[...] += part

    @pl.when(k == nk - 1)
    def _():
        finish(acc_ref[...] + part)


def _mm_res_ln_router(a_bf16, w_bf16, x2d, g, b, wr2, alpha):
    n, kdim = a_bf16.shape
    d = w_bf16.shape[1]
    n_exp = wr2.shape[1] // 2
    tm = _pick(n, MM_LN_TM)
    tk = _pick(kdim, MM_LN_TK)
    nk = kdim // tk
    kern = functools.partial(_mm_ln_kernel, alpha=alpha, nk=nk, n_exp=n_exp)
    row = pl.BlockSpec((1, d), lambda i, k: (0, 0))
    return pl.pallas_call(
        kern,
        grid=(n // tm, nk),
        in_specs=[pl.BlockSpec((tm, tk), lambda i, k: (i, k)),
                  pl.BlockSpec((tk, d), lambda i, k: (k, 0)),
                  pl.BlockSpec((tm, d), lambda i, k: (i, 0)),
                  row, row,
                  pl.BlockSpec((d, 2 * n_exp), lambda i, k: (0, 0))],
        out_specs=[pl.BlockSpec((tm, d), lambda i, k: (i, 0)),
                   pl.BlockSpec((tm, n_exp), lambda i, k: (i, 0))],
        out_shape=[jax.ShapeDtypeStruct((n, d), F32), jax.ShapeDtypeStruct((n, n_exp), F32)],
        scratch_shapes=[pltpu.VMEM((tm, d), F32)],
        compiler_params=_params("parallel", "arbitrary"),
        name="mm_res_ln_router",
    )(a_bf16, w_bf16, x2d, g.reshape(1, d), b.reshape(1, d), wr2)


def _pool_kernel(prev_ref, cur_ref, next_ref, w_ref, sc_ref, g_ref, b_ref, wr_ref, o_ref, aff_ref, ext_ref, *,
                 alpha, nt, seq_len, n_exp):
    i = pl.program_id(1)
    t = cur_ref.shape[1]
    d = cur_ref.shape[2]
    pg = d // len(POOL_WINDOWS)
    x = cur_ref[0]
    ext_ref[0:POOL_HALO, :] = jnp.where(i == 0, 0.0, prev_ref[0])
    ext_ref[POOL_HALO:POOL_HALO + t, :] = x
    ext_ref[POOL_HALO + t:2 * POOL_HALO + t, :] = jnp.where(i == nt - 1, 0.0, next_ref[0])
    pos = i * t + lax.broadcasted_iota(jnp.int32, (t, 1), 0)
    hs = []
    fsum = ext_ref[...]
    rows = fsum.shape[0]
    width = 1
    for gi, w in enumerate(POOL_WINDOWS):
        half = w // 2
        cols = slice(gi * pg, (gi + 1) * pg)
        while width < w:
            fsum = fsum + pltpu.roll(fsum, rows - width, 0)
            width *= 2
        win = fsum[:, :pg]
        fsum = fsum[:, pg:]
        if half == POOL_HALO:
            acc = win[:t]
        else:
            acc = pltpu.roll(win, half, 0)[POOL_HALO:POOL_HALO + t]
        cnt = (jnp.minimum(pos + half, seq_len) - jnp.maximum(pos - half, 0)).astype(F32)
        mixed = (acc / cnt - x[:, cols]).astype(BF16)
        hs.append(jnp.dot(mixed, w_ref[gi], preferred_element_type=F32))
    h = jnp.concatenate(hs, axis=-1) * sc_ref[...]
    xn = _res_ln(x, h, g_ref[...], b_ref[...], alpha)
    o_ref[0] = xn
    aff_ref[0] = _router_affinity(xn, wr_ref, n_exp)


def _pool_layer(x3d, w_bf16, scale, g, b, wr2, alpha):
    bsz, s, d = x3d.shape
    n_exp = wr2.shape[1] // 2
    t = _pick(s, 512)
    nt = s // t
    hb = t // POOL_HALO
    last_hb = s // POOL_HALO - 1
    pg = d // len(POOL_WINDOWS)
    kern = functools.partial(_pool_kernel, alpha=alpha, nt=nt, seq_len=s, n_exp=n_exp)
    row = pl.BlockSpec((1, d), lambda bi, i: (0, 0))
    return pl.pallas_call(
        kern,
        grid=(bsz, nt),
        in_specs=[pl.BlockSpec((1, POOL_HALO, d), lambda bi, i: (bi, jnp.maximum(i * hb - 1, 0), 0)),
                  pl.BlockSpec((1, t, d), lambda bi, i: (bi, i, 0)),
                  pl.BlockSpec((1, POOL_HALO, d), lambda bi, i: (bi, jnp.minimum((i + 1) * hb, last_hb), 0)),
                  pl.BlockSpec((len(POOL_WINDOWS), pg, pg), lambda bi, i: (0, 0, 0)),
                  row, row, row,
                  pl.BlockSpec((d, 2 * n_exp), lambda bi, i: (0, 0))],
        out_specs=[pl.BlockSpec((1, t, d), lambda bi, i: (bi, i, 0)),
                   pl.BlockSpec((1, t, n_exp), lambda bi, i: (bi, i, 0))],
        out_shape=[jax.ShapeDtypeStruct((bsz, s, d), F32), jax.ShapeDtypeStruct((bsz, s, n_exp), F32)],
        scratch_shapes=[pltpu.VMEM((t + 2 * POOL_HALO, d), F32)],
        compiler_params=_params("parallel", "parallel"),
        name="pool_mixer",
    )(x3d, x3d, x3d, w_bf16, scale.reshape(1, d), g.reshape(1, d), b.reshape(1, d), wr2)


def _mm_kernel(x_ref, w_ref, o_ref, xb_ref):
    @pl.when(pl.program_id(1) == 0)
    def _():
        xb_ref[...] = x_ref[...].astype(BF16)

    o_ref[...] = jnp.dot(xb_ref[...], w_ref[...], preferred_element_type=F32)


def _matmul_f32(x2d, w_bf16, tn_pref):
    n, d = x2d.shape
    nout = w_bf16.shape[1]
    tm = _pick(n, 1024)
    tn = tn_pref
    assert nout % tn == 0
    return pl.pallas_call(
        _mm_kernel,
        grid=(n // tm, nout // tn),
        in_specs=[pl.BlockSpec((tm, d), lambda i, j: (i, 0)),
                  pl.BlockSpec((d, tn), lambda i, j: (0, j))],
        out_specs=pl.BlockSpec((tm, tn), lambda i, j: (i, j)),
        out_shape=jax.ShapeDtypeStruct((n, nout), F32),
        scratch_shapes=[pltpu.VMEM((tm, d), BF16)],
        compiler_params=_params("parallel", "arbitrary"),
        name="ssd_in_proj",
    )(x2d, w_bf16)


def _conv_kernel(prev_ref, cur_ref, next_ref, w_ref, b_ref, o_ref, ext_ref, *, nt):
    i = pl.program_id(1)
    t = cur_ref.shape[1]
    ext_ref[0:POOL_HALO, :] = jnp.where(i == 0, 0.0, prev_ref[0])
    ext_ref[POOL_HALO:POOL_HALO + t, :] = cur_ref[0]
    ext_ref[POOL_HALO + t:2 * POOL_HALO + t, :] = jnp.where(i == nt - 1, 0.0, next_ref[0])
    ext = ext_ref[...]
    rows = ext.shape[0]
    acc = None
    for kk in range(D_CONV):
        off = kk - CONV_LEFT
        tap = ext if off == 0 else pltpu.roll(ext, (-off) % rows, 0)
        term = tap[POOL_HALO:POOL_HALO + t, :] * w_ref[kk:kk + 1, :]
        acc = term if acc is None else acc + term
    acc = acc + b_ref[...]
    o_ref[0] = acc / (1.0 + jnp.exp(-acc))


def _ssd_conv(zx3d, conv_w, conv_b, d_inner, conv_dim):
    bsz, s, _ = zx3d.shape
    tc = 1024
    t = _pick(s, 512)
    nt = s // t
    hb = t // POOL_HALO
    last_hb = s // POOL_HALO - 1
    c0 = d_inner // tc
    return pl.pallas_call(
        functools.partial(_conv_kernel, nt=nt),
        grid=(bsz, nt, conv_dim // tc),
        in_specs=[pl.BlockSpec((1, POOL_HALO, tc), lambda bi, i, j: (bi, jnp.maximum(i * hb - 1, 0), c0 + j)),
                  pl.BlockSpec((1, t, tc), lambda bi, i, j: (bi, i, c0 + j)),
                  pl.BlockSpec((1, POOL_HALO, tc), lambda bi, i, j: (bi, jnp.minimum((i + 1) * hb, last_hb), c0 + j)),
                  pl.BlockSpec((D_CONV, tc), lambda bi, i, j: (0, j)),
                  pl.BlockSpec((1, tc), lambda bi, i, j: (0, j))],
        out_specs=pl.BlockSpec((1, t, tc), lambda bi, i, j: (bi, i, j)),
        out_shape=jax.ShapeDtypeStruct((bsz, s, conv_dim), F32),
        scratch_shapes=[pltpu.VMEM((t + 2 * POOL_HALO, tc), F32)],
        compiler_params=_params("parallel", "parallel", "parallel"),
        name="ssd_conv",
    )(zx3d, zx3d, zx3d, conv_w, conv_b.reshape(1, conv_dim))


def _split3(x):
    hi = x.astype(BF16)
    r1 = x - hi.astype(F32)
    mid = r1.astype(BF16)
    lo = (r1 - mid.astype(F32)).astype(BF16)
    return hi, mid, lo


def _dt_kernel(raw_ref, bias_ref, a_ref, dt_ref, e_ref, dtt_ref, et_ref, tot_ref):
    v = raw_ref[0] + bias_ref[...]
    dt = jnp.maximum(v, 0.0) + jnp.log1p(jnp.exp(-jnp.abs(v)))
    dt_ref[0] = dt
    dtt_ref[0] = dt.T
    a = dt * a_ref[...]
    q, w = a.shape
    li = lax.broadcasted_iota(jnp.int32, (q, q), 0)
    si = lax.broadcasted_iota(jnp.int32, (q, q), 1)
    tri = jnp.where(li >= si, 1.0, 0.0).astype(BF16)
    hi, mid, lo = _split3(a)
    cs = (jnp.dot(tri, lo, preferred_element_type=F32) + jnp.dot(tri, mid, preferred_element_type=F32)
          + jnp.dot(tri, hi, preferred_element_type=F32))
    lane = lax.broadcasted_iota(jnp.int32, (q, w), 1)
    e = jnp.where(lane < w // 2, cs, cs - a)
    e_ref[0] = e
    et_ref[0] = e.T
    tot_ref[0, 0] = cs[q - 1:q, :]


def _ssd_dt(zx3d, dt_bias, a_neg, col0):
    bsz, s, _ = zx3d.shape
    w = dt_bias.shape[-1]
    assert w == LANES and col0 % LANES == 0 and SSD_CHUNK == LANES
    nc = s // SSD_CHUNK
    blk = pl.BlockSpec((1, SSD_CHUNK, w), lambda bi, c: (bi, c, 0))
    blk_t = pl.BlockSpec((1, w, SSD_CHUNK), lambda bi, c: (bi, 0, c))
    row = pl.BlockSpec((1, w), lambda bi, c: (0, 0))
    return pl.pallas_call(
        _dt_kernel,
        grid=(bsz, nc),
        in_specs=[pl.BlockSpec((1, SSD_CHUNK, w), lambda bi, c: (bi, c, col0 // LANES)), row, row],
        out_specs=[blk, blk, blk_t, blk_t, pl.BlockSpec((1, 1, 1, w), lambda bi, c: (bi, c, 0, 0))],
        out_shape=[jax.ShapeDtypeStruct((bsz, s, w), F32), jax.ShapeDtypeStruct((bsz, s, w), F32),
                   jax.ShapeDtypeStruct((bsz, w, s), F32), jax.ShapeDtypeStruct((bsz, w, s), F32),
                   jax.ShapeDtypeStruct((bsz, nc, 1, w), F32)],
        compiler_params=_params("parallel", "parallel"),
        name="ssd_dt",
    )(zx3d, dt_bias.reshape(1, w), a_neg.reshape(1, w))


def _expand_heads(v, j0, width):
    m = v.shape[0]
    lane = lax.broadcasted_iota(jnp.int32, (m, LANES), 1)
    parts = []
    for pr in range(width // LANES):
        j = j0 + 2 * pr
        parts.append(jnp.where(lane < SSD_HEAD_DIM, v[:, j:j + 1], v[:, j + 1:j + 2]))
    return jnp.concatenate(parts, axis=1)


def _ssd_direction(x, bmat, cmat, dt_r, e_c, e_r, out_dec, st_w, chunk_dec, j0, st_ref, forward):
    q, width = x.shape
    li = lax.broadcasted_iota(jnp.int32, (q, q), 0)
    si = lax.broadcasted_iota(jnp.int32, (q, q), 1)
    lane = lax.broadcasted_iota(jnp.int32, (q, LANES), 1)
    mask = (li >= si) if forward else (si >= li)
    cb = lax.dot_general(cmat.astype(BF16), bmat.astype(BF16), (((1,), (1,)), ((), ())),
                         preferred_element_type=F32)
    xb = x.astype(BF16)
    y_parts = []
    for pr in range(width // LANES):
        ms = []
        for j in (j0 + 2 * pr, j0 + 2 * pr + 1):
            if forward:
                diff = e_c[:, j:j + 1] - e_r[j:j + 1, :]
            else:
                diff = e_r[j:j + 1, :] - e_c[:, j:j + 1]
            decay = jnp.exp(jnp.where(mask, diff, -jnp.inf))
            ms.append((decay * cb * dt_r[j:j + 1, :]).astype(BF16))
        xp = xb[:, pr * LANES:(pr + 1) * LANES]
        zero = jnp.zeros_like(xp)
        rhs = jnp.concatenate([jnp.where(lane < SSD_HEAD_DIM, xp, zero),
                               jnp.where(lane >= SSD_HEAD_DIM, xp, zero)], axis=0)
        y_parts.append(jnp.dot(jnp.concatenate(ms, axis=1), rhs, preferred_element_type=F32))
    y = jnp.concatenate(y_parts, axis=1)
    st = st_ref[...]
    y = y + jnp.dot(cmat.astype(BF16), st.astype(BF16), preferred_element_type=F32) * _expand_heads(out_dec, j0, width)
    xd = (x * _expand_heads(st_w, j0, width)).astype(BF16)
    st_new = lax.dot_general(bmat.astype(BF16), xd, (((0,), (0,)), ((), ())), preferred_element_type=F32)
    st_ref[...] = st * _expand_heads(chunk_dec, j0, width)[0:1, :] + st_new
    return y


def _ssd_scan_kernel(xf_ref, bf_ref, cf_ref, dtf_ref, ef_ref, dttf_ref, etf_ref, tf_ref,
                     xr_ref, br_ref, cr_ref, dtr_ref, er_ref, dttr_ref, etr_ref, tr_ref,
                     dskip_ref, yf_ref, yb_ref, stf_ref, stb_ref, *, n_groups, gw):
    @pl.when(pl.program_id(1) == 0)
    def _():
        stf_ref[...] = jnp.zeros_like(stf_ref)
        stb_ref[...] = jnp.zeros_like(stb_ref)

    hg = SSD_HEADS_PER_GROUP
    ef, tf = ef_ref[0], tf_ref[0, 0]
    out_dec_f = jnp.exp(ef)
    st_w_f = dtf_ref[0] * jnp.exp(tf - ef)
    chunk_dec_f = jnp.broadcast_to(jnp.exp(tf), (8, LANES))
    er, tr = er_ref[0], tr_ref[0, 0]
    out_dec_r = jnp.exp(tr - er)
    st_w_r = dtr_ref[0] * jnp.exp(er)
    chunk_dec_r = jnp.broadcast_to(jnp.exp(tr), (8, LANES))
    dttf, etf, dttr, etr = dttf_ref[0], etf_ref[0], dttr_ref[0], etr_ref[0]
    for gi in range(n_groups):
        xs = slice(gi * gw, (gi + 1) * gw)
        ns = slice(gi * D_STATE, (gi + 1) * D_STATE)
        xf = xf_ref[0, :, xs]
        yf = _ssd_direction(xf, bf_ref[0, :, ns], cf_ref[0, :, ns], dttf, ef, etf, out_dec_f, st_w_f, chunk_dec_f,
                            gi * hg, stf_ref.at[gi], True)
        yf_ref[0, :, xs] = yf + dskip_ref[:, xs] * xf
        yb_ref[0, :, xs] = _ssd_direction(xr_ref[0, :, xs], br_ref[0, :, ns], cr_ref[0, :, ns], dttr, er, etr,
                                          out_dec_r, st_w_r, chunk_dec_r, (n_groups + gi) * hg, stb_ref.at[gi],
                                          False)


def _ssd_scan(xbc, dt, ecs, dt_t, ecs_t, tot, d_skip, d_inner, n_groups):
    bsz, s, _ = xbc.shape
    nc = s // SSD_CHUNK
    gw = SSD_HEADS_PER_GROUP * SSD_HEAD_DIM
    w = dt.shape[-1]
    assert gw % LANES == 0 and d_inner == n_groups * gw and w == 2 * n_groups * SSD_HEADS_PER_GROUP == LANES
    gn = n_groups * D_STATE
    assert d_inner % gn == 0
    b0 = d_inner // gn
    dskip = jnp.repeat(d_skip.astype(F32), SSD_HEAD_DIM).reshape(1, d_inner)

    def specs(cidx):
        return [
            pl.BlockSpec((1, SSD_CHUNK, d_inner), lambda b, c: (b, cidx(c), 0)),
            pl.BlockSpec((1, SSD_CHUNK, gn), lambda b, c: (b, cidx(c), b0)),
            pl.BlockSpec((1, SSD_CHUNK, gn), lambda b, c: (b, cidx(c), b0 + 1)),
            pl.BlockSpec((1, SSD_CHUNK, w), lambda b, c: (b, cidx(c), 0)),
            pl.BlockSpec((1, SSD_CHUNK, w), lambda b, c: (b, cidx(c), 0)),
            pl.BlockSpec((1, w, SSD_CHUNK), lambda b, c: (b, 0, cidx(c))),
            pl.BlockSpec((1, w, SSD_CHUNK), lambda b, c: (b, 0, cidx(c))),
            pl.BlockSpec((1, 1, 1, w), lambda b, c: (b, cidx(c), 0, 0)),
        ]

    fwd = lambda c: c
    bwd = lambda c: nc - 1 - c
    y_shape = jax.ShapeDtypeStruct((bsz, s, d_inner), F32)
    return pl.pallas_call(
        functools.partial(_ssd_scan_kernel, n_groups=n_groups, gw=gw),
        grid=(bsz, nc),
        in_specs=specs(fwd) + specs(bwd) + [pl.BlockSpec((1, d_inner), lambda b, c: (0, 0))],
        out_specs=[pl.BlockSpec((1, SSD_CHUNK, d_inner), lambda b, c: (b, c, 0)),
                   pl.BlockSpec((1, SSD_CHUNK, d_inner), lambda b, c: (b, nc - 1 - c, 0))],
        out_shape=[y_shape, y_shape],
        scratch_shapes=[pltpu.VMEM((n_groups, D_STATE, gw), F32), pltpu.VMEM((n_groups, D_STATE, gw), F32)],
        compiler_params=_params("parallel", "arbitrary"),
        name="ssd_scan",
    )(xbc, xbc, xbc, dt, ecs, dt_t, ecs_t, tot,
      xbc, xbc, xbc, dt, ecs, dt_t, ecs_t, tot, dskip)


def _gate_kernel(yf_ref, yb_ref, z_ref, nw_ref, o_ref):
    z = z_ref[...]
    y = (yf_ref[...] + yb_ref[...]) * (z / (1.0 + jnp.exp(-z)))
    y = y * lax.rsqrt(jnp.mean(y * y, axis=-1, keepdims=True) + RMS_EPS) * nw_ref[...]
    o_ref[...] = y.astype(BF16)


def _ssd_gate(yf2d, yb2d, zx2d, norm_w):
    n, d_inner = yf2d.shape
    tm = _pick(n, 256)
    blk = pl.BlockSpec((tm, d_inner), lambda i: (i, 0))
    return pl.pallas_call(
        _gate_kernel,
        grid=(n // tm,),
        in_specs=[blk, blk, blk, pl.BlockSpec((1, d_inner), lambda i: (0, 0))],
        out_specs=blk,
        out_shape=jax.ShapeDtypeStruct((n, d_inner), BF16),
        compiler_params=_params("parallel"),
        name="ssd_gate_norm",
    )(yf2d, yb2d, zx2d, norm_w.reshape(1, d_inner))


FFN_TILE = 256


def _ffn_kernel(idx_ref, x_hbm, wg_ref, wu_ref, wd_ref, o_ref, xa, xb, sem, *, tile, n_steps):
    step = pl.program_id(0) * pl.num_programs(1) + pl.program_id(1)

    def issue(tile_idx, buf, s):
        base = tile_idx * tile
        for r in range(tile):
            tok = idx_ref[base + r]
            pltpu.make_async_copy(x_hbm.at[pl.ds(tok, 1), :], buf.at[pl.ds(r, 1), :], sem.at[s]).start(priority=r % 2)

    def wait(buf, s):
        pltpu.make_async_copy(x_hbm.at[pl.ds(0, tile), :], buf, sem.at[s]).wait()

    def ffn(buf, half):
        xs = buf[...].astype(BF16)
        hg = jnp.dot(xs, wg_ref[0, 0], preferred_element_type=F32)
        hu = jnp.dot(xs, wu_ref[0, 0], preferred_element_type=F32)
        h = (hg / (1.0 + jnp.exp(-hg)) * hu).astype(BF16)
        o_ref[0, half * tile:(half + 1) * tile, :] = jnp.dot(h, wd_ref[0, 0], preferred_element_type=F32).astype(BF16)

    @pl.when(step == 0)
    def _():
        issue(0, xa, 0)

    wait(xa, 0)
    issue(2 * step + 1, xb, 1)
    ffn(xa, 0)
    nxt = jnp.where(step + 1 < n_steps, 2 * step + 2, 0)
    issue(nxt, xa, 0)
    wait(xb, 1)
    ffn(xb, 1)

    @pl.when(step == n_steps - 1)
    def _():
        wait(xa, 0)


def _moe_ffn(x2d, idx, wg, wu, wd, layer):
    n, d = x2d.shape
    _, n_exp, _, f = wg.shape
    cap = idx.shape[0] // n_exp
    tile = _pick(cap // 2, FFN_TILE)
    steps_per_exp = cap // (2 * tile)
    kern = functools.partial(_ffn_kernel, tile=tile, n_steps=n_exp * steps_per_exp)
    grid_spec = pltpu.PrefetchScalarGridSpec(
        num_scalar_prefetch=1,
        grid=(n_exp, steps_per_exp),
        in_specs=[pl.BlockSpec(memory_space=pl.ANY),
                  pl.BlockSpec((1, 1, d, f), lambda e, t, ix: (layer, e, 0, 0)),
                  pl.BlockSpec((1, 1, d, f), lambda e, t, ix: (layer, e, 0, 0)),
                  pl.BlockSpec((1, 1, f, d), lambda e, t, ix: (layer, e, 0, 0))],
        out_specs=pl.BlockSpec((1, 2 * tile, d), lambda e, t, ix: (e, t, 0)),
        scratch_shapes=[pltpu.VMEM((tile, d), F32), pltpu.VMEM((tile, d), F32), pltpu.SemaphoreType.DMA((2,))],
    )
    return pl.pallas_call(
        kern,
        grid_spec=grid_spec,
        out_shape=jax.ShapeDtypeStruct((n_exp, cap, d), BF16),
        compiler_params=_params("arbitrary", "arbitrary"),
        name="moe_ffn",
    )(idx, x2d, wg, wu, wd)


def _invert_kernel(st_ref, en_ref, post_ref, acc_ref, *, n_exp, win, cap_tot, tokens):
    i = pl.program_id(0)

    @pl.when(i == 0)
    def _():
        acc_ref[...] = jnp.zeros_like(acc_ref)

    t = tokens
    tok = i * t + lax.broadcasted_iota(jnp.int32, (t, LANES), 0)
    lane = lax.broadcasted_iota(jnp.int32, (t, LANES), 1)
    digits = jnp.where(lane == 0, lax.shift_right_logical(tok, 8), jnp.where(lane == 1, tok & 255, 0))
    digits = digits.astype(F32).astype(BF16)
    post = post_ref[...]
    row = lax.broadcasted_iota(jnp.int32, (win, t), 0)

    def window(e):
        s0 = st_ref[i * n_exp + e]
        return jnp.minimum(lax.shift_left(lax.shift_right_logical(s0, 4), 4), cap_tot - win)

    def place(e, w, r):
        w = pl.multiple_of(w, BF16_ROWS)
        acc_ref[pl.ds(w, win), :] += r if e == 0 else pltpu.roll(r, 2 * e, 1)

    ws = [window(e) for e in range(n_exp)]
    onehots = jnp.concatenate([jnp.where(post[e:e + 1, :] - ws[e] == row, 1.0, 0.0).astype(BF16)
                               for e in range(n_exp)], axis=0)
    res = jnp.dot(onehots, digits, preferred_element_type=F32)
    for e in range(n_exp):
        place(e, ws[e], res[e * win:(e + 1) * win])

    for e in range(n_exp):
        w = ws[e]
        pe = post[e:e + 1, :]
        s1 = en_ref[i * n_exp + e]
        n_extra = jnp.maximum(s1 - w - 1, 0) // win

        def extra(k, carry, e=e, w=w, pe=pe):
            lo = w + win * (k + 1)
            wk = jnp.minimum(lo, cap_tot - win)
            oh = jnp.where((pe - wk == row) & (pe >= lo), 1.0, 0.0).astype(BF16)
            place(e, wk, jnp.dot(oh, digits, preferred_element_type=F32))
            return carry

        lax.fori_loop(0, n_extra, extra, 0)


def _ec_invert(pos, starts, ends, cap_tot):
    n, n_exp = pos.shape
    t = COMBINE_TOKENS
    win = COMBINE_WINDOW
    assert 2 * n_exp <= LANES and n < 256 * 256
    kern = functools.partial(_invert_kernel, n_exp=n_exp, win=win, cap_tot=cap_tot, tokens=t)
    grid_spec = pltpu.PrefetchScalarGridSpec(
        num_scalar_prefetch=2,
        grid=(n // t,),
        in_specs=[pl.BlockSpec((n_exp, t), lambda i, st, en: (0, i))],
        out_specs=pl.BlockSpec((cap_tot, LANES), lambda i, st, en: (0, 0)),
    )
    acc = pl.pallas_call(
        kern,
        grid_spec=grid_spec,
        out_shape=jax.ShapeDtypeStruct((cap_tot, LANES), F32),
        compiler_params=_params("arbitrary"),
        name="ec_invert",
    )(starts, ends, pos.T)
    digits = acc[:, :2 * n_exp].astype(jnp.int32).reshape(cap_tot, n_exp, 2)
    return (digits[:, :, 0] * 256 + digits[:, :, 1]).T.reshape(-1)


SELECT_ROW_TILE = 512


def _select_kernel(aff_ref, pos_ref, cnt_ref, *, cap, n_exp):
    a = aff_ref[...]
    r = a.shape[0]
    bits = pltpu.bitcast(a, jnp.int32)

    def fold(v):
        sh = n_exp
        while sh < LANES:
            v = v + pltpu.roll(v, sh, 1)
            sh *= 2
        return v

    def count(mask):
        return fold(jnp.sum(jnp.where(mask, 1.0, 0.0), axis=0, keepdims=True))

    def search(i, thr):
        cand = thr | jnp.left_shift(jnp.int32(1), 30 - i)
        return jnp.where(count(bits >= cand) >= cap, cand, thr)

    thr = lax.fori_loop(0, 31, search, jnp.zeros((1, LANES), jnp.int32))
    above = bits > thr
    tied = bits == thr
    need = cap - count(above)

    li = lax.broadcasted_iota(jnp.int32, (LANES, 2 * LANES), 0)
    ci = lax.broadcasted_iota(jnp.int32, (LANES, 2 * LANES), 1)
    same_exp = (li & (n_exp - 1)) == (ci & (n_exp - 1))
    earlier = (li // n_exp) < ((ci & (LANES - 1)) // n_exp)
    w2 = jnp.where(same_exp & ((ci >= LANES) | earlier), 1.0, 0.0).astype(BF16)
    tr = min(SELECT_ROW_TILE, r)
    rr = lax.broadcasted_iota(jnp.int32, (tr, tr), 0)
    rc = lax.broadcasted_iota(jnp.int32, (tr, tr), 1)
    rows_before = jnp.where(rr > rc, 1.0, 0.0).astype(BF16)

    def prefix(mask):
        lw = jnp.dot(jnp.where(mask, 1.0, 0.0).astype(BF16), w2, preferred_element_type=F32)
        within, row_tot = lw[:, :LANES], lw[:, LANES:]
        carry = jnp.zeros((1, LANES), F32)
        outs = []
        for t in range(r // tr):
            rt = row_tot[t * tr:(t + 1) * tr]
            outs.append(jnp.dot(rows_before, rt.astype(BF16), preferred_element_type=F32) + carry
                        + within[t * tr:(t + 1) * tr])
            carry = carry + jnp.sum(rt, axis=0, keepdims=True)
        return jnp.concatenate(outs, axis=0)

    sel = above | (tied & (prefix(tied) < need))
    cnt = prefix(sel).astype(jnp.int32)
    pos_ref[...] = jnp.where(sel, cnt, -1)
    cnt_ref[...] = cnt


def _ec_select(aff_group, cap):
    n_g, n_exp = aff_group.shape
    assert LANES % n_exp == 0 and (n_exp & (n_exp - 1)) == 0
    r = n_g * n_exp // LANES
    assert r % min(SELECT_ROW_TILE, r) == 0
    shp = jax.ShapeDtypeStruct((r, LANES), jnp.int32)
    pos, cnt = pl.pallas_call(
        functools.partial(_select_kernel, cap=cap, n_exp=n_exp),
        out_shape=[shp, shp],
        compiler_params=pltpu.CompilerParams(vmem_limit_bytes=V7X_VMEM_LIMIT_BYTES),
        name="ec_select",
    )(aff_group.reshape(r, LANES))
    return pos.reshape(n_g, n_exp), cnt.reshape(n_g, n_exp)


COMBINE_TOKENS = 256
COMBINE_WINDOW = 64
BF16_ROWS = 16


def _combine_kernel(st_ref, en_ref, x_ref, pos_ref, aff_ref, g_ref, b_ref, o_hbm, *rest, alpha, n_exp, win, cap_tot,
                    n_tiles, split_tiles):
    n_out = 1 if split_tiles is None else 2
    out_refs = rest[:n_out]
    buf, sem, xbuf, xsem, acc_ref = rest[n_out:]
    i = pl.program_id(0)
    slot = i % 2

    def window(tile, e):
        s0 = st_ref[tile * n_exp + e]
        return jnp.minimum(lax.shift_left(lax.shift_right_logical(s0, 4), 4), cap_tot - win)

    def fetch(tile, sl, e):
        w = pl.multiple_of(window(tile, e), BF16_ROWS)
        return pltpu.make_async_copy(o_hbm.at[e, pl.ds(w, win), :], buf.at[sl, pl.ds(e * win, win), :], sem.at[sl, e])

    @pl.when(i == 0)
    def _():
        for e in range(n_exp):
            fetch(0, 0, e).start()

    @pl.when(i + 1 < n_tiles)
    def _():
        for e in range(n_exp):
            fetch(i + 1, 1 - slot, e).start()

    pos = pos_ref[...]
    aff = aff_ref[...]
    t = pos.shape[0]
    lane = lax.broadcasted_iota(jnp.int32, (t, 2 * win), 1)
    first = lane < win
    lane_in = jnp.where(first, lane, lane - win)
    parts = []
    for e in range(0, n_exp, 2):
        rel = jnp.where(first, pos[:, e:e + 1] - window(i, e), pos[:, e + 1:e + 2] - window(i, e + 1))
        gate = jnp.where(first, aff[:, e:e + 1], aff[:, e + 1:e + 2])
        parts.append(jnp.where(rel == lane_in, gate, 0.0))
    pmat = jnp.concatenate(parts, axis=1).astype(BF16)
    for e in range(n_exp):
        fetch(i, slot, e).wait()
    acc_ref[...] = jnp.dot(pmat, buf[slot], preferred_element_type=F32)

    lane1 = lax.broadcasted_iota(jnp.int32, (t, win), 1)
    for e in range(n_exp):
        w = window(i, e)
        s1 = en_ref[i * n_exp + e]
        n_extra = jnp.maximum(s1 - w - 1, 0) // win

        def extra(k, carry, e=e, w=w):
            lo = w + win * (k + 1)
            wk = pl.multiple_of(jnp.minimum(lo, cap_tot - win), BF16_ROWS)
            cp = pltpu.make_async_copy(o_hbm.at[e, pl.ds(wk, win), :], xbuf, xsem)
            cp.start()
            cp.wait()
            pe = pos[:, e:e + 1]
            oh = jnp.where((pe - wk == lane1) & (pe >= lo), 1.0, 0.0).astype(BF16)
            acc_ref[...] += jnp.dot(oh, xbuf[...], preferred_element_type=F32) * aff[:, e:e + 1]
            return carry

        lax.fori_loop(0, n_extra, extra, 0)
    y = _res_ln(x_ref[...], acc_ref[...], g_ref[...], b_ref[...], alpha)
    if split_tiles is None:
        out_refs[0][...] = y
    else:
        @pl.when(i < split_tiles)
        def _():
            out_refs[0][...] = y

        @pl.when(i >= split_tiles)
        def _():
            out_refs[1][...] = y


def _moe_combine_ln(x2d, pos, aff, starts, ends, o, g, b, alpha, split=None):
    n, d = x2d.shape
    n_exp, cap_tot, _ = o.shape
    t = COMBINE_TOKENS
    win = COMBINE_WINDOW
    assert n % t == 0 and cap_tot % BF16_ROWS == 0 and cap_tot >= win and n_exp % 2 == 0 and 2 * win == LANES
    n_tiles = n // t
    row = pl.BlockSpec((1, d), lambda i, st, en: (0, 0))
    tok = lambda w: pl.BlockSpec((t, w), lambda i, st, en: (i, 0))
    if split is None:
        split_tiles = None
        out_specs = tok(d)
        out_shape = jax.ShapeDtypeStruct((n, d), F32)
    else:
        assert split % t == 0 and 0 < split < n
        split_tiles = split // t
        out_specs = [pl.BlockSpec((t, d), lambda i, st, en: (jnp.minimum(i, split_tiles - 1), 0)),
                     pl.BlockSpec((t, d), lambda i, st, en: (jnp.maximum(i - split_tiles, 0), 0))]
        out_shape = [jax.ShapeDtypeStruct((split, d), F32), jax.ShapeDtypeStruct((n - split, d), F32)]
    kern = functools.partial(_combine_kernel, alpha=alpha, n_exp=n_exp, win=win, cap_tot=cap_tot, n_tiles=n_tiles,
                             split_tiles=split_tiles)
    grid_spec = pltpu.PrefetchScalarGridSpec(
        num_scalar_prefetch=2,
        grid=(n_tiles,),
        in_specs=[tok(d), tok(n_exp), tok(n_exp), row, row, pl.BlockSpec(memory_space=pl.ANY)],
        out_specs=out_specs,
        scratch_shapes=[pltpu.VMEM((2, n_exp * win, d), BF16), pltpu.SemaphoreType.DMA((2, n_exp)),
                        pltpu.VMEM((win, d), BF16), pltpu.SemaphoreType.DMA(()), pltpu.VMEM((t, d), F32)],
    )
    return pl.pallas_call(
        kern,
        grid_spec=grid_spec,
        out_shape=out_shape,
        compiler_params=_params("arbitrary"),
        name="moe_combine_ln",
    )(starts, ends, x2d, pos, aff, g.reshape(1, d), b.reshape(1, d), o)


def _ec_moe_ln(x2d, aff, groups, wg, wu, wd, layer, g, b, alpha, split=None):
    n, d = x2d.shape
    n_exp = aff.shape[1]
    t = COMBINE_TOKENS
    pos_l, st_l, en_l = [], [], []
    off = 0
    for start, cnt_tok in groups:
        assert start % t == 0 and cnt_tok % t == 0
        cap = EC_CAPACITY_FACTOR * cnt_tok // n_exp
        pos, cnt = _ec_select(aff[start:start + cnt_tok], cap)
        st = cnt[::t] + off
        en = jnp.concatenate([st[1:], jnp.full((1, n_exp), off + cap, jnp.int32)], axis=0)
        pos_l.append(jnp.where(pos >= 0, pos + off, -1))
        st_l.append(st)
        en_l.append(en)
        off += cap
    cap_tot = off
    pos = jnp.concatenate(pos_l, axis=0)
    starts = jnp.concatenate(st_l, axis=0).reshape(-1)
    ends = jnp.concatenate(en_l, axis=0).reshape(-1)
    idx = _ec_invert(pos, starts, ends, cap_tot)
    o = _moe_ffn(x2d, idx, wg, wu, wd, layer)
    return _moe_combine_ln(x2d, pos, aff, starts, ends, o, g, b, alpha, split)


def _split2_bf16(w):
    hi = w.astype(BF16)
    lo = (w - hi.astype(F32)).astype(BF16)
    return jnp.concatenate([hi, lo], axis=1)


def kernel(x_prompt, x_sample, attn_w_qkv, attn_q_norm, attn_k_norm, attn_w_o, pool_w, pool_scale, ssd_w_in,
           ssd_conv_w, ssd_conv_b, ssd_dt_bias, ssd_A_log, ssd_D, ssd_norm, ssd_w_out, moe_w_router, moe_w_gate,
           moe_w_up, moe_w_down, ln_g, ln_b):
    bp, s, d = x_prompt.shape
    bs = x_sample.shape[0]
    assert x_sample.shape[1] == s
    bsz = bp + bs
    n = bsz * s
    groups = [(0, bp * s), (bp * s, bs * s)]
    depth = ln_g.shape[0]
    alpha = (2 * depth) ** 0.25
    n_heads = attn_w_o.shape[1] // HEAD_DIM
    n_kv = (attn_w_qkv.shape[2] // HEAD_DIM - n_heads) // 2
    d_inner = ssd_w_out.shape[1]
    n_ssd_heads = ssd_A_log.shape[-1]
    assert d_inner == n_ssd_heads * SSD_HEAD_DIM
    conv_dim = ssd_conv_w.shape[2]
    n_groups = (conv_dim - d_inner) // (2 * D_STATE)
    rope = _rope_tables(s)

    x = jnp.concatenate([x_prompt, x_sample], axis=0).reshape(n, d)
    wg_all, wu_all, wd_all = moe_w_gate.astype(BF16), moe_w_up.astype(BF16), moe_w_down.astype(BF16)
    ia = ip = isd = 0
    for i in range(depth):
        wr2 = _split2_bf16(moe_w_router[i])
        g1, b1, g2, b2 = ln_g[i, 0], ln_b[i, 0], ln_g[i, 1], ln_b[i, 1]
        kind = i % 3
        if kind == 0:
            w_qkv = _permute_qk_columns(attn_w_qkv[ia], n_heads, n_kv).astype(BF16)
            qkv = _qkv_proj(x, w_qkv, attn_q_norm[ia], attn_k_norm[ia], rope, s, n_heads, n_kv)
            qkv3 = qkv.reshape(bsz, s, -1)
            vt = jnp.swapaxes(qkv3[:, :, (n_heads + n_kv) * HEAD_DIM:], 1, 2).reshape(bsz, n_kv, HEAD_DIM, s)
            o = _flash_attention(qkv3, vt, n_heads, n_kv)
            x, aff = _mm_res_ln_router(o.reshape(n, -1), attn_w_o[ia].astype(BF16), x, g1, b1, wr2, alpha)
            ia += 1
        elif kind == 1:
            x3, aff3 = _pool_layer(x.reshape(bsz, s, d), pool_w[ip].astype(BF16), pool_scale[ip], g1, b1, wr2, alpha)
            x, aff = x3.reshape(n, d), aff3.reshape(n, -1)
            ip += 1
        else:
            zx = _matmul_f32(x, ssd_w_in[isd].astype(BF16), 1152)
            zx3 = zx.reshape(bsz, s, -1)
            xbc = _ssd_conv(zx3, ssd_conv_w[isd], ssd_conv_b[isd], d_inner, conv_dim)
            a_neg = -jnp.exp(ssd_A_log[isd].astype(F32)).reshape(-1)
            dt, ecs, dt_t, ecs_t, tot = _ssd_dt(zx3, ssd_dt_bias[isd].reshape(-1), a_neg, d_inner + conv_dim)
            yf, yb = _ssd_scan(xbc, dt, ecs, dt_t, ecs_t, tot, ssd_D[isd], d_inner, n_groups)
            yn = _ssd_gate(yf.reshape(n, d_inner), yb.reshape(n, d_inner), zx, ssd_norm[isd])
            x, aff = _mm_res_ln_router(yn, ssd_w_out[isd].astype(BF16), x, g1, b1, wr2, alpha)
            isd += 1
        x = _ec_moe_ln(x, aff, groups, wg_all, wu_all, wd_all, i, g2, b2, alpha,
                       split=bp * s if i == depth - 1 else None)
    y_prompt, y_sample = x
    return y_prompt.reshape(bp, s, d), y_sample.reshape(bs, s, d)
```

```python
import functools
import math

import jax
import jax.numpy as jnp
from jax import lax
from jax.experimental import pallas as pl
from jax.experimental.pallas import tpu as pltpu

F32 = jnp.float32
BF16 = jnp.bfloat16

HEAD_DIM = 128
GRID_W = 64
ROPE_THETA = 10000.0
POOL_WINDOWS = (2, 4, 8, 16)
POOL_HALO = 8
D_STATE = 128
SSD_CHUNK = 128
SSD_HEAD_DIM = 64
SSD_HEADS_PER_GROUP = 8
D_CONV = 4
CONV_LEFT = D_CONV // 2
EC_CAPACITY_FACTOR = 2
LN_EPS = 1e-5
RMS_EPS = 1e-6
LOG2E = 1.4426950408889634

V7X_VMEM_LIMIT_BYTES = 52 * 1024 * 1024
LANES = 128


def _params(*sem):
    return pltpu.CompilerParams(dimension_semantics=sem, vmem_limit_bytes=V7X_VMEM_LIMIT_BYTES)


def _pick(n, pref):
    t = min(n, pref)
    while n % t:
        t //= 2
    return t


def _res_ln(x, h, g, b, alpha):
    y = alpha * x + h
    mu = jnp.mean(y, axis=-1, keepdims=True)
    yc = y - mu
    var = jnp.mean(yc * yc, axis=-1, keepdims=True)
    return yc * lax.rsqrt(var + LN_EPS) * g + b


def _router_affinity(xn, wr_ref, n_exp):
    xh = xn.astype(BF16)
    xl = (xn - xh.astype(F32)).astype(BF16)
    wr = wr_ref[...]
    r1 = jnp.dot(xh, wr, preferred_element_type=F32)
    r2 = jnp.dot(xl, wr[:, :n_exp], preferred_element_type=F32)
    logits = r1[:, :n_exp] + (r1[:, n_exp:] + r2)
    m = jnp.max(logits, axis=-1, keepdims=True)
    e = jnp.exp(logits - m)
    return e / jnp.sum(e, axis=-1, keepdims=True)


def _qkv_kernel(x_ref, w_ref, cos_ref, sin_ref, qn_ref, kn_ref, o_ref, xb_ref, *,
                n_q_tiles, n_k_tiles, heads_per_tile, q_scale):
    j = pl.program_id(1)

    @pl.when(j == 0)
    def _():
        xb_ref[...] = x_ref[...].astype(BF16)

    acc = jnp.dot(xb_ref[...], w_ref[...], preferred_element_type=F32)

    def norm_rope(gain_ref, scale):
        cos = cos_ref[...]
        sin = sin_ref[...]
        g = gain_ref[...]
        for h in range(heads_per_tile):
            a = acc[:, h * HEAD_DIM:(h + 1) * HEAD_DIM]
            a = a * lax.rsqrt(jnp.mean(a * a, axis=-1, keepdims=True) + RMS_EPS) * g
            r = a * cos + pltpu.roll(a, HEAD_DIM // 2, 1) * sin
            o_ref[:, h * HEAD_DIM:(h + 1) * HEAD_DIM] = (r * scale).astype(BF16)

    @pl.when(j < n_q_tiles)
    def _():
        norm_rope(qn_ref, q_scale)

    @pl.when((j >= n_q_tiles) & (j < n_q_tiles + n_k_tiles))
    def _():
        norm_rope(kn_ref, 1.0)

    @pl.when(j >= n_q_tiles + n_k_tiles)
    def _():
        o_ref[...] = acc.astype(BF16)


def _rope_perm():
    quarter = HEAD_DIM // 4
    blocks = (0, 2, 1, 3)
    return jnp.concatenate([jnp.arange(quarter) + b * quarter for b in blocks])


def _rope_tables(seq_len):
    rows = seq_len // GRID_W
    row = jnp.repeat(jnp.arange(rows, dtype=F32), GRID_W)
    col = jnp.tile(jnp.arange(GRID_W, dtype=F32), rows)
    inv_freq = ROPE_THETA ** (-jnp.arange(0, HEAD_DIM // 2, 2, dtype=F32) / (HEAD_DIM // 2))
    ang = jnp.concatenate([row[:, None] * inv_freq, col[:, None] * inv_freq], axis=-1)
    cos = jnp.concatenate([jnp.cos(ang), jnp.cos(ang)], axis=-1)
    sin = jnp.concatenate([-jnp.sin(ang), jnp.sin(ang)], axis=-1)
    return cos, sin


def _permute_qk_columns(w_qkv, n_heads, n_kv):
    d = w_qkv.shape[0]
    n_qk = n_heads + n_kv
    qk = w_qkv[:, :n_qk * HEAD_DIM].reshape(d, n_qk, HEAD_DIM)[:, :, _rope_perm()].reshape(d, n_qk * HEAD_DIM)
    return jnp.concatenate([qk, w_qkv[:, n_qk * HEAD_DIM:]], axis=1)


def _qkv_proj(x2d, w_bf16, q_norm, k_norm, rope, seq_len, n_heads, n_kv):
    n, d = x2d.shape
    qkv_dim = w_bf16.shape[1]
    tn = n_kv * HEAD_DIM
    tm = _pick(seq_len, 1024)
    cos, sin = rope
    perm = _rope_perm()
    q_norm, k_norm = q_norm[perm], k_norm[perm]
    nsb = seq_len // tm
    kern = functools.partial(
        _qkv_kernel, n_q_tiles=n_heads // n_kv, n_k_tiles=1, heads_per_tile=n_kv,
        q_scale=HEAD_DIM ** -0.5 * LOG2E)
    tab = pl.BlockSpec((tm, HEAD_DIM), lambda i, j: (i % nsb, 0))
    vec = pl.BlockSpec((1, HEAD_DIM), lambda i, j: (0, 0))
    return pl.pallas_call(
        kern,
        grid=(n // tm, qkv_dim // tn),
        in_specs=[pl.BlockSpec((tm, d), lambda i, j: (i, 0)),
                  pl.BlockSpec((d, tn), lambda i, j: (0, j)),
                  tab, tab, vec, vec],
        out_specs=pl.BlockSpec((tm, tn), lambda i, j: (i, j)),
        out_shape=jax.ShapeDtypeStruct((n, qkv_dim), BF16),
        scratch_shapes=[pltpu.VMEM((tm, d), BF16)],
        compiler_params=_params("parallel", "arbitrary"),
        name="qkv_proj",
    )(x2d, w_bf16, cos, sin, q_norm.reshape(1, HEAD_DIM), k_norm.reshape(1, HEAD_DIM))


FLASH_TQ = 128
FLASH_TK = 2048


def _flash_kernel(q_ref, k_ref, vt_ref, o_ref, s_scr, p_scr, acc_scr, *, tk, group):
    tq = q_ref.shape[1]
    seq = k_ref.shape[1]
    q = jnp.concatenate([q_ref[0, :, g * HEAD_DIM:(g + 1) * HEAD_DIM] for g in range(group)], axis=0)
    rows = group * tq
    nc = seq // tk

    def scores(c, slot):
        k = k_ref[0, pl.ds(c * tk, tk), :]
        s_scr[slot] = lax.dot_general(k, q, (((1,), (1,)), ((), ())), preferred_element_type=F32)

    def pv(c, slot, alpha):
        vt = vt_ref[0, 0, :, pl.ds(c * tk, tk)]
        acc_scr[...] = acc_scr[...] * alpha + jnp.dot(vt, p_scr[slot], preferred_element_type=F32)

    def softmax(slot, m, l):
        s = s_scr[slot]
        m_new = jnp.maximum(m, jnp.max(s, axis=0, keepdims=True))
        alpha = jnp.exp2(m - m_new)
        p = jnp.exp2(s - m_new)
        l = alpha * l + jnp.sum(p, axis=0, keepdims=True)
        p_scr[slot] = p.astype(BF16)
        return m_new, l, alpha

    m = jnp.full((1, rows), -jnp.inf, F32)
    l = jnp.zeros((1, rows), F32)
    acc_scr[...] = jnp.zeros_like(acc_scr)
    scores(0, 0)
    if nc > 1:
        scores(1, 1)
    m, l, alpha = softmax(0, m, l)
    for c in range(1, nc):
        if c + 1 < nc:
            scores(c + 1, (c + 1) % 2)
        pv(c - 1, (c - 1) % 2, alpha)
        m, l, alpha = softmax(c % 2, m, l)
    pv(nc - 1, (nc - 1) % 2, alpha)
    o = (acc_scr[...] / l).T
    for g in range(group):
        o_ref[0, :, g * HEAD_DIM:(g + 1) * HEAD_DIM] = o[g * tq:(g + 1) * tq].astype(BF16)


def _flash_attention(qkv, vt, n_heads, n_kv):
    b, s, _ = qkv.shape
    group = n_heads // n_kv
    tq = _pick(s, FLASH_TQ)
    tk = _pick(s, FLASH_TK)
    gw = group * HEAD_DIM
    rows = group * tq
    vrows = vt.shape[2]
    kern = functools.partial(_flash_kernel, tk=tk, group=group)
    return pl.pallas_call(
        kern,
        grid=(b, n_kv, s // tq),
        in_specs=[pl.BlockSpec((1, tq, gw), lambda bi, h, i: (bi, i, h)),
                  pl.BlockSpec((1, s, HEAD_DIM), lambda bi, h, i: (bi, 0, n_heads + h)),
                  pl.BlockSpec((1, 1, vrows, s), lambda bi, h, i: (bi, h, 0, 0))],
        out_specs=pl.BlockSpec((1, tq, gw), lambda bi, h, i: (bi, i, h)),
        out_shape=jax.ShapeDtypeStruct((b, s, n_heads * HEAD_DIM), BF16),
        scratch_shapes=[pltpu.VMEM((2, tk, rows), F32), pltpu.VMEM((2, tk, rows), BF16),
                        pltpu.VMEM((vrows, rows), F32)],
        compiler_params=_params("parallel", "parallel", "arbitrary"),
        name="flash_attention",
    )(qkv, qkv, vt)


MM_LN_TM = 512
MM_LN_TK = 2048


def _mm_ln_kernel(a_ref, w_ref, x_ref, g_ref, b_ref, wr_ref, o_ref, aff_ref, acc_ref, *, alpha, nk, n_exp):
    k = pl.program_id(1)
    part = jnp.dot(a_ref[...], w_ref[...], preferred_element_type=F32)

    def finish(h):
        xn = _res_ln(x_ref[...], h, g_ref[...], b_ref[...], alpha)
        o_ref[...] = xn
        aff_ref[...] = _router_affinity(xn, wr_ref, n_exp)

    if nk == 1:
        finish(part)
        return

    @pl.when(k == 0)
    def _():
        acc_ref[...] = part

    @pl.when((k > 0) & (k < nk - 1))
    def _():
        acc_ref[...] += part

    @pl.when(k == nk - 1)
    def _():
        finish(acc_ref[...] + part)


def _mm_res_ln_router(a_bf16, w_bf16, x2d, g, b, wr2, alpha):
    n, kdim = a_bf16.shape
    d = w_bf16.shape[1]
    n_exp = wr2.shape[1] // 2
    tm = _pick(n, MM_LN_TM)
    tk = _pick(kdim, MM_LN_TK)
    nk = kdim // tk
    kern = functools.partial(_mm_ln_kernel, alpha=alpha, nk=nk, n_exp=n_exp)
    row = pl.BlockSpec((1, d), lambda i, k: (0, 0))
    return pl.pallas_call(
        kern,
        grid=(n // tm, nk),
        in_specs=[pl.BlockSpec((tm, tk), lambda i, k: (i, k)),
                  pl.BlockSpec((tk, d), lambda i, k: (k, 0)),
                  pl.BlockSpec((tm, d), lambda i, k: (i, 0)),
                  row, row,
                  pl.BlockSpec((d, 2 * n_exp), lambda i, k: (0, 0))],
        out_specs=[pl.BlockSpec((tm, d), lambda i, k: (i, 0)),
                   pl.BlockSpec((tm, n_exp), lambda i, k: (i, 0))],
        out_shape=[jax.ShapeDtypeStruct((n, d), F32), jax.ShapeDtypeStruct((n, n_exp), F32)],
        scratch_shapes=[pltpu.VMEM((tm, d), F32)],
        compiler_params=_params("parallel", "arbitrary"),
        name="mm_res_ln_router",
    )(a_bf16, w_bf16, x2d, g.reshape(1, d), b.reshape(1, d), wr2)


def _pool_kernel(prev_ref, cur_ref, next_ref, w_ref, sc_ref, g_ref, b_ref, wr_ref, o_ref, aff_ref, ext_ref, *,
                 alpha, nt, seq_len, n_exp):
    i = pl.program_id(1)
    t = cur_ref.shape[1]
    d = cur_ref.shape[2]
    pg = d // len(POOL_WINDOWS)
    x = cur_ref[0]
    ext_ref[0:POOL_HALO, :] = jnp.where(i == 0, 0.0, prev_ref[0])
    ext_ref[POOL_HALO:POOL_HALO + t, :] = x
    ext_ref[POOL_HALO + t:2 * POOL_HALO + t, :] = jnp.where(i == nt - 1, 0.0, next_ref[0])
    pos = i * t + lax.broadcasted_iota(jnp.int32, (t, 1), 0)
    hs = []
    fsum = ext_ref[...]
    rows = fsum.shape[0]
    width = 1
    for gi, w in enumerate(POOL_WINDOWS):
        half = w // 2
        cols = slice(gi * pg, (gi + 1) * pg)
        while width < w:
            fsum = fsum + pltpu.roll(fsum, rows - width, 0)
            width *= 2
        win = fsum[:, :pg]
        fsum = fsum[:, pg:]
        if half == POOL_HALO:
            acc = win[:t]
        else:
            acc = pltpu.roll(win, half, 0)[POOL_HALO:POOL_HALO + t]
        cnt = (jnp.minimum(pos + half, seq_len) - jnp.maximum(pos - half, 0)).astype(F32)
        mixed = (acc / cnt - x[:, cols]).astype(BF16)
        hs.append(jnp.dot(mixed, w_ref[gi], preferred_element_type=F32))
    h = jnp.concatenate(hs, axis=-1) * sc_ref[...]
    xn = _res_ln(x, h, g_ref[...], b_ref[...], alpha)
    o_ref[0] = xn
    aff_ref[0] = _router_affinity(xn, wr_ref, n_exp)


def _pool_layer(x3d, w_bf16, scale, g, b, wr2, alpha):
    bsz, s, d = x3d.shape
    n_exp = wr2.shape[1] // 2
    t = _pick(s, 512)
    nt = s // t
    hb = t // POOL_HALO
    last_hb = s // POOL_HALO - 1
    pg = d // len(POOL_WINDOWS)
    kern = functools.partial(_pool_kernel, alpha=alpha, nt=nt, seq_len=s, n_exp=n_exp)
    row = pl.BlockSpec((1, d), lambda bi, i: (0, 0))
    return pl.pallas_call(
        kern,
        grid=(bsz, nt),
        in_specs=[pl.BlockSpec((1, POOL_HALO, d), lambda bi, i: (bi, jnp.maximum(i * hb - 1, 0), 0)),
                  pl.BlockSpec((1, t, d), lambda bi, i: (bi, i, 0)),
                  pl.BlockSpec((1, POOL_HALO, d), lambda bi, i: (bi, jnp.minimum((i + 1) * hb, last_hb), 0)),
                  pl.BlockSpec((len(POOL_WINDOWS), pg, pg), lambda bi, i: (0, 0, 0)),
                  row, row, row,
                  pl.BlockSpec((d, 2 * n_exp), lambda bi, i: (0, 0))],
        out_specs=[pl.BlockSpec((1, t, d), lambda bi, i: (bi, i, 0)),
                   pl.BlockSpec((1, t, n_exp), lambda bi, i: (bi, i, 0))],
        out_shape=[jax.ShapeDtypeStruct((bsz, s, d), F32), jax.ShapeDtypeStruct((bsz, s, n_exp), F32)],
        scratch_shapes=[pltpu.VMEM((t + 2 * POOL_HALO, d), F32)],
        compiler_params=_params("parallel", "parallel"),
        name="pool_mixer",
    )(x3d, x3d, x3d, w_bf16, scale.reshape(1, d), g.reshape(1, d), b.reshape(1, d), wr2)


def _mm_kernel(x_ref, w_ref, o_ref, xb_ref):
    @pl.when(pl.program_id(1) == 0)
    def _():
        xb_ref[...] = x_ref[...].astype(BF16)

    o_ref[...] = jnp.dot(xb_ref[...], w_ref[...], preferred_element_type=F32)


def _matmul_f32(x2d, w_bf16, tn_pref):
    n, d = x2d.shape
    nout = w_bf16.shape[1]
    tm = _pick(n, 1024)
    tn = tn_pref
    assert nout % tn == 0
    return pl.pallas_call(
        _mm_kernel,
        grid=(n // tm, nout // tn),
        in_specs=[pl.BlockSpec((tm, d), lambda i, j: (i, 0)),
                  pl.BlockSpec((d, tn), lambda i, j: (0, j))],
        out_specs=pl.BlockSpec((tm, tn), lambda i, j: (i, j)),
        out_shape=jax.ShapeDtypeStruct((n, nout), F32),
        scratch_shapes=[pltpu.VMEM((tm, d), BF16)],
        compiler_params=_params("parallel", "arbitrary"),
        name="ssd_in_proj",
    )(x2d, w_bf16)


def _conv_kernel(prev_ref, cur_ref, next_ref, w_ref, b_ref, o_ref, ext_ref, *, nt):
    i = pl.program_id(1)
    t = cur_ref.shape[1]
    ext_ref[0:POOL_HALO, :] = jnp.where(i == 0, 0.0, prev_ref[0])
    ext_ref[POOL_HALO:POOL_HALO + t, :] = cur_ref[0]
    ext_ref[POOL_HALO + t:2 * POOL_HALO + t, :] = jnp.where(i == nt - 1, 0.0, next_ref[0])
    ext = ext_ref[...]
    rows = ext.shape[0]
    acc = None
    for kk in range(D_CONV):
        off = kk - CONV_LEFT
        tap = ext if off == 0 else pltpu.roll(ext, (-off) % rows, 0)
        term = tap[POOL_HALO:POOL_HALO + t, :] * w_ref[kk:kk + 1, :]
        acc = term if acc is None else acc + term
    acc = acc + b_ref[...]
    o_ref[0] = acc / (1.0 + jnp.exp(-acc))


def _ssd_conv(zx3d, conv_w, conv_b, d_inner, conv_dim):
    bsz, s, _ = zx3d.shape
    tc = 1024
    t = _pick(s, 512)
    nt = s // t
    hb = t // POOL_HALO
    last_hb = s // POOL_HALO - 1
    c0 = d_inner // tc
    return pl.pallas_call(
        functools.partial(_conv_kernel, nt=nt),
        grid=(bsz, nt, conv_dim // tc),
        in_specs=[pl.BlockSpec((1, POOL_HALO, tc), lambda bi, i, j: (bi, jnp.maximum(i * hb - 1, 0), c0 + j)),
                  pl.BlockSpec((1, t, tc), lambda bi, i, j: (bi, i, c0 + j)),
                  pl.BlockSpec((1, POOL_HALO, tc), lambda bi, i, j: (bi, jnp.minimum((i + 1) * hb, last_hb), c0 + j)),
                  pl.BlockSpec((D_CONV, tc), lambda bi, i, j: (0, j)),
                  pl.BlockSpec((1, tc), lambda bi, i, j: (0, j))],
        out_specs=pl.BlockSpec((1, t, tc), lambda bi, i, j: (bi, i, j)),
        out_shape=jax.ShapeDtypeStruct((bsz, s, conv_dim), F32),
        scratch_shapes=[pltpu.VMEM((t + 2 * POOL_HALO, tc), F32)],
        compiler_params=_params("parallel", "parallel", "parallel"),
        name="ssd_conv",
    )(zx3d, zx3d, zx3d, conv_w, conv_b.reshape(1, conv_dim))


def _split3(x):
    hi = x.astype(BF16)
    r1 = x - hi.astype(F32)
    mid = r1.astype(BF16)
    lo = (r1 - mid.astype(F32)).astype(BF16)
    return hi, mid, lo


def _dt_kernel(raw_ref, bias_ref, a_ref, dt_ref, e_ref, dtt_ref, et_ref, tot_ref):
    v = raw_ref[0] + bias_ref[...]
    dt = jnp.maximum(v, 0.0) + jnp.log1p(jnp.exp(-jnp.abs(v)))
    dt_ref[0] = dt
    dtt_ref[0] = dt.T
    a = dt * a_ref[...]
    q, w = a.shape
    li = lax.broadcasted_iota(jnp.int32, (q, q), 0)
    si = lax.broadcasted_iota(jnp.int32, (q, q), 1)
    tri = jnp.where(li >= si, 1.0, 0.0).astype(BF16)
    hi, mid, lo = _split3(a)
    cs = (jnp.dot(tri, lo, preferred_element_type=F32) + jnp.dot(tri, mid, preferred_element_type=F32)
          + jnp.dot(tri, hi, preferred_element_type=F32))
    lane = lax.broadcasted_iota(jnp.int32, (q, w), 1)
    e = jnp.where(lane < w // 2, cs, cs - a)
    e_ref[0] = e
    et_ref[0] = e.T
    tot_ref[0, 0] = cs[q - 1:q, :]


def _ssd_dt(zx3d, dt_bias, a_neg, col0):
    bsz, s, _ = zx3d.shape
    w = dt_bias.shape[-1]
    assert w == LANES and col0 % LANES == 0 and SSD_CHUNK == LANES
    nc = s // SSD_CHUNK
    blk = pl.BlockSpec((1, SSD_CHUNK, w), lambda bi, c: (bi, c, 0))
    blk_t = pl.BlockSpec((1, w, SSD_CHUNK), lambda bi, c: (bi, 0, c))
    row = pl.BlockSpec((1, w), lambda bi, c: (0, 0))
    return pl.pallas_call(
        _dt_kernel,
        grid=(bsz, nc),
        in_specs=[pl.BlockSpec((1, SSD_CHUNK, w), lambda bi, c: (bi, c, col0 // LANES)), row, row],
        out_specs=[blk, blk, blk_t, blk_t, pl.BlockSpec((1, 1, 1, w), lambda bi, c: (bi, c, 0, 0))],
        out_shape=[jax.ShapeDtypeStruct((bsz, s, w), F32), jax.ShapeDtypeStruct((bsz, s, w), F32),
                   jax.ShapeDtypeStruct((bsz, w, s), F32), jax.ShapeDtypeStruct((bsz, w, s), F32),
                   jax.ShapeDtypeStruct((bsz, nc, 1, w), F32)],
        compiler_params=_params("parallel", "parallel"),
        name="ssd_dt",
    )(zx3d, dt_bias.reshape(1, w), a_neg.reshape(1, w))


def _expand_heads(v, j0, width):
    m = v.shape[0]
    lane = lax.broadcasted_iota(jnp.int32, (m, LANES), 1)
    parts = []
    for pr in range(width // LANES):
        j = j0 + 2 * pr
        parts.append(jnp.where(lane < SSD_HEAD_DIM, v[:, j:j + 1], v[:, j + 1:j + 2]))
    return jnp.concatenate(parts, axis=1)


def _ssd_direction(x, bmat, cmat, dt_r, e_c, e_r, out_dec, st_w, chunk_dec, j0, st_ref, forward):
    q, width = x.shape
    li = lax.broadcasted_iota(jnp.int32, (q, q), 0)
    si = lax.broadcasted_iota(jnp.int32, (q, q), 1)
    lane = lax.broadcasted_iota(jnp.int32, (q, LANES), 1)
    mask = (li >= si) if forward else (si >= li)
    cb = lax.dot_general(cmat.astype(BF16), bmat.astype(BF16), (((1,), (1,)), ((), ())),
                         preferred_element_type=F32)
    xb = x.astype(BF16)
    y_parts = []
    for pr in range(width // LANES):
        ms = []
        for j in (j0 + 2 * pr, j0 + 2 * pr + 1):
            if forward:
                diff = e_c[:, j:j + 1] - e_r[j:j + 1, :]
            else:
                diff = e_r[j:j + 1, :] - e_c[:, j:j + 1]
            decay = jnp.exp(jnp.where(mask, diff, -jnp.inf))
            ms.append((decay * cb * dt_r[j:j + 1, :]).astype(BF16))
        xp = xb[:, pr * LANES:(pr + 1) * LANES]
        zero = jnp.zeros_like(xp)
        rhs = jnp.concatenate([jnp.where(lane < SSD_HEAD_DIM, xp, zero),
                               jnp.where(lane >= SSD_HEAD_DIM, xp, zero)], axis=0)
        y_parts.append(jnp.dot(jnp.concatenate(ms, axis=1), rhs, preferred_element_type=F32))
    y = jnp.concatenate(y_parts, axis=1)
    st = st_ref[...]
    y = y + jnp.dot(cmat.astype(BF16), st.astype(BF16), preferred_element_type=F32) * _expand_heads(out_dec, j0, width)
    xd = (x * _expand_heads(st_w, j0, width)).astype(BF16)
    st_new = lax.dot_general(bmat.astype(BF16), xd, (((0,), (0,)), ((), ())), preferred_element_type=F32)
    st_ref[...] = st * _expand_heads(chunk_dec, j0, width)[0:1, :] + st_new
    return y


def _ssd_scan_kernel(xf_ref, bf_ref, cf_ref, dtf_ref, ef_ref, dttf_ref, etf_ref, tf_ref,
                     xr_ref, br_ref, cr_ref, dtr_ref, er_ref, dttr_ref, etr_ref, tr_ref,
                     dskip_ref, yf_ref, yb_ref, stf_ref, stb_ref, *, n_groups, gw):
    @pl.when(pl.program_id(1) == 0)
    def _():
        stf_ref[...] = jnp.zeros_like(stf_ref)
        stb_ref[...] = jnp.zeros_like(stb_ref)

    hg = SSD_HEADS_PER_GROUP
    ef, tf = ef_ref[0], tf_ref[0, 0]
    out_dec_f = jnp.exp(ef)
    st_w_f = dtf_ref[0] * jnp.exp(tf - ef)
    chunk_dec_f = jnp.broadcast_to(jnp.exp(tf), (8, LANES))
    er, tr = er_ref[0], tr_ref[0, 0]
    out_dec_r = jnp.exp(tr - er)
    st_w_r = dtr_ref[0] * jnp.exp(er)
    chunk_dec_r = jnp.broadcast_to(jnp.exp(tr), (8, LANES))
    dttf, etf, dttr, etr = dttf_ref[0], etf_ref[0], dttr_ref[0], etr_ref[0]
    for gi in range(n_groups):
        xs = slice(gi * gw, (gi + 1) * gw)
        ns = slice(gi * D_STATE, (gi + 1) * D_STATE)
        xf = xf_ref[0, :, xs]
        yf = _ssd_direction(xf, bf_ref[0, :, ns], cf_ref[0, :, ns], dttf, ef, etf, out_dec_f, st_w_f, chunk_dec_f,
                            gi * hg, stf_ref.at[gi], True)
        yf_ref[0, :, xs] = yf + dskip_ref[:, xs] * xf
        yb_ref[0, :, xs] = _ssd_direction(xr_ref[0, :, xs], br_ref[0, :, ns], cr_ref[0, :, ns], dttr, er, etr,
                                          out_dec_r, st_w_r, chunk_dec_r, (n_groups + gi) * hg, stb_ref.at[gi],
                                          False)


def _ssd_scan(xbc, dt, ecs, dt_t, ecs_t, tot, d_skip, d_inner, n_groups):
    bsz, s, _ = xbc.shape
    nc = s // SSD_CHUNK
    gw = SSD_HEADS_PER_GROUP * SSD_HEAD_DIM
    w = dt.shape[-1]
    assert gw % LANES == 0 and d_inner == n_groups * gw and w == 2 * n_groups * SSD_HEADS_PER_GROUP == LANES
    gn = n_groups * D_STATE
    assert d_inner % gn == 0
    b0 = d_inner // gn
    dskip = jnp.repeat(d_skip.astype(F32), SSD_HEAD_DIM).reshape(1, d_inner)

    def specs(cidx):
        return [
            pl.BlockSpec((1, SSD_CHUNK, d_inner), lambda b, c: (b, cidx(c), 0)),
            pl.BlockSpec((1, SSD_CHUNK, gn), lambda b, c: (b, cidx(c), b0)),
            pl.BlockSpec((1, SSD_CHUNK, gn), lambda b, c: (b, cidx(c), b0 + 1)),
            pl.BlockSpec((1, SSD_CHUNK, w), lambda b, c: (b, cidx(c), 0)),
            pl.BlockSpec((1, SSD_CHUNK, w), lambda b, c: (b, cidx(c), 0)),
            pl.BlockSpec((1, w, SSD_CHUNK), lambda b, c: (b, 0, cidx(c))),
            pl.BlockSpec((1, w, SSD_CHUNK), lambda b, c: (b, 0, cidx(c))),
            pl.BlockSpec((1, 1, 1, w), lambda b, c: (b, cidx(c), 0, 0)),
        ]

    fwd = lambda c: c
    bwd = lambda c: nc - 1 - c
    y_shape = jax.ShapeDtypeStruct((bsz, s, d_inner), F32)
    return pl.pallas_call(
        functools.partial(_ssd_scan_kernel, n_groups=n_groups, gw=gw),
        grid=(bsz, nc),
        in_specs=specs(fwd) + specs(bwd) + [pl.BlockSpec((1, d_inner), lambda b, c: (0, 0))],
        out_specs=[pl.BlockSpec((1, SSD_CHUNK, d_inner), lambda b, c: (b, c, 0)),
                   pl.BlockSpec((1, SSD_CHUNK, d_inner), lambda b, c: (b, nc - 1 - c, 0))],
        out_shape=[y_shape, y_shape],
        scratch_shapes=[pltpu.VMEM((n_groups, D_STATE, gw), F32), pltpu.VMEM((n_groups, D_STATE, gw), F32)],
        compiler_params=_params("parallel", "arbitrary"),
        name="ssd_scan",
    )(xbc, xbc, xbc, dt, ecs, dt_t, ecs_t, tot,
      xbc, xbc, xbc, dt, ecs, dt_t, ecs_t, tot, dskip)


def _gate_kernel(yf_ref, yb_ref, z_ref, nw_ref, o_ref):
    z = z_ref[...]
    y = (yf_ref[...] + yb_ref[...]) * (z / (1.0 + jnp.exp(-z)))
    y = y * lax.rsqrt(jnp.mean(y * y, axis=-1, keepdims=True) + RMS_EPS) * nw_ref[...]
    o_ref[...] = y.astype(BF16)


def _ssd_gate(yf2d, yb2d, zx2d, norm_w):
    n, d_inner = yf2d.shape
    tm = _pick(n, 256)
    blk = pl.BlockSpec((tm, d_inner), lambda i: (i, 0))
    return pl.pallas_call(
        _gate_kernel,
        grid=(n // tm,),
        in_specs=[blk, blk, blk, pl.BlockSpec((1, d_inner), lambda i: (0, 0))],
        out_specs=blk,
        out_shape=jax.ShapeDtypeStruct((n, d_inner), BF16),
        compiler_params=_params("parallel"),
        name="ssd_gate_norm",
    )(yf2d, yb2d, zx2d, norm_w.reshape(1, d_inner))


FFN_TILE = 512


def _ffn_kernel(idx_ref, x_hbm, wg_ref, wu_ref, wd_ref, o_ref, xa, xb, sem, *, tile, n_steps):
    step = pl.program_id(0) * pl.num_programs(1) + pl.program_id(1)

    def issue(tile_idx, buf, s):
        base = tile_idx * tile
        for r in range(tile):
            tok = idx_ref[base + r]
            pltpu.make_async_copy(x_hbm.at[pl.ds(tok, 1), :], buf.at[pl.ds(r, 1), :], sem.at[s]).start(priority=r % 2)

    def wait(buf, s):
        pltpu.make_async_copy(x_hbm.at[pl.ds(0, tile), :], buf, sem.at[s]).wait()

    def ffn(buf, half):
        xs = buf[...].astype(BF16)
        hg = jnp.dot(xs, wg_ref[0, 0], preferred_element_type=F32)
        hu = jnp.dot(xs, wu_ref[0, 0], preferred_element_type=F32)
        h = (hg / (1.0 + jnp.exp(-hg)) * hu).astype(BF16)
        o_ref[0, half * tile:(half + 1) * tile, :] = jnp.dot(h, wd_ref[0, 0], preferred_element_type=F32).astype(BF16)

    @pl.when(step == 0)
    def _():
        issue(0, xa, 0)

    wait(xa, 0)
    issue(2 * step + 1, xb, 1)
    ffn(xa, 0)
    nxt = jnp.where(step + 1 < n_steps, 2 * step + 2, 0)
    issue(nxt, xa, 0)
    wait(xb, 1)
    ffn(xb, 1)

    @pl.when(step == n_steps - 1)
    def _():
        wait(xa, 0)


def _moe_ffn(x2d, idx, wg, wu, wd, layer):
    n, d = x2d.shape
    _, n_exp, _, f = wg.shape
    cap = idx.shape[0] // n_exp
    tile = _pick(cap // 2, FFN_TILE)
    steps_per_exp = cap // (2 * tile)
    kern = functools.partial(_ffn_kernel, tile=tile, n_steps=n_exp * steps_per_exp)
    grid_spec = pltpu.PrefetchScalarGridSpec(
        num_scalar_prefetch=1,
        grid=(n_exp, steps_per_exp),
        in_specs=[pl.BlockSpec(memory_space=pl.ANY),
                  pl.BlockSpec((1, 1, d, f), lambda e, t, ix: (layer, e, 0, 0)),
                  pl.BlockSpec((1, 1, d, f), lambda e, t, ix: (layer, e, 0, 0)),
                  pl.BlockSpec((1, 1, f, d), lambda e, t, ix: (layer, e, 0, 0))],
        out_specs=pl.BlockSpec((1, 2 * tile, d), lambda e, t, ix: (e, t, 0)),
        scratch_shapes=[pltpu.VMEM((tile, d), F32), pltpu.VMEM((tile, d), F32), pltpu.SemaphoreType.DMA((2,))],
    )
    return pl.pallas_call(
        kern,
        grid_spec=grid_spec,
        out_shape=jax.ShapeDtypeStruct((n_exp, cap, d), BF16),
        compiler_params=_params("arbitrary", "arbitrary"),
        name="moe_ffn",
    )(idx, x2d, wg, wu, wd)


def _invert_kernel(st_ref, en_ref, post_ref, acc_ref, *, n_exp, win, cap_tot, tokens):
    i = pl.program_id(0)

    @pl.when(i == 0)
    def _():
        acc_ref[...] = jnp.zeros_like(acc_ref)

    t = tokens
    tok = i * t + lax.broadcasted_iota(jnp.int32, (t, LANES), 0)
    lane = lax.broadcasted_iota(jnp.int32, (t, LANES), 1)
    digits = jnp.where(lane == 0, lax.shift_right_logical(tok, 8), jnp.where(lane == 1, tok & 255, 0))
    digits = digits.astype(F32).astype(BF16)
    post = post_ref[...]
    row = lax.broadcasted_iota(jnp.int32, (win, t), 0)

    def window(e):
        s0 = st_ref[i * n_exp + e]
        return jnp.minimum(lax.shift_left(lax.shift_right_logical(s0, 4), 4), cap_tot - win)

    def place(e, w, r):
        w = pl.multiple_of(w, BF16_ROWS)
        acc_ref[pl.ds(w, win), :] += r if e == 0 else pltpu.roll(r, 2 * e, 1)

    ws = [window(e) for e in range(n_exp)]
    onehots = jnp.concatenate([jnp.where(post[e:e + 1, :] - ws[e] == row, 1.0, 0.0).astype(BF16)
                               for e in range(n_exp)], axis=0)
    res = jnp.dot(onehots, digits, preferred_element_type=F32)
    for e in range(n_exp):
        place(e, ws[e], res[e * win:(e + 1) * win])

    for e in range(n_exp):
        w = ws[e]
        pe = post[e:e + 1, :]
        s1 = en_ref[i * n_exp + e]
        n_extra = jnp.maximum(s1 - w - 1, 0) // win

        def extra(k, carry, e=e, w=w, pe=pe):
            lo = w + win * (k + 1)
            wk = jnp.minimum(lo, cap_tot - win)
            oh = jnp.where((pe - wk == row) & (pe >= lo), 1.0, 0.0).astype(BF16)
            place(e, wk, jnp.dot(oh, digits, preferred_element_type=F32))
            return carry

        lax.fori_loop(0, n_extra, extra, 0)


def _ec_invert(pos, starts, ends, cap_tot):
    n, n_exp = pos.shape
    t = COMBINE_TOKENS
    win = COMBINE_WINDOW
    assert 2 * n_exp <= LANES and n < 256 * 256
    kern = functools.partial(_invert_kernel, n_exp=n_exp, win=win, cap_tot=cap_tot, tokens=t)
    grid_spec = pltpu.PrefetchScalarGridSpec(
        num_scalar_prefetch=2,
        grid=(n // t,),
        in_specs=[pl.BlockSpec((n_exp, t), lambda i, st, en: (0, i))],
        out_specs=pl.BlockSpec((cap_tot, LANES), lambda i, st, en: (0, 0)),
    )
    acc = pl.pallas_call(
        kern,
        grid_spec=grid_spec,
        out_shape=jax.ShapeDtypeStruct((cap_tot, LANES), F32),
        compiler_params=_params("arbitrary"),
        name="ec_invert",
    )(starts, ends, pos.T)
    digits = acc[:, :2 * n_exp].astype(jnp.int32).reshape(cap_tot, n_exp, 2)
    return (digits[:, :, 0] * 256 + digits[:, :, 1]).T.reshape(-1)


SELECT_ROW_TILE = 512


def _select_kernel(aff_ref, pos_ref, cnt_ref, *, cap, n_exp):
    a = aff_ref[...]
    r = a.shape[0]
    bits = pltpu.bitcast(a, jnp.int32)

    def fold(v):
        sh = n_exp
        while sh < LANES:
            v = v + pltpu.roll(v, sh, 1)
            sh *= 2
        return v

    def count(mask):
        return fold(jnp.sum(jnp.where(mask, 1.0, 0.0), axis=0, keepdims=True))

    def search(i, thr):
        cand = thr | jnp.left_shift(jnp.int32(1), 30 - i)
        return jnp.where(count(bits >= cand) >= cap, cand, thr)

    thr = lax.fori_loop(0, 31, search, jnp.zeros((1, LANES), jnp.int32))
    above = bits > thr
    tied = bits == thr
    need = cap - count(above)

    li = lax.broadcasted_iota(jnp.int32, (LANES, 2 * LANES), 0)
    ci = lax.broadcasted_iota(jnp.int32, (LANES, 2 * LANES), 1)
    same_exp = (li & (n_exp - 1)) == (ci & (n_exp - 1))
    earlier = (li // n_exp) < ((ci & (LANES - 1)) // n_exp)
    w2 = jnp.where(same_exp & ((ci >= LANES) | earlier), 1.0, 0.0).astype(BF16)
    tr = min(SELECT_ROW_TILE, r)
    rr = lax.broadcasted_iota(jnp.int32, (tr, tr), 0)
    rc = lax.broadcasted_iota(jnp.int32, (tr, tr), 1)
    rows_before = jnp.where(rr > rc, 1.0, 0.0).astype(BF16)

    def prefix(mask):
        lw = jnp.dot(jnp.where(mask, 1.0, 0.0).astype(BF16), w2, preferred_element_type=F32)
        within, row_tot = lw[:, :LANES], lw[:, LANES:]
        carry = jnp.zeros((1, LANES), F32)
        outs = []
        for t in range(r // tr):
            rt = row_tot[t * tr:(t + 1) * tr]
            outs.append(jnp.dot(rows_before, rt.astype(BF16), preferred_element_type=F32) + carry
                        + within[t * tr:(t + 1) * tr])
            carry = carry + jnp.sum(rt, axis=0, keepdims=True)
        return jnp.concatenate(outs, axis=0)

    sel = above | (tied & (prefix(tied) < need))
    cnt = prefix(sel).astype(jnp.int32)
    pos_ref[...] = jnp.where(sel, cnt, -1)
    cnt_ref[...] = cnt


def _ec_select(aff_group, cap):
    n_g, n_exp = aff_group.shape
    assert LANES % n_exp == 0 and (n_exp & (n_exp - 1)) == 0
    r = n_g * n_exp // LANES
    assert r % min(SELECT_ROW_TILE, r) == 0
    shp = jax.ShapeDtypeStruct((r, LANES), jnp.int32)
    pos, cnt = pl.pallas_call(
        functools.partial(_select_kernel, cap=cap, n_exp=n_exp),
        out_shape=[shp, shp],
        compiler_params=pltpu.CompilerParams(vmem_limit_bytes=V7X_VMEM_LIMIT_BYTES),
        name="ec_select",
    )(aff_group.reshape(r, LANES))
    return pos.reshape(n_g, n_exp), cnt.reshape(n_g, n_exp)


COMBINE_TOKENS = 256
COMBINE_WINDOW = 64
BF16_ROWS = 16


def _combine_kernel(st_ref, en_ref, x_ref, pos_ref, aff_ref, g_ref, b_ref, o_hbm, *rest, alpha, n_exp, win, cap_tot,
                    n_tiles, split_tiles):
    n_out = 1 if split_tiles is None else 2
    out_refs = rest[:n_out]
    buf, sem, xbuf, xsem, acc_ref = rest[n_out:]
    i = pl.program_id(0)
    slot = i % 2

    def window(tile, e):
        s0 = st_ref[tile * n_exp + e]
        return jnp.minimum(lax.shift_left(lax.shift_right_logical(s0, 4), 4), cap_tot - win)

    def fetch(tile, sl, e):
        w = pl.multiple_of(window(tile, e), BF16_ROWS)
        return pltpu.make_async_copy(o_hbm.at[e, pl.ds(w, win), :], buf.at[sl, pl.ds(e * win, win), :], sem.at[sl, e])

    @pl.when(i == 0)
    def _():
        for e in range(n_exp):
            fetch(0, 0, e).start()

    @pl.when(i + 1 < n_tiles)
    def _():
        for e in range(n_exp):
            fetch(i + 1, 1 - slot, e).start()

    pos = pos_ref[...]
    aff = aff_ref[...]
    t = pos.shape[0]
    lane = lax.broadcasted_iota(jnp.int32, (t, 2 * win), 1)
    first = lane < win
    lane_in = jnp.where(first, lane, lane - win)
    parts = []
    for e in range(0, n_exp, 2):
        rel = jnp.where(first, pos[:, e:e + 1] - window(i, e), pos[:, e + 1:e + 2] - window(i, e + 1))
        gate = jnp.where(first, aff[:, e:e + 1], aff[:, e + 1:e + 2])
        parts.append(jnp.where(rel == lane_in, gate, 0.0))
    pmat = jnp.concatenate(parts, axis=1).astype(BF16)
    for e in range(n_exp):
        fetch(i, slot, e).wait()
    acc_ref[...] = jnp.dot(pmat, buf[slot], preferred_element_type=F32)

    lane1 = lax.broadcasted_iota(jnp.int32, (t, win), 1)
    for e in range(n_exp):
        w = window(i, e)
        s1 = en_ref[i * n_exp + e]
        n_extra = jnp.maximum(s1 - w - 1, 0) // win

        def extra(k, carry, e=e, w=w):
            lo = w + win * (k + 1)
            wk = pl.multiple_of(jnp.minimum(lo, cap_tot - win), BF16_ROWS)
            cp = pltpu.make_async_copy(o_hbm.at[e, pl.ds(wk, win), :], xbuf, xsem)
            cp.start()
            cp.wait()
            pe = pos[:, e:e + 1]
            oh = jnp.where((pe - wk == lane1) & (pe >= lo), 1.0, 0.0).astype(BF16)
            acc_ref[...] += jnp.dot(oh, xbuf[...], preferred_element_type=F32) * aff[:, e:e + 1]
            return carry

        lax.fori_loop(0, n_extra, extra, 0)
    y = _res_ln(x_ref[...], acc_ref[...], g_ref[...], b_ref[...], alpha)
    if split_tiles is None:
        out_refs[0][...] = y
    else:
        @pl.when(i < split_tiles)
        def _():
            out_refs[0][...] = y

        @pl.when(i >= split_tiles)
        def _():
            out_refs[1][...] = y


def _moe_combine_ln(x2d, pos, aff, starts, ends, o, g, b, alpha, split=None):
    n, d = x2d.shape
    n_exp, cap_tot, _ = o.shape
    t = COMBINE_TOKENS
    win = COMBINE_WINDOW
    assert n % t == 0 and cap_tot % BF16_ROWS == 0 and cap_tot >= win and n_exp % 2 == 0 and 2 * win == LANES
    n_tiles = n // t
    row = pl.BlockSpec((1, d), lambda i, st, en: (0, 0))
    tok = lambda w: pl.BlockSpec((t, w), lambda i, st, en: (i, 0))
    if split is None:
        split_tiles = None
        out_specs = tok(d)
        out_shape = jax.ShapeDtypeStruct((n, d), F32)
    else:
        assert split % t == 0 and 0 < split < n
        split_tiles = split // t
        out_specs = [pl.BlockSpec((t, d), lambda i, st, en: (jnp.minimum(i, split_tiles - 1), 0)),
                     pl.BlockSpec((t, d), lambda i, st, en: (jnp.maximum(i - split_tiles, 0), 0))]
        out_shape = [jax.ShapeDtypeStruct((split, d), F32), jax.ShapeDtypeStruct((n - split, d), F32)]
    kern = functools.partial(_combine_kernel, alpha=alpha, n_exp=n_exp, win=win, cap_tot=cap_tot, n_tiles=n_tiles,
                             split_tiles=split_tiles)
    grid_spec = pltpu.PrefetchScalarGridSpec(
        num_scalar_prefetch=2,
        grid=(n_tiles,),
        in_specs=[tok(d), tok(n_exp), tok(n_exp), row, row, pl.BlockSpec(memory_space=pl.ANY)],
        out_specs=out_specs,
        scratch_shapes=[pltpu.VMEM((2, n_exp * win, d), BF16), pltpu.SemaphoreType.DMA((2, n_exp)),
                        pltpu.VMEM((win, d), BF16), pltpu.SemaphoreType.DMA(()), pltpu.VMEM((t, d), F32)],
    )
    return pl.pallas_call(
        kern,
        grid_spec=grid_spec,
        out_shape=out_shape,
        compiler_params=_params("arbitrary"),
        name="moe_combine_ln",
    )(starts, ends, x2d, pos, aff, g.reshape(1, d), b.reshape(1, d), o)


def _ec_moe_ln(x2d, aff, groups, wg, wu, wd, layer, g, b, alpha, split=None):
    n, d = x2d.shape
    n_exp = aff.shape[1]
    t = COMBINE_TOKENS
    pos_l, st_l, en_l = [], [], []
    off = 0
    for start, cnt_tok in groups:
        assert start % t == 0 and cnt_tok % t == 0
        cap = EC_CAPACITY_FACTOR * cnt_tok // n_exp
        pos, cnt = _ec_select(aff[start:start + cnt_tok], cap)
        st = cnt[::t] + off
        en = jnp.concatenate([st[1:], jnp.full((1, n_exp), off + cap, jnp.int32)], axis=0)
        pos_l.append(jnp.where(pos >= 0, pos + off, -1))
        st_l.append(st)
        en_l.append(en)
        off += cap
    cap_tot = off
    pos = jnp.concatenate(pos_l, axis=0)
    starts = jnp.concatenate(st_l, axis=0).reshape(-1)
    ends = jnp.concatenate(en_l, axis=0).reshape(-1)
    idx = _ec_invert(pos, starts, ends, cap_tot)
    o = _moe_ffn(x2d, idx, wg, wu, wd, layer)
    return _moe_combine_ln(x2d, pos, aff, starts, ends, o, g, b, alpha, split)


def _split2_bf16(w):
    hi = w.astype(BF16)
    lo = (w - hi.astype(F32)).astype(BF16)
    return jnp.concatenate([hi, lo], axis=1)


def kernel(x_prompt, x_sample, attn_w_qkv, attn_q_norm, attn_k_norm, attn_w_o, pool_w, pool_scale, ssd_w_in,
           ssd_conv_w, ssd_conv_b, ssd_dt_bias, ssd_A_log, ssd_D, ssd_norm, ssd_w_out, moe_w_router, moe_w_gate,
           moe_w_up, moe_w_down, ln_g, ln_b):
    bp, s, d = x_prompt.shape
    bs = x_sample.shape[0]
    assert x_sample.shape[1] == s
    bsz = bp + bs
    n = bsz * s
    groups = [(0, bp * s), (bp * s, bs * s)]
    depth = ln_g.shape[0]
    alpha = (2 * depth) ** 0.25
    n_heads = attn_w_o.shape[1] // HEAD_DIM
    n_kv = (attn_w_qkv.shape[2] // HEAD_DIM - n_heads) // 2
    d_inner = ssd_w_out.shape[1]
    n_ssd_heads = ssd_A_log.shape[-1]
    assert d_inner == n_ssd_heads * SSD_HEAD_DIM
    conv_dim = ssd_conv_w.shape[2]
    n_groups = (conv_dim - d_inner) // (2 * D_STATE)
    rope = _rope_tables(s)

    x = jnp.concatenate([x_prompt, x_sample], axis=0).reshape(n, d)
    wg_all, wu_all, wd_all = moe_w_gate.astype(BF16), moe_w_up.astype(BF16), moe_w_down.astype(BF16)
    ia = ip = isd = 0
    for i in range(depth):
        wr2 = _split2_bf16(moe_w_router[i])
        g1, b1, g2, b2 = ln_g[i, 0], ln_b[i, 0], ln_g[i, 1], ln_b[i, 1]
        kind = i % 3
        if kind == 0:
            w_qkv = _permute_qk_columns(attn_w_qkv[ia], n_heads, n_kv).astype(BF16)
            qkv = _qkv_proj(x, w_qkv, attn_q_norm[ia], attn_k_norm[ia], rope, s, n_heads, n_kv)
            qkv3 = qkv.reshape(bsz, s, -1)
            vt = jnp.swapaxes(qkv3[:, :, (n_heads + n_kv) * HEAD_DIM:], 1, 2).reshape(bsz, n_kv, HEAD_DIM, s)
            o = _flash_attention(qkv3, vt, n_heads, n_kv)
            x, aff = _mm_res_ln_router(o.reshape(n, -1), attn_w_o[ia].astype(BF16), x, g1, b1, wr2, alpha)
            ia += 1
        elif kind == 1:
            x3, aff3 = _pool_layer(x.reshape(bsz, s, d), pool_w[ip].astype(BF16), pool_scale[ip], g1, b1, wr2, alpha)
            x, aff = x3.reshape(n, d), aff3.reshape(n, -1)
            ip += 1
        else:
            n_main = d_inner + conv_dim
            zx = _matmul_f32(x, ssd_w_in[isd][:, :n_main].astype(BF16), 1024)
            dt_raw = _matmul_f32(x, ssd_w_in[isd][:, n_main:].astype(BF16), LANES)
            zx3 = zx.reshape(bsz, s, -1)
            xbc = _ssd_conv(zx3, ssd_conv_w[isd], ssd_conv_b[isd], d_inner, conv_dim)
            a_neg = -jnp.exp(ssd_A_log[isd].astype(F32)).reshape(-1)
            dt, ecs, dt_t, ecs_t, tot = _ssd_dt(dt_raw.reshape(bsz, s, -1), ssd_dt_bias[isd].reshape(-1), a_neg, 0)
            yf, yb = _ssd_scan(xbc, dt, ecs, dt_t, ecs_t, tot, ssd_D[isd], d_inner, n_groups)
            yn = _ssd_gate(yf.reshape(n, d_inner), yb.reshape(n, d_inner), zx, ssd_norm[isd])
            x, aff = _mm_res_ln_router(yn, ssd_w_out[isd].astype(BF16), x, g1, b1, wr2, alpha)
            isd += 1
        x = _ec_moe_ln(x, aff, groups, wg_all, wu_all, wd_all, i, g2, b2, alpha,
                       split=bp * s if i == depth - 1 else None)
    y_prompt, y_sample = x
    return y_prompt.reshape(bp, s, d), y_sample.reshape(bs, s, d)
```
